```python
import jax
import jax.numpy as jnp
from jax import lax
import numpy as np

D_MODEL = 1024
BATCH = 4
SEQ = 4096
DEPTH = 2

N_EVEN = (DEPTH + 1) // 2
N_ODD = DEPTH // 2
NORM_EPS = 1e-6

GLA_HEADS = 4
GLA_DK = 64
GLA_DV = 128
GLA_LOWRANK = 16
GLA_GATE_NORM = 16.0
GLA_CHUNK = 16
GLA_WIDTH = GLA_HEADS * GLA_DV
GLA_SPLITS = (GLA_HEADS * GLA_DK, GLA_HEADS * GLA_DK, GLA_WIDTH, GLA_WIDTH, GLA_LOWRANK)
GLA_COLS = 2 * GLA_HEADS * GLA_DK + 2 * GLA_WIDTH + GLA_LOWRANK

RWKV_HEADS = 8
RWKV_N = 64
RWKV_W_LORA = 64
RWKV_A_LORA = 64
RWKV_G_LORA = 128
RWKV_GN_EPS = 64e-5
RWKV_WIDTH = RWKV_HEADS * RWKV_N
RWKV_SPLITS = (RWKV_WIDTH, RWKV_W_LORA, RWKV_WIDTH, RWKV_WIDTH, RWKV_A_LORA, RWKV_G_LORA)
RWKV_COLS = 3 * RWKV_WIDTH + RWKV_W_LORA + RWKV_A_LORA + RWKV_G_LORA
EVEN_COLS = GLA_COLS + RWKV_COLS
EVEN_MIX_WIDTH = GLA_WIDTH + RWKV_WIDTH

NSA_HEADS = 16
NSA_GROUPS = 4
NSA_REP = NSA_HEADS // NSA_GROUPS
NSA_DH = 64
NSA_WIDTH = NSA_HEADS * NSA_DH
NSA_KV = NSA_GROUPS * NSA_DH
NSA_CMP_LEN = 32
NSA_CMP_STRIDE = 16
NSA_CMP_HIDDEN = 64
NSA_SEL_LEN = 64
NSA_N_SEL = 8
NSA_N_LOCAL = 2
NSA_WINDOW = 512
NSA_Q_BLOCK = 128
NSA_FORCE = 100.0
ODD_SPLITS = (NSA_WIDTH, NSA_KV, NSA_KV, NSA_KV, NSA_KV, NSA_KV, NSA_KV, 3 * NSA_HEADS)
ODD_COLS = NSA_WIDTH + 6 * NSA_KV + 3 * NSA_HEADS

FFN_HIDDEN = 2816
CONV_WIDTH = 3

kernel_name = 'hybrid_gla_rwkv7_nsa_convffn_adaln'


def _split(t, sizes):
    return jnp.split(t, np.cumsum(sizes)[:-1].tolist(), axis=-1)


def _rmsnorm(x, g, eps=NORM_EPS):
    x32 = x.astype(jnp.float32)
    y = x32 * lax.rsqrt(jnp.mean(x32 * x32, axis=-1, keepdims=True) + eps)
    return (y * g).astype(x.dtype)


def _masked_softmax(s, mask):
    s = jnp.where(mask, s, -jnp.inf)
    m = jnp.max(s, axis=-1, keepdims=True)
    m = jnp.where(jnp.isfinite(m), m, 0.0)
    e = jnp.exp(s - m)
    den = jnp.sum(e, axis=-1, keepdims=True)
    return e / jnp.where(den > 0, den, 1.0)


def _gla_chunked(q, k, v, log_a):
    B, T, H, DK = q.shape
    DV = v.shape[-1]
    C = GLA_CHUNK
    N = T // C

    def chunks(t):
        return t.astype(jnp.float32).reshape(B, N, C, H, t.shape[-1]).transpose(0, 3, 1, 2, 4)

    q, k, v, log_a = chunks(q * DK ** -0.5), chunks(k), chunks(v), chunks(log_a)
    b = jnp.cumsum(log_a, axis=3)
    causal = jnp.tril(jnp.ones((C, C), dtype=bool))
    rel = jnp.where(causal[:, :, None], b[:, :, :, :, None, :] - b[:, :, :, None, :, :], -jnp.inf)
    scores = jnp.einsum('bhntk,bhnsk,bhntsk->bhnts', q, k, jnp.exp(rel))
    o_intra = jnp.einsum('bhnts,bhnsv->bhntv', scores, v)
    b_end = b[:, :, :, -1]
    upd = jnp.einsum('bhnsk,bhnsv->bhnkv', k * jnp.exp(b_end[:, :, :, None] - b), v)

    def step(state, inp):
        decay, u = inp
        return decay[..., None] * state + u, state

    s0 = jnp.zeros((B, H, DK, DV), jnp.float32)
    _, s_prev = lax.scan(step, s0, (jnp.moveaxis(jnp.exp(b_end), 2, 0), jnp.moveaxis(upd, 2, 0)))
    s_prev = jnp.moveaxis(s_prev, 0, 2)
    o_inter = jnp.einsum('bhntk,bhnkv->bhntv', q * jnp.exp(b), s_prev)
    return (o_intra + o_inter).transpose(0, 2, 3, 1, 4).reshape(B, T, H, DV)


def _rwkv7_scan(r, w, k, v, kk, a):
    B, T, H, N = r.shape

    def step(state, inp):
        r_t, w_t, k_t, v_t, kk_t, a_t = inp
        removed = jnp.einsum('bhvk,bhk->bhv', state, kk_t)
        state = (state * w_t[:, :, None, :]
                 - removed[..., None] * (kk_t * a_t)[:, :, None, :]
                 + v_t[..., None] * k_t[:, :, None, :])
        return state, jnp.einsum('bhvk,bhk->bhv', state, r_t)

    s0 = jnp.zeros((B, H, N, N), jnp.float32)
    _, y = lax.scan(step, s0, tuple(jnp.moveaxis(t, 1, 0) for t in (r, w, k, v, kk, a)))
    return jnp.moveaxis(y, 0, 1)


def _even_mixer(h, w_in, shift_mu, a_up, a_b, gla_g, w0, w2, a0, a2, g2, k_k, k_a, r_k,
                gn_w, gn_b, w_out):
    B, T, _ = h.shape
    hd = lambda t, d: t.reshape(B, T, -1, d)
    p = (h @ w_in).astype(jnp.float32)
    p_gla, p_rw = p[..., :GLA_COLS], p[..., GLA_COLS:]
    q, k, v, og, lr = _split(p_gla, GLA_SPLITS)
    log_a = jax.nn.log_sigmoid(lr @ a_up + a_b) / GLA_GATE_NORM
    o = _gla_chunked(hd(q, GLA_DK), hd(k, GLA_DK), hd(v, GLA_DV), hd(log_a, GLA_DK))
    o_gla = (_rmsnorm(o, gla_g) * jax.nn.silu(hd(og, GLA_DV))).reshape(B, T, GLA_WIDTH)
    p_prev = jnp.concatenate([jnp.zeros_like(p_rw[:, :1]), p_rw[:, :-1]], axis=1)
    p_rw = p_rw + (p_prev - p_rw) * shift_mu
    r, wl, k, v, al, gl = _split(p_rw, RWKV_SPLITS)
    w = jnp.exp(-jnp.exp(-jax.nn.softplus(-(w0 + jnp.tanh(wl) @ w2)) - 0.5))
    a = jax.nn.sigmoid(a0 + al @ a2)
    g = jax.nn.sigmoid(gl) @ g2
    kk = hd(k * k_k, RWKV_N)
    kk = kk * lax.rsqrt(jnp.maximum(jnp.sum(kk * kk, axis=-1, keepdims=True), 1e-24))
    k = k * (1.0 + (a - 1.0) * k_a)
    rh, kh, vh = hd(r, RWKV_N), hd(k, RWKV_N), hd(v, RWKV_N)
    y = _rwkv7_scan(rh, hd(w, RWKV_N), kh, vh, kk, hd(a, RWKV_N))
    mu = jnp.mean(y, axis=-1, keepdims=True)
    var = jnp.mean(jnp.square(y - mu), axis=-1, keepdims=True)
    y = ((y - mu) * lax.rsqrt(var + RWKV_GN_EPS)).reshape(B, T, RWKV_WIDTH) * gn_w + gn_b
    bonus = (jnp.sum(rh * kh * r_k, axis=-1, keepdims=True) * vh).reshape(B, T, RWKV_WIDTH)
    o_rw = (y + bonus) * g
    out = jnp.concatenate([o_gla, o_rw], axis=-1) @ w_out
    return out.astype(h.dtype)


def _compress(t, pe, w1, w2, cmp_start):
    B, T, G, DH = t.shape
    idx = cmp_start[:, None] + jnp.arange(NSA_CMP_LEN)[None, :]
    blocks = t[:, idx] + pe[:, None, :]
    flat = blocks.transpose(0, 1, 3, 2, 4).reshape(B, idx.shape[0], G, NSA_CMP_LEN * DH)
    return jax.nn.gelu(flat @ w1) @ w2


def _nsa_mixer(h, w_in, pe_k, w1_k, w2_k, pe_v, w1_v, w2_v, w_out):
    B, T, _ = h.shape
    G, R, DH = NSA_GROUPS, NSA_REP, NSA_DH
    f32 = jnp.float32
    p = h @ w_in
    q, kc, vc, ks, vs, kw, vw, gt = _split(p, ODD_SPLITS)
    q = q.reshape(B, T, G, R, DH) * DH ** -0.5
    kc, vc, ks, vs, kw, vw = (t.reshape(B, T, G, DH) for t in (kc, vc, ks, vs, kw, vw))
    gates = jax.nn.sigmoid(gt.astype(f32)).reshape(B, T, G, R, 3)
    slopes = (2.0 ** (-8.0 * jnp.arange(1, NSA_HEADS + 1, dtype=f32) / NSA_HEADS)).reshape(G, R)
    n_cmp = (T - NSA_CMP_LEN) // NSA_CMP_STRIDE + 1
    cmp_start = jnp.arange(n_cmp) * NSA_CMP_STRIDE
    cmp_end = cmp_start + NSA_CMP_LEN - 1
    kc = _compress(kc, pe_k, w1_k, w2_k, cmp_start)
    vc = _compress(vc, pe_v, w1_v, w2_v, cmp_start)
    n_sel = T // NSA_SEL_LEN
    k_sel = min(NSA_N_SEL, n_sel)
    sel_start = jnp.arange(n_sel) * NSA_SEL_LEN
    overlap = ((cmp_start[:, None] <= sel_start[None, :] + NSA_SEL_LEN - 1)
               & (cmp_end[:, None] >= sel_start[None, :])).astype(f32)
    ks_blk = ks.reshape(B, n_sel, NSA_SEL_LEN, G, DH).transpose(0, 3, 1, 2, 4)
    vs_blk = vs.reshape(B, n_sel, NSA_SEL_LEN, G, DH).transpose(0, 3, 1, 2, 4)
    pad = ((0, 0), (NSA_WINDOW, 0), (0, 0), (0, 0))
    kw_pad, vw_pad = jnp.pad(kw, pad), jnp.pad(vw, pad)
    b_idx = jnp.arange(B)[:, None, None, None]
    g_idx = jnp.arange(G)[None, :, None, None]
    QB = NSA_Q_BLOCK

    def query_block(qi):
        q0 = qi * QB
        qb = lax.dynamic_slice_in_dim(q, q0, QB, axis=1)
        t = q0 + jnp.arange(QB)
        dist_c = t[:, None] - cmp_end[None, :]
        s = jnp.einsum('bqgrd,bngd->bgrqn', qb, kc).astype(f32) - slopes[:, :, None, None] * dist_c
        p_c = _masked_softmax(s, dist_c >= 0)
        o_c = jnp.einsum('bgrqn,bngd->bqgrd', p_c.astype(vc.dtype), vc)
        imp = jnp.einsum('bgrqn,nj->bgqj', p_c, overlap)
        blk = jnp.arange(n_sel)
        ahead = (t // NSA_SEL_LEN)[:, None] - blk[None, :]
        valid = ahead >= 0
        forced = (blk[None, :] == 0) | (valid & (ahead < NSA_N_LOCAL))
        score = jnp.where(valid, imp + jnp.where(forced, NSA_FORCE, 0.0), -NSA_FORCE)
        _, sel = lax.top_k(score, k_sel)
        tok = (sel[..., None] * NSA_SEL_LEN + jnp.arange(NSA_SEL_LEN)).reshape(B, G, QB, k_sel * NSA_SEL_LEN)
        dist_s = t[:, None] - tok
        k_g = ks_blk[b_idx, g_idx, sel].reshape(B, G, QB, k_sel * NSA_SEL_LEN, DH)
        v_g = vs_blk[b_idx, g_idx, sel].reshape(B, G, QB, k_sel * NSA_SEL_LEN, DH)
        s = (jnp.einsum('bqgrd,bgqmd->bgrqm', qb, k_g).astype(f32)
             - slopes[:, :, None, None] * dist_s[:, :, None])
        p_s = _masked_softmax(s, dist_s[:, :, None] >= 0)
        o_s = jnp.einsum('bgrqm,bgqmd->bqgrd', p_s.astype(v_g.dtype), v_g)
        k_w = lax.dynamic_slice_in_dim(kw_pad, q0, QB + NSA_WINDOW, axis=1)
        v_w = lax.dynamic_slice_in_dim(vw_pad, q0, QB + NSA_WINDOW, axis=1)
        pos = q0 - NSA_WINDOW + jnp.arange(QB + NSA_WINDOW)
        dist_w = t[:, None] - pos[None, :]
        mask_w = (dist_w >= 0) & (dist_w < NSA_WINDOW) & (pos[None, :] >= 0)
        s = jnp.einsum('bqgrd,bkgd->bgrqk', qb, k_w).astype(f32) - slopes[:, :, None, None] * dist_w
        p_w = _masked_softmax(s, mask_w)
        o_w = jnp.einsum('bgrqk,bkgd->bqgrd', p_w.astype(v_w.dtype), v_w)
        gb = lax.dynamic_slice_in_dim(gates, q0, QB, axis=1)
        return gb[..., 0:1] * o_c + gb[..., 1:2] * o_s + gb[..., 2:3] * o_w

    out = lax.map(query_block, jnp.arange(T // QB))
    out = jnp.moveaxis(out, 0, 1).reshape(B, T, NSA_WIDTH)
    return (out @ w_out).astype(h.dtype)


def _conv_ffn(h, w_up, conv_w, conv_b, w_down):
    u, v = jnp.split(h @ w_up, 2, axis=-1)
    u = lax.conv_general_dilated(u, conv_w[:, None, :].astype(u.dtype), window_strides=(1,),
                                 padding=[(CONV_WIDTH - 1, 0)],
                                 dimension_numbers=('NWC', 'WIO', 'NWC'),
                                 feature_group_count=FFN_HIDDEN) + conv_b
    return ((jax.nn.gelu(u) * v) @ w_down).astype(h.dtype)


def setup_inputs(seed: int = 0) -> dict:
    key = jax.random.key(seed)
    keys = iter(jax.random.split(key, 40))
    nrm = lambda shape, scale: jax.random.normal(next(keys), shape, jnp.float32) * scale
    uni = lambda shape, lo, hi: jax.random.uniform(next(keys), shape, jnp.float32, lo, hi)
    D, F = D_MODEL, FFN_HIDDEN
    L = NSA_CMP_LEN * NSA_DH
    return {
        'x': nrm((BATCH, SEQ, D), 1.0),
        'c': nrm((BATCH, D), 1.0),
        'ada_w': nrm((DEPTH, D, 6 * D), D ** -0.5),
        'ada_b': nrm((DEPTH, 6 * D), 0.02),
        'norm1_g': 1.0 + nrm((DEPTH, D), 0.02),
        'norm2_g': 1.0 + nrm((DEPTH, D), 0.02),
        'ffn_w_up': nrm((DEPTH, D, 2 * F), D ** -0.5),
        'ffn_conv_w': nrm((DEPTH, CONV_WIDTH, F), CONV_WIDTH ** -0.5),
        'ffn_conv_b': nrm((DEPTH, F), 0.02),
        'ffn_w_down': nrm((DEPTH, F, D), F ** -0.5),
        'ev_w_in': nrm((N_EVEN, D, EVEN_COLS), D ** -0.5),
        'ev_shift_mu': uni((N_EVEN, RWKV_COLS), 0.0, 1.0),
        'gla_a_up': nrm((N_EVEN, GLA_LOWRANK, GLA_HEADS * GLA_DK), GLA_LOWRANK ** -0.5),
        'gla_a_b': nrm((N_EVEN, GLA_HEADS * GLA_DK), 0.1),
        'gla_norm_g': 1.0 + nrm((N_EVEN, GLA_DV), 0.02),
        'rw_w0': uni((N_EVEN, RWKV_WIDTH), -5.0, 1.0),
        'rw_w2': nrm((N_EVEN, RWKV_W_LORA, RWKV_WIDTH), RWKV_W_LORA ** -0.5),
        'rw_a0': nrm((N_EVEN, RWKV_WIDTH), 0.1),
        'rw_a2': nrm((N_EVEN, RWKV_A_LORA, RWKV_WIDTH), RWKV_A_LORA ** -0.5),
        'rw_g2': nrm((N_EVEN, RWKV_G_LORA, RWKV_WIDTH), RWKV_G_LORA ** -0.5),
        'rw_k_k': 0.85 + nrm((N_EVEN, RWKV_WIDTH), 0.02),
        'rw_k_a': 1.0 + nrm((N_EVEN, RWKV_WIDTH), 0.02),
        'rw_r_k': nrm((N_EVEN, RWKV_HEADS, RWKV_N), 0.1),
        'rw_gn_w': 1.0 + nrm((N_EVEN, RWKV_WIDTH), 0.02),
        'rw_gn_b': nrm((N_EVEN, RWKV_WIDTH), 0.02),
        'ev_w_out': nrm((N_EVEN, EVEN_MIX_WIDTH, D), EVEN_MIX_WIDTH ** -0.5),
        'od_w_in': nrm((N_ODD, D, ODD_COLS), D ** -0.5),
        'cmp_pe_k': nrm((N_ODD, NSA_CMP_LEN, NSA_DH), 0.1),
        'cmp_w1_k': nrm((N_ODD, L, NSA_CMP_HIDDEN), L ** -0.5),
        'cmp_w2_k': nrm((N_ODD, NSA_CMP_HIDDEN, NSA_DH), NSA_CMP_HIDDEN ** -0.5),
        'cmp_pe_v': nrm((N_ODD, NSA_CMP_LEN, NSA_DH), 0.1),
        'cmp_w1_v': nrm((N_ODD, L, NSA_CMP_HIDDEN), L ** -0.5),
        'cmp_w2_v': nrm((N_ODD, NSA_CMP_HIDDEN, NSA_DH), NSA_CMP_HIDDEN ** -0.5),
        'od_w_out': nrm((N_ODD, NSA_WIDTH, D), NSA_WIDTH ** -0.5),
        'final_norm_g': 1.0 + nrm((D,), 0.02),
    }


def reference(x, c, ada_w, ada_b, norm1_g, norm2_g, ffn_w_up, ffn_conv_w, ffn_conv_b, ffn_w_down,
              ev_w_in, ev_shift_mu, gla_a_up, gla_a_b, gla_norm_g, rw_w0, rw_w2, rw_a0, rw_a2,
              rw_g2, rw_k_k, rw_k_a, rw_r_k, rw_gn_w, rw_gn_b, ev_w_out,
              od_w_in, cmp_pe_k, cmp_w1_k, cmp_w2_k, cmp_pe_v, cmp_w1_v, cmp_w2_v, od_w_out,
              final_norm_g):
    cond = jax.nn.silu(c)
    for layer in range(DEPTH):
        mod = cond @ ada_w[layer] + ada_b[layer]
        sh1, sc1, g1, sh2, sc2, g2 = jnp.split(mod[:, None, :], 6, axis=-1)
        hn = _rmsnorm(x, norm1_g[layer]) * (1.0 + sc1) + sh1
        i = layer // 2
        if layer % 2 == 0:
            mix = _even_mixer(hn, ev_w_in[i], ev_shift_mu[i], gla_a_up[i], gla_a_b[i], gla_norm_g[i],
                              rw_w0[i], rw_w2[i], rw_a0[i], rw_a2[i], rw_g2[i], rw_k_k[i], rw_k_a[i],
                              rw_r_k[i], rw_gn_w[i], rw_gn_b[i], ev_w_out[i])
        else:
            mix = _nsa_mixer(hn, od_w_in[i], cmp_pe_k[i], cmp_w1_k[i], cmp_w2_k[i],
                             cmp_pe_v[i], cmp_w1_v[i], cmp_w2_v[i], od_w_out[i])
        x = x + g1 * mix
        hn = _rmsnorm(x, norm2_g[layer]) * (1.0 + sc2) + sh2
        x = x + g2 * _conv_ffn(hn, ffn_w_up[layer], ffn_conv_w[layer], ffn_conv_b[layer], ffn_w_down[layer])
    return _rmsnorm(x, final_norm_g)
```

```python
import functools

import numpy as np
import jax
import jax.numpy as jnp
from jax import lax
from jax.experimental import pallas as pl
from jax.experimental.pallas import tpu as pltpu

F32 = jnp.float32
BF16 = jnp.bfloat16

D_MODEL = 1024
NORM_EPS = 1e-6
GLA_HEADS, GLA_DK, GLA_DV, GLA_LOWRANK, GLA_GATE_NORM, GLA_CHUNK = 4, 64, 128, 16, 16.0, 16
RWKV_HEADS, RWKV_N, RWKV_GN_EPS = 8, 64, 64e-5
RWKV_W_LORA, RWKV_A_LORA, RWKV_G_LORA = 64, 64, 128
RWKV_CHUNK = 16
NSA_HEADS, NSA_GROUPS, NSA_DH = 16, 4, 64
NSA_REP = NSA_HEADS // NSA_GROUPS
NSA_CMP_LEN, NSA_CMP_STRIDE, NSA_CMP_HIDDEN = 32, 16, 64
NSA_SEL_LEN, NSA_N_SEL, NSA_N_LOCAL, NSA_WINDOW, NSA_Q_BLOCK, NSA_FORCE = 64, 8, 2, 512, 128, 100.0
FFN_HIDDEN = 2816

LANES = 128
VMEM_LIMIT = 56 * 1024 * 1024
NEG_BIG = -1e30


def _cparams(sem):
    return pltpu.CompilerParams(dimension_semantics=sem, vmem_limit_bytes=VMEM_LIMIT)


def _dot(a, b):
    return jnp.dot(a, b, preferred_element_type=F32)


def _dot_nt(a, b):
    return lax.dot_general(a, b, (((1,), (1,)), ((), ())), preferred_element_type=F32)


def _mm(a, b):
    return _dot(a.astype(BF16), b.astype(BF16))


def _mm_nt(a, b):
    return _dot_nt(a.astype(BF16), b.astype(BF16))


def _split3(a):
    a1 = a.astype(BF16)
    r1 = a - a1.astype(F32)
    a2 = r1.astype(BF16)
    a3 = (r1 - a2.astype(F32)).astype(BF16)
    return a1, a2, a3


def _mm_sel(a, b01):
    a1, a2, a3 = _split3(a)
    b = b01.astype(BF16)
    return _dot(a1, b) + _dot(a2, b) + _dot(a3, b)


def _sel_mm(a01, b):
    b1, b2, b3 = _split3(b)
    a = a01.astype(BF16)
    return _dot(a, b1) + _dot(a, b2) + _dot(a, b3)


def _mm3(a, b):
    a1 = a.astype(BF16)
    a2 = (a - a1.astype(F32)).astype(BF16)
    b1 = b.astype(BF16)
    b2 = (b - b1.astype(F32)).astype(BF16)
    return _dot(a1, b1) + _dot(a1, b2) + _dot(a2, b1)


def _mm3_nt(a, b):
    a1 = a.astype(BF16)
    a2 = (a - a1.astype(F32)).astype(BF16)
    b1 = b.astype(BF16)
    b2 = (b - b1.astype(F32)).astype(BF16)
    return _dot_nt(a1, b1) + _dot_nt(a1, b2) + _dot_nt(a2, b1)


def _iota(shape, dim):
    return lax.broadcasted_iota(jnp.int32, shape, dim)


def _sigmoid(x):
    return 1.0 / (1.0 + jnp.exp(-x))


def _softplus(x):
    return jnp.maximum(x, 0.0) + jnp.log(1.0 + jnp.exp(-jnp.abs(x)))


def _gelu_tanh(x):
    return x * (0.5 * (1.0 + jnp.tanh(0.7978845608028654 * (x + 0.044715 * (x * x * x)))))


def _rms(x, eps):
    return x * lax.rsqrt(jnp.mean(x * x, axis=-1, keepdims=True) + eps)


def _mod_kernel(c_ref, w_ref, b_ref, o_ref):
    c = c_ref[...]
    cond = c * _sigmoid(c)
    o_ref[0] = _mm3(cond, w_ref[0]) + b_ref[0]


def _ada_mod(c, ada_w, ada_b):
    depth, d, n = ada_w.shape
    bsz = c.shape[0]
    rows = 8
    c8 = jnp.zeros((rows, d), F32).at[:bsz].set(c)
    tn = 1536
    out = pl.pallas_call(
        _mod_kernel,
        grid=(depth, n // tn),
        in_specs=[
            pl.BlockSpec((rows, d), lambda l, j: (0, 0)),
            pl.BlockSpec((1, d, tn), lambda l, j: (l, 0, j)),
            pl.BlockSpec((1, 1, tn), lambda l, j: (l, 0, j)),
        ],
        out_specs=pl.BlockSpec((1, rows, tn), lambda l, j: (l, 0, j)),
        out_shape=jax.ShapeDtypeStruct((depth, rows, n), F32),
        compiler_params=_cparams(("parallel", "parallel")),
        name="ada_mod",
    )(c8, ada_w, ada_b.reshape(depth, 1, n))
    return out[:, :bsz]


def _norm_proj_kernel(x_ref, g_ref, sc_ref, sh_ref, w_ref, o_ref, hn_ref):
    @pl.when(pl.program_id(2) == 0)
    def _():
        hn = _rms(x_ref[0], NORM_EPS) * g_ref[...]
        hn_ref[...] = (hn * (1.0 + sc_ref[0]) + sh_ref[0]).astype(BF16)

    o_ref[0] = _dot(hn_ref[...], w_ref[...])


def _norm_proj(x, g, sc, sh, w, *, tm=512, n_split=3):
    bsz, t, d = x.shape
    n = w.shape[1]
    tn = n // n_split
    return pl.pallas_call(
        _norm_proj_kernel,
        grid=(bsz, t // tm, n_split),
        in_specs=[
            pl.BlockSpec((1, tm, d), lambda b, i, j: (b, i, 0)),
            pl.BlockSpec((1, d), lambda b, i, j: (0, 0)),
            pl.BlockSpec((1, 1, d), lambda b, i, j: (b, 0, 0)),
            pl.BlockSpec((1, 1, d), lambda b, i, j: (b, 0, 0)),
            pl.BlockSpec((d, tn), lambda b, i, j: (0, j)),
        ],
        out_specs=pl.BlockSpec((1, tm, tn), lambda b, i, j: (b, i, j)),
        out_shape=jax.ShapeDtypeStruct((bsz, t, n), F32),
        scratch_shapes=[pltpu.VMEM((tm, d), BF16)],
        compiler_params=_cparams(("parallel", "parallel", "arbitrary")),
        name="norm_proj",
    )(x, g.reshape(1, d), sc, sh, w)


def _out_proj_kernel(m_ref, w_ref, x_ref, gate_ref, o_ref):
    o_ref[0] = x_ref[0] + gate_ref[0] * _mm(m_ref[0], w_ref[...])


def _out_proj(mix, w, x, gate, *, tm=512):
    bsz, t, d = x.shape
    kdim = mix.shape[-1]
    return pl.pallas_call(
        _out_proj_kernel,
        grid=(bsz, t // tm),
        in_specs=[
            pl.BlockSpec((1, tm, kdim), lambda b, i: (b, i, 0)),
            pl.BlockSpec((kdim, d), lambda b, i: (0, 0)),
            pl.BlockSpec((1, tm, d), lambda b, i: (b, i, 0)),
            pl.BlockSpec((1, 1, d), lambda b, i: (b, 0, 0)),
        ],
        out_specs=pl.BlockSpec((1, tm, d), lambda b, i: (b, i, 0)),
        out_shape=jax.ShapeDtypeStruct((bsz, t, d), F32),
        compiler_params=_cparams(("parallel", "parallel")),
        name="out_proj",
    )(mix, w, x, gate)


def _ffn_kernel(x_ref, g_ref, sc_ref, sh_ref, gate_ref, wu_ref, wv_ref, cw_ref, cb_ref, wd_ref, fg_ref,
                o_ref, hn_ref, acc_ref, halo_ref, *, final_norm):
    ti = pl.program_id(1)
    fj = pl.program_id(2)
    tm, fk = acc_ref.shape[0], wu_ref.shape[1]

    @pl.when(fj == 0)
    def _():
        hn = _rms(x_ref[0], NORM_EPS) * g_ref[...]
        hn_ref[...] = (hn * (1.0 + sc_ref[0]) + sh_ref[0]).astype(BF16)
        acc_ref[...] = jnp.zeros_like(acc_ref)

    @pl.when(ti == 0)
    def _():
        halo_ref[fj] = jnp.zeros((8, fk), F32)

    hn = hn_ref[...]
    u = _dot(hn, wu_ref[...])
    v = _dot(hn, wv_ref[...])
    prev = halo_ref[fj]
    row = _iota((tm, fk), 0)
    u1 = jnp.where(row == 0, prev[7:8], pltpu.roll(u, 1, 0))
    u2 = jnp.where(row == 0, prev[6:7], jnp.where(row == 1, prev[7:8], pltpu.roll(u, 2, 0)))
    halo_ref[fj] = u[tm - 8:tm]
    cw = cw_ref[...]
    uc = cw[0:1] * u2 + cw[1:2] * u1 + cw[2:3] * u + cb_ref[...]
    h = _gelu_tanh(uc) * v
    acc_ref[...] += _dot(h.astype(BF16), wd_ref[...])

    @pl.when(fj == pl.num_programs(2) - 1)
    def _():
        y = x_ref[0] + gate_ref[0] * acc_ref[...]
        if final_norm:
            y = _rms(y, NORM_EPS) * fg_ref[...]
        o_ref[0] = y


def _conv_ffn(x, g, sc, sh, gate, w_up, conv_w, conv_b, w_down, final_g, *, final_norm, tm=512, fk=256):
    bsz, t, d = x.shape
    f = w_down.shape[0]
    nf = f // fk
    kern = functools.partial(_ffn_kernel, final_norm=final_norm)
    return pl.pallas_call(
        kern,
        grid=(bsz, t // tm, nf),
        in_specs=[
            pl.BlockSpec((1, tm, d), lambda b, i, j: (b, i, 0)),
            pl.BlockSpec((1, d), lambda b, i, j: (0, 0)),
            pl.BlockSpec((1, 1, d), lambda b, i, j: (b, 0, 0)),
            pl.BlockSpec((1, 1, d), lambda b, i, j: (b, 0, 0)),
            pl.BlockSpec((1, 1, d), lambda b, i, j: (b, 0, 0)),
            pl.BlockSpec((d, fk), lambda b, i, j: (0, j)),
            pl.BlockSpec((d, fk), lambda b, i, j: (0, j + nf)),
            pl.BlockSpec((3, fk), lambda b, i, j: (0, j)),
            pl.BlockSpec((1, fk), lambda b, i, j: (0, j)),
            pl.BlockSpec((fk, d), lambda b, i, j: (j, 0)),
            pl.BlockSpec((1, d), lambda b, i, j: (0, 0)),
        ],
        out_specs=pl.BlockSpec((1, tm, d), lambda b, i, j: (b, i, 0)),
        out_shape=jax.ShapeDtypeStruct((bsz, t, d), F32),
        scratch_shapes=[pltpu.VMEM((tm, d), BF16), pltpu.VMEM((tm, d), F32), pltpu.VMEM((nf, 8, fk), F32)],
        compiler_params=_cparams(("parallel", "arbitrary", "arbitrary")),
        name="conv_ffn",
    )(x, g.reshape(1, d), sc, sh, gate, w_up, w_up, conv_w, conv_b.reshape(1, f), w_down, final_g.reshape(1, d))


def _gla_kernel(q_ref, k_ref, v_ref, og_ref, lr_ref, aup_ref, ab_ref, gg_ref, o_ref,
                st_ref, b_scr, bend_scr, q_scr, k_scr, kd_scr, qg_scr, vt_scr, o_scr):
    tb = q_ref.shape[1]
    cs = GLA_CHUNK
    n_chunks = tb // cs

    @pl.when(pl.program_id(1) == 0)
    def _():
        st_ref[...] = jnp.zeros_like(st_ref)

    z = _mm3(lr_ref[0], aup_ref[...]) + ab_ref[...]
    la = -_softplus(-z) * (1.0 / GLA_GATE_NORM)
    rr, cc = _iota((tb, tb), 0), _iota((tb, tb), 1)
    same = (rr // cs) == (cc // cs)
    b = _sel_mm(jnp.where(same & (cc <= rr), 1.0, 0.0), la)
    bend = _sel_mm(jnp.where(same, 1.0, 0.0), la)
    q = q_ref[0] * (GLA_DK ** -0.5)
    k = k_ref[0]
    b_scr[...] = b
    bend_scr[...] = bend
    q_scr[...] = q
    k_scr[...] = k
    kd_scr[...] = (k * jnp.exp(bend - b)).astype(BF16)
    qg_scr[...] = q * jnp.exp(b)
    v_all = v_ref[0]
    for h in range(GLA_HEADS):
        vt_scr[h] = v_all[:, h * GLA_DV:(h + 1) * GLA_DV].T.astype(BF16)

    lane_c = _iota((cs, LANES), 1)
    row_c = _iota((cs, LANES), 0)
    lane_t = _iota((tb, LANES), 1)
    head_rows = _iota((LANES, LANES), 0) // GLA_DK

    def chunk(c, carry):
        r0 = pl.multiple_of(c * cs, cs)
        tmask = (lane_t // cs) == c
        for p in range(GLA_HEADS // 2):
            lanes = slice(p * LANES, (p + 1) * LANES)
            bc = b_scr[pl.ds(r0, cs), lanes]
            qc = q_scr[pl.ds(r0, cs), lanes]
            kc = k_scr[pl.ds(r0, cs), lanes]
            blocks = []
            for s in range(cs):
                m = row_c >= s
                rel = jnp.where(m, bc - bc[s:s + 1], 0.0)
                blocks.append(jnp.where(m, qc * kc[s:s + 1] * jnp.exp(rel), 0.0))
            a_st = jnp.concatenate(blocks, axis=0)
            qgc = qg_scr[pl.ds(r0, cs), lanes]
            lhs = jnp.concatenate([jnp.where(lane_c < GLA_DK, qgc, 0.0),
                                   jnp.where(lane_c >= GLA_DK, qgc, 0.0)], axis=0)
            st = st_ref[p]
            o_inter = _mm_nt(lhs, st)
            for hh in range(2):
                h = 2 * p + hh
                zsum = _mm_sel(a_st, jnp.where(head_rows == hh, 1.0, 0.0))
                vc = v_ref[0, pl.ds(r0, cs), h * GLA_DV:(h + 1) * GLA_DV]
                o_h = o_inter[hh * cs:(hh + 1) * cs]
                for s in range(cs):
                    o_h = o_h + zsum[s * cs:(s + 1) * cs] * vc[s:s + 1]
                o_scr[pl.ds(r0, cs), h * GLA_DV:(h + 1) * GLA_DV] = o_h
            lhs_u = jnp.concatenate([jnp.where(tmask, vt_scr[2 * p], 0.0).astype(BF16),
                                     jnp.where(tmask, vt_scr[2 * p + 1], 0.0).astype(BF16)], axis=0)
            upd = _dot(lhs_u, kd_scr[:, lanes])
            upd = jnp.where(lane_t < GLA_DK, upd[:GLA_DV], upd[GLA_DV:])
            decay = jnp.exp(bend_scr[pl.ds(r0, 1), lanes])
            st_ref[p] = st * decay + upd
        return carry

    lax.fori_loop(0, n_chunks, chunk, 0)

    og = og_ref[0]
    for h in range(GLA_HEADS):
        sl = slice(h * GLA_DV, (h + 1) * GLA_DV)
        gate = og[:, sl]
        o_ref[0, :, sl] = (_rms(o_scr[:, sl], NORM_EPS) * gg_ref[...] * (gate * _sigmoid(gate))).astype(o_ref.dtype)


def _gla(p, a_up_pad, a_b, gla_g, *, tb=128):
    bsz, t, _ = p.shape
    qk = GLA_HEADS * GLA_DK
    vw = GLA_HEADS * GLA_DV
    assert tb == LANES and GLA_DV == LANES
    return pl.pallas_call(
        _gla_kernel,
        grid=(bsz, t // tb),
        in_specs=[
            pl.BlockSpec((1, tb, qk), lambda b, i: (b, i, 0)),
            pl.BlockSpec((1, tb, qk), lambda b, i: (b, i, 1)),
            pl.BlockSpec((1, tb, vw), lambda b, i: (b, i, 1)),
            pl.BlockSpec((1, tb, vw), lambda b, i: (b, i, 2)),
            pl.BlockSpec((1, tb, LANES), lambda b, i: (b, i, 24)),
            pl.BlockSpec((LANES, qk), lambda b, i: (0, 0)),
            pl.BlockSpec((1, qk), lambda b, i: (0, 0)),
            pl.BlockSpec((1, GLA_DV), lambda b, i: (0, 0)),
        ],
        out_specs=pl.BlockSpec((1, tb, vw), lambda b, i: (b, i, 0)),
        out_shape=jax.ShapeDtypeStruct((bsz, t, vw), BF16),
        scratch_shapes=[
            pltpu.VMEM((GLA_HEADS // 2, GLA_DV, LANES), F32),
            pltpu.VMEM((tb, qk), F32), pltpu.VMEM((tb, qk), F32),
            pltpu.VMEM((tb, qk), F32), pltpu.VMEM((tb, qk), F32),
            pltpu.VMEM((tb, qk), BF16), pltpu.VMEM((tb, qk), F32),
            pltpu.VMEM((GLA_HEADS, GLA_DV, tb), BF16),
            pltpu.VMEM((tb, vw), F32),
        ],
        compiler_params=_cparams(("parallel", "arbitrary")),
        name="gla",
    )(p, p, p, p, p, a_up_pad, a_b.reshape(1, qk), gla_g.reshape(1, GLA_DV))


def _rwkv_kernel(r_ref, k_ref, v_ref, wa_ref, gl_ref, mur_ref, muk_ref, muv_ref, muwa_ref, mugl_ref,
                 w0_ref, w2_ref, a0_ref, a2_ref, g2_ref, kkw_ref, ka_ref, rk_ref, gnw_ref, gnb_ref,
                 o_ref,
                 lr_scr, lk_scr, lv_scr, lwa_scr, lgl_scr, s_ref,
                 lw_p, r_p, kk_p, be_p, k2_p, v_p, y_p,
                 pp_scr, y0_scr, ge_scr, u0t_scr, vt_scr, mbt_scr, ak_scr):
    tb = r_ref.shape[1]
    cs = RWKV_CHUNK
    n_chunks = tb // cs
    n_pairs = RWKV_HEADS // 2
    ti = pl.program_id(1)
    lasts = (lr_scr, lk_scr, lv_scr, lwa_scr, lgl_scr)

    @pl.when(ti == 0)
    def _():
        s_ref[...] = jnp.zeros_like(s_ref)
        for ref in lasts:
            ref[...] = jnp.zeros_like(ref)

    def shifted(x_ref, last_ref, mu_ref):
        x = x_ref[0]
        row = _iota(x.shape, 0)
        prev = jnp.where(row == 0, last_ref[7:8], pltpu.roll(x, 1, 0))
        last_ref[...] = x[tb - 8:tb]
        return x + (prev - x) * mu_ref[...]

    r = shifted(r_ref, lr_scr, mur_ref)
    k = shifted(k_ref, lk_scr, muk_ref)
    v = shifted(v_ref, lv_scr, muv_ref)
    wa = shifted(wa_ref, lwa_scr, muwa_ref)
    gl = shifted(gl_ref, lgl_scr, mugl_ref)

    logw = -jnp.exp(-_softplus(-(w0_ref[...] + _mm3(jnp.tanh(wa), w2_ref[...]))) - 0.5)
    a = _sigmoid(a0_ref[...] + _mm3(wa, a2_ref[...]))
    g = _mm3(_sigmoid(gl), g2_ref[...])
    seg = jnp.where((_iota((LANES, LANES), 0) // RWKV_N) == (_iota((LANES, LANES), 1) // RWKV_N), 1.0, 0.0)

    def segsum(x):
        return jnp.concatenate([_mm_sel(x[:, i * LANES:(i + 1) * LANES], seg) for i in range(n_pairs)], axis=1)

    kk = k * kkw_ref[...]
    kk = kk * lax.rsqrt(jnp.maximum(segsum(kk * kk), 1e-24))
    k2 = k * (1.0 + (a - 1.0) * ka_ref[...])
    beta = kk * a
    for p in range(n_pairs):
        sl = slice(p * LANES, (p + 1) * LANES)
        lw_p[p] = logw[:, sl]
        r_p[p] = r[:, sl]
        kk_p[p] = kk[:, sl]
        be_p[p] = beta[:, sl]
        k2_p[p] = k2[:, sl]
        v_p[p] = v[:, sl]

    rr, cc = _iota((tb, tb), 0), _iota((tb, tb), 1)
    same = (rr // cs) == (cc // cs)
    tri_incl = same & (cc <= rr)
    tri_strict = same & (cc < rr)
    l_incl = jnp.where(tri_incl, 1.0, 0.0)
    l_all = jnp.where(same, 1.0, 0.0)
    eye = jnp.where(rr == cc, 1.0, 0.0)
    lane = _iota((tb, LANES), 1)
    half = lane < RWKV_N
    same_half = (rr // RWKV_N) == (cc // RWKV_N)
    lane_c = _iota((cs, LANES), 1)
    lane_s = _iota((RWKV_N, LANES), 1)

    def pair(p, carry):
        lw = lw_p[p]
        cum = _sel_mm(l_incl, lw)
        gam = jnp.exp(cum)
        rt = r_p[p] * gam
        bt = kk_p[p] * jnp.exp(cum - lw)
        ig = jnp.exp(-cum)
        at = -(be_p[p] * ig)
        kt = k2_p[p] * ig
        vp = v_p[p]
        ge_scr[...] = jnp.exp(_sel_mm(l_all, lw))
        br = jnp.concatenate([bt, rt], axis=0)
        ak = jnp.concatenate([at, kt], axis=0)
        u0_h, mb_h, p_h, y0_h = [], [], [], []
        for hh in range(2):
            hm = half if hh == 0 else jnp.logical_not(half)
            hm2 = jnp.concatenate([hm, hm], axis=0)
            s4 = _mm3_nt(jnp.where(hm2, br, 0.0), ak)
            a_ab = jnp.where(tri_strict, s4[:tb, :tb], 0.0)
            a_ak = jnp.where(tri_strict, s4[:tb, tb:], 0.0)
            a_ra = jnp.where(tri_incl, s4[tb:, :tb], 0.0)
            a_rk = jnp.where(tri_incl, s4[tb:, tb:], 0.0)
            m_inv = eye
            for j in range(cs - 1):
                col_j = _mm_sel(jnp.where((cc % cs) == j, a_ab, 0.0), l_all)
                row_j = _sel_mm(jnp.where(same & ((cc % cs) == j), 1.0, 0.0), m_inv)
                m_inv = m_inv + col_j * row_j
            w1 = _mm3(a_ak, vp)
            mw = _mm3(m_inv, jnp.concatenate([w1, bt], axis=1))
            u0, mb = mw[:, :LANES], mw[:, LANES:]
            u0_h.append(u0)
            mb_h.append(mb)
            p_h.append(_mm3(a_ra, mb))
            y0_h.append(_mm3(a_ra, u0) + _mm3(a_rk, vp))
        u0 = jnp.where(half, u0_h[0], u0_h[1])
        mb = jnp.where(half, mb_h[0], mb_h[1])
        pp_scr[...] = rt + jnp.where(half, p_h[0], p_h[1])
        y0_scr[...] = jnp.where(half, y0_h[0], y0_h[1])
        u0t_scr[...] = u0.T
        vt_scr[...] = vp.T
        mbt_scr[...] = mb.T
        ak_scr[...] = ak

        def chunk(c, carry2):
            r0 = pl.multiple_of(c * cs, cs)
            tmask = (lane // cs) == c
            lhs_h = jnp.concatenate([jnp.where(tmask, u0t_scr[...], 0.0), jnp.where(tmask, vt_scr[...], 0.0)], axis=1)
            h_full = _mm3(lhs_h, ak_scr[...])
            h_c = jnp.where(lane_s < RWKV_N, h_full[:RWKV_N], h_full[RWKV_N:])
            g_full = _mm3(jnp.where(tmask, mbt_scr[...], 0.0), ak_scr[0:tb, :])
            g_bd = jnp.where(same_half, g_full, 0.0) + eye
            sp = s_ref[p]
            pc = pp_scr[pl.ds(r0, cs), :]
            lhs_y = jnp.concatenate([jnp.where(lane_c < RWKV_N, pc, 0.0), jnp.where(lane_c >= RWKV_N, pc, 0.0)], axis=0)
            yy = _mm3_nt(lhs_y, jnp.concatenate([sp, sp], axis=0))
            y_p[p, pl.ds(r0, cs), :] = jnp.where(lane_c < RWKV_N, yy[:cs], yy[cs:]) + y0_scr[pl.ds(r0, cs), :]
            s_ref[p] = (_mm3(sp, g_bd) + h_c) * ge_scr[pl.ds(r0, 1), :]
            return carry2

        lax.fori_loop(0, n_chunks, chunk, 0)
        return carry

    lax.fori_loop(0, n_pairs, pair, 0)

    y = jnp.concatenate([y_p[p] for p in range(n_pairs)], axis=1)
    mu = segsum(y) * (1.0 / RWKV_N)
    yc = y - mu
    var = segsum(yc * yc) * (1.0 / RWKV_N)
    yn = yc * lax.rsqrt(var + RWKV_GN_EPS) * gnw_ref[...] + gnb_ref[...]
    bonus = segsum(r * k2 * rk_ref[...]) * v
    o_ref[0] = ((yn + bonus) * g).astype(o_ref.dtype)


def _rwkv(p, mus, w0, w2p, a0, a2p, g2, k_k, k_a, r_k, gn_w, gn_b, *, tb=128):
    bsz, t, _ = p.shape
    w = RWKV_HEADS * RWKV_N
    n_pairs = RWKV_HEADS // 2
    assert tb == LANES
    row = lambda a: a.reshape(1, -1)
    full = lambda shape: pl.BlockSpec(shape, lambda b, i: (0,) * len(shape))
    tile = lambda: pltpu.VMEM((tb, LANES), F32)
    ptile = lambda: pltpu.VMEM((n_pairs, tb, LANES), F32)
    return pl.pallas_call(
        _rwkv_kernel,
        grid=(bsz, t // tb),
        in_specs=[
            pl.BlockSpec((1, tb, w), lambda b, i: (b, i, 3)),
            pl.BlockSpec((1, tb, w), lambda b, i: (b, i, 4)),
            pl.BlockSpec((1, tb, w), lambda b, i: (b, i, 5)),
            pl.BlockSpec((1, tb, LANES), lambda b, i: (b, i, 25)),
            pl.BlockSpec((1, tb, LANES), lambda b, i: (b, i, 26)),
            full((1, w)), full((1, w)), full((1, w)), full((1, LANES)), full((1, LANES)),
            full((1, w)), full((LANES, w)), full((1, w)), full((LANES, w)), full((LANES, w)),
            full((1, w)), full((1, w)), full((1, w)), full((1, w)), full((1, w)),
        ],
        out_specs=pl.BlockSpec((1, tb, w), lambda b, i: (b, i, 0)),
        out_shape=jax.ShapeDtypeStruct((bsz, t, w), BF16),
        scratch_shapes=[
            pltpu.VMEM((8, w), F32), pltpu.VMEM((8, w), F32), pltpu.VMEM((8, w), F32),
            pltpu.VMEM((8, LANES), F32), pltpu.VMEM((8, LANES), F32),
            pltpu.VMEM((n_pairs, RWKV_N, LANES), F32),
            ptile(), ptile(), ptile(), ptile(), ptile(), ptile(), ptile(),
            tile(), tile(), tile(), tile(), tile(), tile(),
            pltpu.VMEM((2 * tb, LANES), F32),
        ],
        compiler_params=_cparams(("parallel", "arbitrary")),
        name="rwkv7",
    )(p, p, p, p, p, *[row(m) for m in mus], row(w0), w2p, row(a0), a2p, g2,
      row(k_k), row(k_a), row(r_k), row(gn_w), row(gn_b))


def _cmp_kernel(sk_ref, sv_ref, wak_ref, wbk_ref, wav_ref, wbv_ref, pek_ref, pev_ref, w1k_ref, w1v_ref,
                w2k_ref, w2v_ref, ok_ref, ov_ref):
    def one(seg_ref, wa_ref, wb_ref, pe_ref, w1_ref, w2_ref, o_ref):
        seg = seg_ref[0].astype(BF16)
        first = _dot(seg, wa_ref[...])
        second = _dot(seg, wb_ref[...])
        n = first.shape[0]
        pe_term = _mm3(pe_ref[...], w1_ref[...])[0:1]
        hidden = _gelu_tanh(first + pltpu.roll(second, n - 1, 0) + pe_term)
        o_ref[0] = _mm(hidden, w2_ref[...])

    one(sk_ref, wak_ref, wbk_ref, pek_ref, w1k_ref, w2k_ref, ok_ref)
    one(sv_ref, wav_ref, wbv_ref, pev_ref, w1v_ref, w2v_ref, ov_ref)


def _nsa_compress(kc_tok, vc_tok, pe_k, w1_k, w2_k, pe_v, w1_v, w2_v):
    bsz, t, gw = kc_tok.shape
    st, dh, hid, g = NSA_CMP_STRIDE, NSA_DH, NSA_CMP_HIDDEN, NSA_GROUPS
    nseg = t // st
    eye = jnp.eye(g, dtype=F32)

    def expand_w1(w1):
        w = w1.reshape(NSA_CMP_LEN, dh, hid)
        big = jnp.einsum('ldc,gh->lgdhc', w, eye).reshape(NSA_CMP_LEN * g * dh, g * hid)
        half = st * g * dh
        return big[:half].astype(BF16), big[half:].astype(BF16)

    def expand_w2(w2):
        return jnp.einsum('cd,gh->gchd', w2, eye).reshape(g * hid, g * dh).astype(BF16)

    def pe_rows(pe):
        return jnp.zeros((8, NSA_CMP_LEN * dh), F32).at[0].set(pe.reshape(-1))

    wak, wbk = expand_w1(w1_k)
    wav, wbv = expand_w1(w1_v)
    full = lambda shape: pl.BlockSpec(shape, lambda b: (0,) * len(shape))
    seg_spec = pl.BlockSpec((1, nseg, st * gw), lambda b: (b, 0, 0))
    out_spec = pl.BlockSpec((1, nseg, gw), lambda b: (b, 0, 0))
    return pl.pallas_call(
        _cmp_kernel,
        grid=(bsz,),
        in_specs=[seg_spec, seg_spec,
                  full(wak.shape), full(wbk.shape), full(wav.shape), full(wbv.shape),
                  full((8, NSA_CMP_LEN * dh)), full((8, NSA_CMP_LEN * dh)),
                  full((NSA_CMP_LEN * dh, g * hid)), full((NSA_CMP_LEN * dh, g * hid)),
                  full((g * hid, gw)), full((g * hid, gw))],
        out_specs=[out_spec, out_spec],
        out_shape=[jax.ShapeDtypeStruct((bsz, nseg, gw), F32)] * 2,
        compiler_params=_cparams(("parallel",)),
        name="nsa_compress",
    )(kc_tok.reshape(bsz, nseg, st * gw), vc_tok.reshape(bsz, nseg, st * gw),
      wak, wbk, wav, wbv, pe_rows(pe_k), pe_rows(pe_v),
      jnp.tile(w1_k, (1, g)), jnp.tile(w1_v, (1, g)), expand_w2(w2_k), expand_w2(w2_v))


def _nsa_attn_kernel(q_ref, kc_ref, vc_ref, ks_ref, vs_ref, kw_ref, vw_ref, gt_ref, sl_ref, o_ref):
    g = pl.program_id(1)
    qi = pl.program_id(2)
    qb, dh, rep = NSA_Q_BLOCK, NSA_DH, NSA_REP
    n_cmp_pad = kc_ref.shape[2]
    kt = LANES

    q = (q_ref[0].reshape(rep * qb, dh).astype(F32) * (dh ** -0.5)).astype(BF16)
    slope_col = jnp.concatenate(
        [jnp.broadcast_to(sl_ref[0, r:r + 1, 0:1], (qb, 1)) for r in range(rep)], axis=0)
    t0 = qi * qb

    n_idx = _iota((qb, n_cmp_pad), 1)
    dist_c = t0 + _iota((qb, n_cmp_pad), 0) - (n_idx * NSA_CMP_STRIDE + NSA_CMP_LEN - 1)
    valid_c = (dist_c >= 0) & (n_idx < n_cmp_pad - 1)
    dist_cf = dist_c.astype(F32)
    s_all = _dot_nt(q, kc_ref[0, 0])
    vcmp = vc_ref[0, 0]
    n_sel_blocks = ks_ref.shape[2] // NSA_SEL_LEN
    on, oj = _iota((n_cmp_pad, LANES), 0), _iota((n_cmp_pad, LANES), 1)
    overlap = jnp.where((on * NSA_CMP_STRIDE <= oj * NSA_SEL_LEN + NSA_SEL_LEN - 1)
                        & (on * NSA_CMP_STRIDE + NSA_CMP_LEN - 1 >= oj * NSA_SEL_LEN)
                        & (oj < n_sel_blocks) & (on < n_cmp_pad - 1), 1.0, 0.0)
    imp = jnp.zeros((qb, LANES), F32)
    o_cmp = []
    for r in range(rep):
        s = s_all[r * qb:(r + 1) * qb] - sl_ref[0, r:r + 1, 0:1] * dist_cf
        s = jnp.where(valid_c, s, NEG_BIG)
        m = jnp.max(s, axis=-1, keepdims=True)
        e = jnp.where(valid_c, jnp.exp(s - m), 0.0)
        den = jnp.sum(e, axis=-1, keepdims=True)
        pr = e / jnp.where(den > 0.0, den, 1.0)
        o_cmp.append(_mm(pr, vcmp))
        imp = imp + _mm_sel(pr, overlap)

    imp_t = imp.T[:n_sel_blocks]
    jj = _iota((n_sel_blocks, qb), 0)
    jf = jj.astype(F32)
    ahead = (t0 + _iota((n_sel_blocks, qb), 1)) // NSA_SEL_LEN - jj
    valid_b = ahead >= 0
    forced = (jj == 0) | (valid_b & (ahead < NSA_N_LOCAL))
    score = jnp.where(valid_b, imp_t + jnp.where(forced, NSA_FORCE, 0.0), -NSA_FORCE)
    sel = jnp.zeros((n_sel_blocks, qb), F32)
    for _ in range(NSA_N_SEL):
        best = jnp.max(score, axis=0, keepdims=True)
        first = jnp.min(jnp.where(score == best, jf, float(n_sel_blocks)), axis=0, keepdims=True)
        pick = jf == first
        sel = jnp.where(pick, 1.0, sel)
        score = jnp.where(pick, NEG_BIG, score)
    sel_q = jnp.concatenate([sel, jnp.zeros((LANES - n_sel_blocks, qb), F32)], axis=0).T.astype(BF16)

    row_t = t0 + _iota((qb, kt), 0)
    lane_k = _iota((qb, kt), 1)

    def tile_update(carry, k_tile, v_tile, ok, dist_f):
        m, l, acc = carry
        okf = jnp.where(ok, 1.0, 0.0)
        ok4 = jnp.concatenate([okf] * rep, axis=0) > 0.5
        s = _dot_nt(q, k_tile) - slope_col * jnp.concatenate([dist_f] * rep, axis=0)
        s = jnp.where(ok4, s, NEG_BIG)
        m_new = jnp.maximum(m, jnp.max(s, axis=-1, keepdims=True))
        e = jnp.where(ok4, jnp.exp(s - m_new), 0.0)
        alpha = jnp.exp(m - m_new)
        return (m_new, alpha * l + jnp.sum(e, axis=-1, keepdims=True), alpha * acc + _mm(e, v_tile))

    def init():
        return (jnp.full((rep * qb, 1), NEG_BIG, F32), jnp.zeros((rep * qb, 1), F32), jnp.zeros((rep * qb, dh), F32))

    def finish(carry):
        _, l, acc = carry
        return acc / jnp.where(l > 0.0, l, 1.0)

    def sel_body(kb, carry):
        k0 = pl.multiple_of(kb * kt, kt)
        dist = row_t - (k0 + lane_k)
        expand = jnp.where(_iota((LANES, kt), 0) == 2 * kb + _iota((LANES, kt), 1) // NSA_SEL_LEN, 1.0, 0.0)
        chosen = _dot(sel_q, expand.astype(BF16))
        ok = (chosen > 0.5) & (dist >= 0)
        return tile_update(carry, ks_ref[0, 0, pl.ds(k0, kt), :], vs_ref[0, 0, pl.ds(k0, kt), :], ok, dist.astype(F32))

    o_sel = finish(lax.fori_loop(0, qi + 1, sel_body, init()))

    carry = init()
    for w in range(NSA_WINDOW // kt + 1):
        kb = qi - NSA_WINDOW // kt + w
        k0 = pl.multiple_of(jnp.maximum(kb, 0) * kt, kt)
        dist = row_t - (k0 + lane_k)
        ok = (dist >= jnp.where(kb >= 0, 0, NSA_WINDOW)) & (dist < NSA_WINDOW)
        carry = tile_update(carry, kw_ref[0, 0, pl.ds(k0, kt), :], vw_ref[0, 0, pl.ds(k0, kt), :], ok, dist.astype(F32))
    o_win = finish(carry)

    gsel = jnp.where(_iota((LANES, LANES), 0) == g * (rep * 3) + _iota((LANES, LANES), 1), 1.0, 0.0)
    gates = _sigmoid(_mm_sel(gt_ref[0], gsel))
    for r in range(rep):
        rows = slice(r * qb, (r + 1) * qb)
        o_ref[0, r] = (gates[:, 3 * r:3 * r + 1] * o_cmp[r] + gates[:, 3 * r + 1:3 * r + 2] * o_sel[rows]
                       + gates[:, 3 * r + 2:3 * r + 3] * o_win[rows])


def _nsa_attention(q_h, kcmp, vcmp, ks, vs, kw, vw, p, slopes):
    bsz, heads, t, dh = q_h.shape
    g, rep, qb = NSA_GROUPS, NSA_REP, NSA_Q_BLOCK
    n_pad = kcmp.shape[2]
    kv_spec = pl.BlockSpec((1, 1, t, dh), lambda b, gi, i: (b, gi, 0, 0))
    cmp_spec = pl.BlockSpec((1, 1, n_pad, dh), lambda b, gi, i: (b, gi, 0, 0))
    gate_col = p.shape[-1] // LANES - 1
    return pl.pallas_call(
        _nsa_attn_kernel,
        grid=(bsz, g, t // qb),
        in_specs=[
            pl.BlockSpec((1, rep, qb, dh), lambda b, gi, i: (b, gi, i, 0)),
            cmp_spec, cmp_spec, kv_spec, kv_spec, kv_spec, kv_spec,
            pl.BlockSpec((1, qb, LANES), lambda b, gi, i: (b, i, gate_col)),
            pl.BlockSpec((1, 8, LANES), lambda b, gi, i: (gi, 0, 0)),
        ],
        out_specs=pl.BlockSpec((1, rep, qb, dh), lambda b, gi, i: (b, gi, i, 0)),
        out_shape=jax.ShapeDtypeStruct((bsz, heads, t, dh), F32),
        compiler_params=_cparams(("parallel", "parallel", "arbitrary")),
        name="nsa_attention",
    )(q_h, kcmp, vcmp, ks, vs, kw, vw, p, slopes)


def _even_mixer(x, norm_g, sc, sh, w_in, shift_mu, a_up, a_b, gla_g, w0, w2, a0, a2, g2, k_k, k_a, r_k, gn_w, gn_b):
    d = x.shape[-1]
    qk, vw, w = GLA_HEADS * GLA_DK, GLA_HEADS * GLA_DV, RWKV_HEADS * RWKV_N
    gla_cols = 2 * qk + 2 * vw + GLA_LOWRANK
    wg, wr = w_in[:, :gla_cols], w_in[:, gla_cols:]
    o_r, o_wl, o_k, o_v, o_al, o_gl = np.cumsum([0, w, RWKV_W_LORA, w, w, RWKV_A_LORA]).tolist()
    pad = lambda a, n: jnp.pad(a, ((0, 0), (0, n - a.shape[1])))
    w_perm = jnp.concatenate([
        wg[:, :2 * qk + 2 * vw],
        wr[:, o_r:o_r + w], wr[:, o_k:o_k + w], wr[:, o_v:o_v + w],
        pad(wg[:, 2 * qk + 2 * vw:], LANES),
        wr[:, o_wl:o_wl + RWKV_W_LORA], wr[:, o_al:o_al + RWKV_A_LORA],
        wr[:, o_gl:o_gl + RWKV_G_LORA]], axis=1).astype(BF16)
    p = _norm_proj(x, norm_g, sc, sh, w_perm)
    a_up_pad = jnp.zeros((LANES, qk), F32).at[:GLA_LOWRANK].set(a_up)
    o_gla = _gla(p, a_up_pad, a_b, gla_g)
    mu = shift_mu
    mus = [mu[o_r:o_r + w], mu[o_k:o_k + w], mu[o_v:o_v + w],
           jnp.concatenate([mu[o_wl:o_wl + RWKV_W_LORA], mu[o_al:o_al + RWKV_A_LORA]]), mu[o_gl:o_gl + RWKV_G_LORA]]
    w2p = jnp.zeros((LANES, w), F32).at[:RWKV_W_LORA].set(w2)
    a2p = jnp.zeros((LANES, w), F32).at[RWKV_W_LORA:RWKV_W_LORA + RWKV_A_LORA].set(a2)
    o_rw = _rwkv(p, mus, w0, w2p, a0, a2p, g2, k_k, k_a, r_k.reshape(-1), gn_w, gn_b)
    return jnp.concatenate([o_gla, o_rw], axis=-1)


def _nsa_mixer(x, norm_g, sc, sh, w_in, pe_k, w1_k, w2_k, pe_v, w1_v, w2_v):
    bsz, t, d = x.shape
    g, dh, heads = NSA_GROUPS, NSA_DH, NSA_HEADS
    n_cols = w_in.shape[1]
    n_pad = -(-n_cols // (3 * LANES)) * (3 * LANES)
    w_pad = jnp.pad(w_in, ((0, 0), (0, n_pad - n_cols))).astype(BF16)
    p = _norm_proj(x, norm_g, sc, sh, w_pad)
    kv = g * dh
    off = heads * dh
    seg = lambda i: p[..., off + i * kv: off + (i + 1) * kv]
    kcmp, vcmp = _nsa_compress(seg(0), seg(1), pe_k, w1_k, w2_k, pe_v, w1_v, w2_v)
    to_heads = lambda a, n: a.reshape(bsz, a.shape[1], n, dh).transpose(0, 2, 1, 3).astype(BF16)
    slopes = 2.0 ** (-8.0 * jnp.arange(1, heads + 1, dtype=F32) / heads)
    slopes = jnp.broadcast_to(jnp.pad(slopes.reshape(g, NSA_REP), ((0, 0), (0, 8 - NSA_REP)))[:, :, None], (g, 8, LANES))
    o = _nsa_attention(to_heads(p[..., :off], heads), to_heads(kcmp, g), to_heads(vcmp, g),
                       to_heads(seg(2), g), to_heads(seg(3), g), to_heads(seg(4), g), to_heads(seg(5), g), p, slopes)
    return o.transpose(0, 2, 1, 3).reshape(bsz, t, heads * dh)


def kernel(x, c, ada_w, ada_b, norm1_g, norm2_g, ffn_w_up, ffn_conv_w, ffn_conv_b, ffn_w_down, ev_w_in, ev_shift_mu, gla_a_up, gla_a_b, gla_norm_g, rw_w0, rw_w2, rw_a0, rw_a2, rw_g2, rw_k_k, rw_k_a, rw_r_k, rw_gn_w, rw_gn_b, ev_w_out, od_w_in, cmp_pe_k, cmp_w1_k, cmp_w2_k, cmp_pe_v, cmp_w1_v, cmp_w2_v, od_w_out, final_norm_g):
    bsz, t, d = x.shape
    depth = ada_w.shape[0]
    mod = _ada_mod(c, ada_w, ada_b)
    for layer in range(depth):
        sh1, sc1, g1, sh2, sc2, g2 = (mod[layer, :, i * d:(i + 1) * d].reshape(bsz, 1, d) for i in range(6))
        i = layer // 2
        if layer % 2 == 0:
            mix = _even_mixer(x, norm1_g[layer], sc1, sh1, ev_w_in[i], ev_shift_mu[i], gla_a_up[i], gla_a_b[i],
                              gla_norm_g[i], rw_w0[i], rw_w2[i], rw_a0[i], rw_a2[i], rw_g2[i], rw_k_k[i], rw_k_a[i],
                              rw_r_k[i], rw_gn_w[i], rw_gn_b[i])
            w_out = ev_w_out[i]
        else:
            mix = _nsa_mixer(x, norm1_g[layer], sc1, sh1, od_w_in[i], cmp_pe_k[i], cmp_w1_k[i], cmp_w2_k[i],
                             cmp_pe_v[i], cmp_w1_v[i], cmp_w2_v[i])
            w_out = od_w_out[i]
        x = _out_proj(mix, w_out.astype(BF16), x, g1)
        x = _conv_ffn(x, norm2_g[layer], sc2, sh2, g2, ffn_w_up[layer].astype(BF16), ffn_conv_w[layer],
                      ffn_conv_b[layer], ffn_w_down[layer].astype(BF16), final_norm_g,
                      final_norm=(layer == depth - 1))
    return x
```

```python
import functools

import numpy as np
import jax
import jax.numpy as jnp
from jax import lax
from jax.experimental import pallas as pl
from jax.experimental.pallas import tpu as pltpu

F32 = jnp.float32
BF16 = jnp.bfloat16

D_MODEL = 1024
NORM_EPS = 1e-6
GLA_HEADS, GLA_DK, GLA_DV, GLA_LOWRANK, GLA_GATE_NORM, GLA_CHUNK = 4, 64, 128, 16, 16.0, 16
RWKV_HEADS, RWKV_N, RWKV_GN_EPS = 8, 64, 64e-5
RWKV_W_LORA, RWKV_A_LORA, RWKV_G_LORA = 64, 64, 128
RWKV_CHUNK = 16
NSA_HEADS, NSA_GROUPS, NSA_DH = 16, 4, 64
NSA_REP = NSA_HEADS // NSA_GROUPS
NSA_CMP_LEN, NSA_CMP_STRIDE, NSA_CMP_HIDDEN = 32, 16, 64
NSA_SEL_LEN, NSA_N_SEL, NSA_N_LOCAL, NSA_WINDOW, NSA_Q_BLOCK, NSA_FORCE = 64, 8, 2, 512, 128, 100.0
FFN_HIDDEN = 2816

LANES = 128
VMEM_LIMIT = 56 * 1024 * 1024
NEG_BIG = -1e30


def _cparams(sem):
    return pltpu.CompilerParams(dimension_semantics=sem, vmem_limit_bytes=VMEM_LIMIT)


def _dot(a, b):
    return jnp.dot(a, b, preferred_element_type=F32)


def _dot_nt(a, b):
    return lax.dot_general(a, b, (((1,), (1,)), ((), ())), preferred_element_type=F32)


def _mm(a, b):
    return _dot(a.astype(BF16), b.astype(BF16))


def _mm_nt(a, b):
    return _dot_nt(a.astype(BF16), b.astype(BF16))


def _split3(a):
    a1 = a.astype(BF16)
    r1 = a - a1.astype(F32)
    a2 = r1.astype(BF16)
    a3 = (r1 - a2.astype(F32)).astype(BF16)
    return a1, a2, a3


def _mm_sel(a, b01):
    a1, a2, a3 = _split3(a)
    b = b01.astype(BF16)
    return _dot(a1, b) + _dot(a2, b) + _dot(a3, b)


def _sel_mm(a01, b):
    b1, b2, b3 = _split3(b)
    a = a01.astype(BF16)
    return _dot(a, b1) + _dot(a, b2) + _dot(a, b3)


def _mm3(a, b):
    a1 = a.astype(BF16)
    a2 = (a - a1.astype(F32)).astype(BF16)
    b1 = b.astype(BF16)
    b2 = (b - b1.astype(F32)).astype(BF16)
    return _dot(a1, b1) + _dot(a1, b2) + _dot(a2, b1)


def _mm3_nt(a, b):
    a1 = a.astype(BF16)
    a2 = (a - a1.astype(F32)).astype(BF16)
    b1 = b.astype(BF16)
    b2 = (b - b1.astype(F32)).astype(BF16)
    return _dot_nt(a1, b1) + _dot_nt(a1, b2) + _dot_nt(a2, b1)


def _iota(shape, dim):
    return lax.broadcasted_iota(jnp.int32, shape, dim)


def _sigmoid(x):
    return 1.0 / (1.0 + jnp.exp(-x))


def _softplus(x):
    return jnp.maximum(x, 0.0) + jnp.log(1.0 + jnp.exp(-jnp.abs(x)))


def _gelu_tanh(x):
    return x * (0.5 * (1.0 + jnp.tanh(0.7978845608028654 * (x + 0.044715 * (x * x * x)))))


def _rms(x, eps):
    return x * lax.rsqrt(jnp.mean(x * x, axis=-1, keepdims=True) + eps)


def _mod_kernel(c_ref, w_ref, b_ref, o_ref):
    c = c_ref[...]
    cond = c * _sigmoid(c)
    o_ref[0] = _mm3(cond, w_ref[0]) + b_ref[0]


def _ada_mod(c, ada_w, ada_b):
    depth, d, n = ada_w.shape
    bsz = c.shape[0]
    rows = 8
    c8 = jnp.zeros((rows, d), F32).at[:bsz].set(c)
    tn = 1536
    out = pl.pallas_call(
        _mod_kernel,
        grid=(depth, n // tn),
        in_specs=[
            pl.BlockSpec((rows, d), lambda l, j: (0, 0)),
            pl.BlockSpec((1, d, tn), lambda l, j: (l, 0, j)),
            pl.BlockSpec((1, 1, tn), lambda l, j: (l, 0, j)),
        ],
        out_specs=pl.BlockSpec((1, rows, tn), lambda l, j: (l, 0, j)),
        out_shape=jax.ShapeDtypeStruct((depth, rows, n), F32),
        compiler_params=_cparams(("parallel", "parallel")),
        name="ada_mod",
    )(c8, ada_w, ada_b.reshape(depth, 1, n))
    return out[:, :bsz]


def _norm_proj_kernel(x_ref, g_ref, sc_ref, sh_ref, w_ref, o_ref, hn_ref):
    @pl.when(pl.program_id(2) == 0)
    def _():
        hn = _rms(x_ref[0], NORM_EPS) * g_ref[...]
        hn_ref[...] = (hn * (1.0 + sc_ref[0]) + sh_ref[0]).astype(BF16)

    o_ref[0] = _dot(hn_ref[...], w_ref[...])


def _norm_proj(x, g, sc, sh, w, *, tm=512, n_split=3):
    bsz, t, d = x.shape
    n = w.shape[1]
    tn = n // n_split
    return pl.pallas_call(
        _norm_proj_kernel,
        grid=(bsz, t // tm, n_split),
        in_specs=[
            pl.BlockSpec((1, tm, d), lambda b, i, j: (b, i, 0)),
            pl.BlockSpec((1, d), lambda b, i, j: (0, 0)),
            pl.BlockSpec((1, 1, d), lambda b, i, j: (b, 0, 0)),
            pl.BlockSpec((1, 1, d), lambda b, i, j: (b, 0, 0)),
            pl.BlockSpec((d, tn), lambda b, i, j: (0, j)),
        ],
        out_specs=pl.BlockSpec((1, tm, tn), lambda b, i, j: (b, i, j)),
        out_shape=jax.ShapeDtypeStruct((bsz, t, n), F32),
        scratch_shapes=[pltpu.VMEM((tm, d), BF16)],
        compiler_params=_cparams(("parallel", "parallel", "arbitrary")),
        name="norm_proj",
    )(x, g.reshape(1, d), sc, sh, w)


def _out_proj_kernel(m_ref, w_ref, x_ref, gate_ref, o_ref):
    o_ref[0] = x_ref[0] + gate_ref[0] * _mm(m_ref[0], w_ref[...])


def _out_proj(mix, w, x, gate, *, tm=512):
    bsz, t, d = x.shape
    kdim = mix.shape[-1]
    return pl.pallas_call(
        _out_proj_kernel,
        grid=(bsz, t // tm),
        in_specs=[
            pl.BlockSpec((1, tm, kdim), lambda b, i: (b, i, 0)),
            pl.BlockSpec((kdim, d), lambda b, i: (0, 0)),
            pl.BlockSpec((1, tm, d), lambda b, i: (b, i, 0)),
            pl.BlockSpec((1, 1, d), lambda b, i: (b, 0, 0)),
        ],
        out_specs=pl.BlockSpec((1, tm, d), lambda b, i: (b, i, 0)),
        out_shape=jax.ShapeDtypeStruct((bsz, t, d), F32),
        compiler_params=_cparams(("parallel", "parallel")),
        name="out_proj",
    )(mix, w, x, gate)


def _ffn_kernel(x_ref, g_ref, sc_ref, sh_ref, gate_ref, wu_ref, wv_ref, cw_ref, cb_ref, wd_ref, fg_ref,
                o_ref, hn_ref, acc_ref, halo_ref, *, final_norm):
    ti = pl.program_id(1)
    fj = pl.program_id(2)
    tm, fk = acc_ref.shape[0], wu_ref.shape[1]

    @pl.when(fj == 0)
    def _():
        hn = _rms(x_ref[0], NORM_EPS) * g_ref[...]
        hn_ref[...] = (hn * (1.0 + sc_ref[0]) + sh_ref[0]).astype(BF16)
        acc_ref[...] = jnp.zeros_like(acc_ref)

    @pl.when(ti == 0)
    def _():
        halo_ref[fj] = jnp.zeros((8, fk), F32)

    hn = hn_ref[...]
    u = _dot(hn, wu_ref[...])
    v = _dot(hn, wv_ref[...])
    prev = halo_ref[fj]
    row = _iota((tm, fk), 0)
    u1 = jnp.where(row == 0, prev[7:8], pltpu.roll(u, 1, 0))
    u2 = jnp.where(row == 0, prev[6:7], jnp.where(row == 1, prev[7:8], pltpu.roll(u, 2, 0)))
    halo_ref[fj] = u[tm - 8:tm]
    cw = cw_ref[...]
    uc = cw[0:1] * u2 + cw[1:2] * u1 + cw[2:3] * u + cb_ref[...]
    h = _gelu_tanh(uc) * v
    acc_ref[...] += _dot(h.astype(BF16), wd_ref[...])

    @pl.when(fj == pl.num_programs(2) - 1)
    def _():
        y = x_ref[0] + gate_ref[0] * acc_ref[...]
        if final_norm:
            y = _rms(y, NORM_EPS) * fg_ref[...]
        o_ref[0] = y


def _conv_ffn(x, g, sc, sh, gate, w_up, conv_w, conv_b, w_down, final_g, *, final_norm, tm=512, fk=256):
    bsz, t, d = x.shape
    f = w_down.shape[0]
    nf = f // fk
    kern = functools.partial(_ffn_kernel, final_norm=final_norm)
    return pl.pallas_call(
        kern,
        grid=(bsz, t // tm, nf),
        in_specs=[
            pl.BlockSpec((1, tm, d), lambda b, i, j: (b, i, 0)),
            pl.BlockSpec((1, d), lambda b, i, j: (0, 0)),
            pl.BlockSpec((1, 1, d), lambda b, i, j: (b, 0, 0)),
            pl.BlockSpec((1, 1, d), lambda b, i, j: (b, 0, 0)),
            pl.BlockSpec((1, 1, d), lambda b, i, j: (b, 0, 0)),
            pl.BlockSpec((d, fk), lambda b, i, j: (0, j)),
            pl.BlockSpec((d, fk), lambda b, i, j: (0, j + nf)),
            pl.BlockSpec((3, fk), lambda b, i, j: (0, j)),
            pl.BlockSpec((1, fk), lambda b, i, j: (0, j)),
            pl.BlockSpec((fk, d), lambda b, i, j: (j, 0)),
            pl.BlockSpec((1, d), lambda b, i, j: (0, 0)),
        ],
        out_specs=pl.BlockSpec((1, tm, d), lambda b, i, j: (b, i, 0)),
        out_shape=jax.ShapeDtypeStruct((bsz, t, d), F32),
        scratch_shapes=[pltpu.VMEM((tm, d), BF16), pltpu.VMEM((tm, d), F32), pltpu.VMEM((nf, 8, fk), F32)],
        compiler_params=_cparams(("parallel", "arbitrary", "arbitrary")),
        name="conv_ffn",
    )(x, g.reshape(1, d), sc, sh, gate, w_up, w_up, conv_w, conv_b.reshape(1, f), w_down, final_g.reshape(1, d))


def _gla_kernel(q_ref, k_ref, v_ref, og_ref, lr_ref, aup_ref, ab_ref, gg_ref, o_ref,
                st_ref, b_scr, bend_scr, q_scr, k_scr, kd_scr, qg_scr, vt_scr, o_scr):
    tb = q_ref.shape[1]
    cs = GLA_CHUNK
    n_chunks = tb // cs

    @pl.when(pl.program_id(1) == 0)
    def _():
        st_ref[...] = jnp.zeros_like(st_ref)

    z = _mm3(lr_ref[0], aup_ref[...]) + ab_ref[...]
    la = -_softplus(-z) * (1.0 / GLA_GATE_NORM)
    rr, cc = _iota((tb, tb), 0), _iota((tb, tb), 1)
    same = (rr // cs) == (cc // cs)
    b = _sel_mm(jnp.where(same & (cc <= rr), 1.0, 0.0), la)
    bend = _sel_mm(jnp.where(same, 1.0, 0.0), la)
    q = q_ref[0] * (GLA_DK ** -0.5)
    k = k_ref[0]
    b_scr[...] = b
    bend_scr[...] = bend
    q_scr[...] = q
    k_scr[...] = k
    kd_scr[...] = (k * jnp.exp(bend - b)).astype(BF16)
    qg_scr[...] = q * jnp.exp(b)
    v_all = v_ref[0]
    for h in range(GLA_HEADS):
        vt_scr[h] = v_all[:, h * GLA_DV:(h + 1) * GLA_DV].T.astype(BF16)

    lane_c = _iota((cs, LANES), 1)
    row_c = _iota((cs, LANES), 0)
    lane_t = _iota((tb, LANES), 1)
    head_rows = _iota((LANES, LANES), 0) // GLA_DK

    def chunk(c, carry):
        r0 = pl.multiple_of(c * cs, cs)
        tmask = (lane_t // cs) == c
        for p in range(GLA_HEADS // 2):
            lanes = slice(p * LANES, (p + 1) * LANES)
            bc = b_scr[pl.ds(r0, cs), lanes]
            qc = q_scr[pl.ds(r0, cs), lanes]
            kc = k_scr[pl.ds(r0, cs), lanes]
            blocks = []
            for s in range(cs):
                m = row_c >= s
                rel = jnp.where(m, bc - bc[s:s + 1], 0.0)
                blocks.append(jnp.where(m, qc * kc[s:s + 1] * jnp.exp(rel), 0.0))
            a_st = jnp.concatenate(blocks, axis=0)
            qgc = qg_scr[pl.ds(r0, cs), lanes]
            lhs = jnp.concatenate([jnp.where(lane_c < GLA_DK, qgc, 0.0),
                                   jnp.where(lane_c >= GLA_DK, qgc, 0.0)], axis=0)
            st = st_ref[p]
            o_inter = _mm_nt(lhs, st)
            for hh in range(2):
                h = 2 * p + hh
                zsum = _mm_sel(a_st, jnp.where(head_rows == hh, 1.0, 0.0))
                vc = v_ref[0, pl.ds(r0, cs), h * GLA_DV:(h + 1) * GLA_DV]
                o_h = o_inter[hh * cs:(hh + 1) * cs]
                for s in range(cs):
                    o_h = o_h + zsum[s * cs:(s + 1) * cs] * vc[s:s + 1]
                o_scr[pl.ds(r0, cs), h * GLA_DV:(h + 1) * GLA_DV] = o_h
            lhs_u = jnp.concatenate([jnp.where(tmask, vt_scr[2 * p], 0.0).astype(BF16),
                                     jnp.where(tmask, vt_scr[2 * p + 1], 0.0).astype(BF16)], axis=0)
            upd = _dot(lhs_u, kd_scr[:, lanes])
            upd = jnp.where(lane_t < GLA_DK, upd[:GLA_DV], upd[GLA_DV:])
            decay = jnp.exp(bend_scr[pl.ds(r0, 1), lanes])
            st_ref[p] = st * decay + upd
        return carry

    lax.fori_loop(0, n_chunks, chunk, 0)

    og = og_ref[0]
    for h in range(GLA_HEADS):
        sl = slice(h * GLA_DV, (h + 1) * GLA_DV)
        gate = og[:, sl]
        o_ref[0, :, sl] = (_rms(o_scr[:, sl], NORM_EPS) * gg_ref[...] * (gate * _sigmoid(gate))).astype(o_ref.dtype)


def _gla(p, a_up_pad, a_b, gla_g, *, tb=128):
    bsz, t, _ = p.shape
    qk = GLA_HEADS * GLA_DK
    vw = GLA_HEADS * GLA_DV
    assert tb == LANES and GLA_DV == LANES
    return pl.pallas_call(
        _gla_kernel,
        grid=(bsz, t // tb),
        in_specs=[
            pl.BlockSpec((1, tb, qk), lambda b, i: (b, i, 0)),
            pl.BlockSpec((1, tb, qk), lambda b, i: (b, i, 1)),
            pl.BlockSpec((1, tb, vw), lambda b, i: (b, i, 1)),
            pl.BlockSpec((1, tb, vw), lambda b, i: (b, i, 2)),
            pl.BlockSpec((1, tb, LANES), lambda b, i: (b, i, 24)),
            pl.BlockSpec((LANES, qk), lambda b, i: (0, 0)),
            pl.BlockSpec((1, qk), lambda b, i: (0, 0)),
            pl.BlockSpec((1, GLA_DV), lambda b, i: (0, 0)),
        ],
        out_specs=pl.BlockSpec((1, tb, vw), lambda b, i: (b, i, 0)),
        out_shape=jax.ShapeDtypeStruct((bsz, t, vw), BF16),
        scratch_shapes=[
            pltpu.VMEM((GLA_HEADS // 2, GLA_DV, LANES), F32),
            pltpu.VMEM((tb, qk), F32), pltpu.VMEM((tb, qk), F32),
            pltpu.VMEM((tb, qk), F32), pltpu.VMEM((tb, qk), F32),
            pltpu.VMEM((tb, qk), BF16), pltpu.VMEM((tb, qk), F32),
            pltpu.VMEM((GLA_HEADS, GLA_DV, tb), BF16),
            pltpu.VMEM((tb, vw), F32),
        ],
        compiler_params=_cparams(("parallel", "arbitrary")),
        name="gla",
    )(p, p, p, p, p, a_up_pad, a_b.reshape(1, qk), gla_g.reshape(1, GLA_DV))


def _rwkv_kernel(r_ref, k_ref, v_ref, wa_ref, gl_ref, mur_ref, muk_ref, muv_ref, muwa_ref, mugl_ref,
                 w0_ref, w2_ref, a0_ref, a2_ref, g2_ref, kkw_ref, ka_ref, rk_ref, gnw_ref, gnb_ref,
                 o_ref,
                 lr_scr, lk_scr, lv_scr, lwa_scr, lgl_scr, s_ref,
                 lw_p, r_p, kk_p, be_p, k2_p, v_p, y_p, pp_all, y0_all, ge_all, g_all, h_all):
    tb = r_ref.shape[1]
    cs = RWKV_CHUNK
    n_chunks = tb // cs
    n_pairs = RWKV_HEADS // 2
    ti = pl.program_id(1)
    lasts = (lr_scr, lk_scr, lv_scr, lwa_scr, lgl_scr)

    @pl.when(ti == 0)
    def _():
        s_ref[...] = jnp.zeros_like(s_ref)
        for ref in lasts:
            ref[...] = jnp.zeros_like(ref)

    def shifted(x_ref, last_ref, mu_ref):
        x = x_ref[0]
        row = _iota(x.shape, 0)
        prev = jnp.where(row == 0, last_ref[7:8], pltpu.roll(x, 1, 0))
        last_ref[...] = x[tb - 8:tb]
        return x + (prev - x) * mu_ref[...]

    r = shifted(r_ref, lr_scr, mur_ref)
    k = shifted(k_ref, lk_scr, muk_ref)
    v = shifted(v_ref, lv_scr, muv_ref)
    wa = shifted(wa_ref, lwa_scr, muwa_ref)
    gl = shifted(gl_ref, lgl_scr, mugl_ref)

    logw = -jnp.exp(-_softplus(-(w0_ref[...] + _mm3(jnp.tanh(wa), w2_ref[...]))) - 0.5)
    a = _sigmoid(a0_ref[...] + _mm3(wa, a2_ref[...]))
    g = _mm3(_sigmoid(gl), g2_ref[...])
    seg = jnp.where((_iota((LANES, LANES), 0) // RWKV_N) == (_iota((LANES, LANES), 1) // RWKV_N), 1.0, 0.0)

    def segsum(x):
        return jnp.concatenate([_mm_sel(x[:, i * LANES:(i + 1) * LANES], seg) for i in range(n_pairs)], axis=1)

    kk = k * kkw_ref[...]
    kk = kk * lax.rsqrt(jnp.maximum(segsum(kk * kk), 1e-24))
    k2 = k * (1.0 + (a - 1.0) * ka_ref[...])
    beta = kk * a
    for p in range(n_pairs):
        sl = slice(p * LANES, (p + 1) * LANES)
        lw_p[p] = logw[:, sl]
        r_p[p] = r[:, sl]
        kk_p[p] = kk[:, sl]
        be_p[p] = beta[:, sl]
        k2_p[p] = k2[:, sl]
        v_p[p] = v[:, sl]

    rr, cc = _iota((tb, tb), 0), _iota((tb, tb), 1)
    same = (rr // cs) == (cc // cs)
    tri_incl = same & (cc <= rr)
    tri_strict = same & (cc < rr)
    l_incl = jnp.where(tri_incl, 1.0, 0.0)
    l_all = jnp.where(same, 1.0, 0.0)
    eye = jnp.where(rr == cc, 1.0, 0.0)
    lane = _iota((tb, LANES), 1)
    half = lane < RWKV_N
    same_half = (rr // RWKV_N) == (cc // RWKV_N)
    lane_c = _iota((cs, LANES), 1)
    lane_s = _iota((RWKV_N, LANES), 1)

    def below_left(s):
        return ((rr // (2 * s)) == (cc // (2 * s))) & ((rr % (2 * s)) >= s) & ((cc % (2 * s)) < s)

    def pair(p, carry):
        lw = lw_p[p]
        cum = _sel_mm(l_incl, lw)
        gam = jnp.exp(cum)
        rt = r_p[p] * gam
        bt = kk_p[p] * jnp.exp(cum - lw)
        ig = jnp.exp(-cum)
        at = -(be_p[p] * ig)
        kt = k2_p[p] * ig
        vp = v_p[p]
        ge_all[p] = jnp.exp(_sel_mm(l_all, lw))
        br = jnp.concatenate([bt, rt], axis=0)
        ak = jnp.concatenate([at, kt], axis=0)
        u0_h, mb_h, p_h, y0_h = [], [], [], []
        for hh in range(2):
            hm = half if hh == 0 else jnp.logical_not(half)
            hm2 = jnp.concatenate([hm, hm], axis=0)
            s4 = _mm3_nt(jnp.where(hm2, br, 0.0), ak)
            a_ab = jnp.where(tri_strict, s4[:tb, :tb], 0.0)
            a_ak = jnp.where(tri_strict, s4[:tb, tb:], 0.0)
            a_ra = jnp.where(tri_incl, s4[tb:, :tb], 0.0)
            a_rk = jnp.where(tri_incl, s4[tb:, tb:], 0.0)
            m_inv = eye + jnp.where(below_left(1), a_ab, 0.0)
            s = 2
            while s < cs:
                m_inv = m_inv + _mm3(_mm3(m_inv, jnp.where(below_left(s), a_ab, 0.0)), m_inv)
                s *= 2
            w1 = _mm3(a_ak, vp)
            mw = _mm3(m_inv, jnp.concatenate([w1, bt], axis=1))
            u0, mb = mw[:, :LANES], mw[:, LANES:]
            u0_h.append(u0)
            mb_h.append(mb)
            p_h.append(_mm3(a_ra, mb))
            y0_h.append(_mm3(a_ra, u0) + _mm3(a_rk, vp))
        u0 = jnp.where(half, u0_h[0], u0_h[1])
        mb = jnp.where(half, mb_h[0], mb_h[1])
        pp_all[p] = rt + jnp.where(half, p_h[0], p_h[1])
        y0_all[p] = jnp.where(half, y0_h[0], y0_h[1])
        u0t, vt, mbt = u0.T, vp.T, mb.T
        zeros = jnp.zeros((tb, LANES), F32)
        for c in range(n_chunks):
            tmask = (lane // cs) == c
            lhs = jnp.concatenate([
                jnp.concatenate([jnp.where(tmask, u0t, 0.0), jnp.where(tmask, vt, 0.0)], axis=1),
                jnp.concatenate([jnp.where(tmask, mbt, 0.0), zeros], axis=1)], axis=0)
            hg = _mm3(lhs, ak)
            h_all[p * n_chunks + c] = jnp.where(lane_s < RWKV_N, hg[:RWKV_N], hg[RWKV_N:tb])
            g_all[p * n_chunks + c] = jnp.where(same_half, hg[tb:], 0.0) + eye
        return carry

    for p in range(n_pairs):
        pair(p, 0)

    state = [s_ref[p] for p in range(n_pairs)]
    for c in range(n_chunks):
        rows = slice(c * cs, (c + 1) * cs)
        for p in range(n_pairs):
            sp = state[p]
            pc = pp_all[p, rows, :]
            lhs_y = jnp.concatenate([jnp.where(lane_c < RWKV_N, pc, 0.0), jnp.where(lane_c >= RWKV_N, pc, 0.0)], axis=0)
            yy = _mm3_nt(lhs_y, jnp.concatenate([sp, sp], axis=0))
            y_p[p, rows, :] = jnp.where(lane_c < RWKV_N, yy[:cs], yy[cs:]) + y0_all[p, rows, :]
            state[p] = (_mm3(sp, g_all[p * n_chunks + c]) + h_all[p * n_chunks + c]) * ge_all[p, c * cs:c * cs + 1, :]
    for p in range(n_pairs):
        s_ref[p] = state[p]

    y = jnp.concatenate([y_p[p] for p in range(n_pairs)], axis=1)
    mu = segsum(y) * (1.0 / RWKV_N)
    yc = y - mu
    var = segsum(yc * yc) * (1.0 / RWKV_N)
    yn = yc * lax.rsqrt(var + RWKV_GN_EPS) * gnw_ref[...] + gnb_ref[...]
    bonus = segsum(r * k2 * rk_ref[...]) * v
    o_ref[0] = ((yn + bonus) * g).astype(o_ref.dtype)


def _rwkv(p, mus, w0, w2p, a0, a2p, g2, k_k, k_a, r_k, gn_w, gn_b, *, tb=128):
    bsz, t, _ = p.shape
    w = RWKV_HEADS * RWKV_N
    n_pairs = RWKV_HEADS // 2
    assert tb == LANES
    row = lambda a: a.reshape(1, -1)
    full = lambda shape: pl.BlockSpec(shape, lambda b, i: (0,) * len(shape))
    ptile = lambda: pltpu.VMEM((n_pairs, tb, LANES), F32)
    return pl.pallas_call(
        _rwkv_kernel,
        grid=(bsz, t // tb),
        in_specs=[
            pl.BlockSpec((1, tb, w), lambda b, i: (b, i, 3)),
            pl.BlockSpec((1, tb, w), lambda b, i: (b, i, 4)),
            pl.BlockSpec((1, tb, w), lambda b, i: (b, i, 5)),
            pl.BlockSpec((1, tb, LANES), lambda b, i: (b, i, 25)),
            pl.BlockSpec((1, tb, LANES), lambda b, i: (b, i, 26)),
            full((1, w)), full((1, w)), full((1, w)), full((1, LANES)), full((1, LANES)),
            full((1, w)), full((LANES, w)), full((1, w)), full((LANES, w)), full((LANES, w)),
            full((1, w)), full((1, w)), full((1, w)), full((1, w)), full((1, w)),
        ],
        out_specs=pl.BlockSpec((1, tb, w), lambda b, i: (b, i, 0)),
        out_shape=jax.ShapeDtypeStruct((bsz, t, w), BF16),
        scratch_shapes=[
            pltpu.VMEM((8, w), F32), pltpu.VMEM((8, w), F32), pltpu.VMEM((8, w), F32),
            pltpu.VMEM((8, LANES), F32), pltpu.VMEM((8, LANES), F32),
            pltpu.VMEM((n_pairs, RWKV_N, LANES), F32),
            ptile(), ptile(), ptile(), ptile(), ptile(), ptile(), ptile(), ptile(), ptile(), ptile(),
            pltpu.VMEM((n_pairs * (tb // RWKV_CHUNK), LANES, LANES), F32),
            pltpu.VMEM((n_pairs * (tb // RWKV_CHUNK), RWKV_N, LANES), F32),
        ],
        compiler_params=_cparams(("parallel", "arbitrary")),
        name="rwkv7",
    )(p, p, p, p, p, *[row(m) for m in mus], row(w0), w2p, row(a0), a2p, g2,
      row(k_k), row(k_a), row(r_k), row(gn_w), row(gn_b))


def _cmp_kernel(sk_ref, sv_ref, wak_ref, wbk_ref, wav_ref, wbv_ref, pek_ref, pev_ref, w1k_ref, w1v_ref,
                w2k_ref, w2v_ref, ok_ref, ov_ref):
    def one(seg_ref, wa_ref, wb_ref, pe_ref, w1_ref, w2_ref, o_ref):
        seg = seg_ref[0].astype(BF16)
        first = _dot(seg, wa_ref[...])
        second = _dot(seg, wb_ref[...])
        n = first.shape[0]
        pe_term = _mm3(pe_ref[...], w1_ref[...])[0:1]
        hidden = _gelu_tanh(first + pltpu.roll(second, n - 1, 0) + pe_term)
        o_ref[0] = _mm(hidden, w2_ref[...])

    one(sk_ref, wak_ref, wbk_ref, pek_ref, w1k_ref, w2k_ref, ok_ref)
    one(sv_ref, wav_ref, wbv_ref, pev_ref, w1v_ref, w2v_ref, ov_ref)


def _nsa_compress(kc_tok, vc_tok, pe_k, w1_k, w2_k, pe_v, w1_v, w2_v):
    bsz, t, gw = kc_tok.shape
    st, dh, hid, g = NSA_CMP_STRIDE, NSA_DH, NSA_CMP_HIDDEN, NSA_GROUPS
    nseg = t // st
    eye = jnp.eye(g, dtype=F32)

    def expand_w1(w1):
        w = w1.reshape(NSA_CMP_LEN, dh, hid)
        big = jnp.einsum('ldc,gh->lgdhc', w, eye).reshape(NSA_CMP_LEN * g * dh, g * hid)
        half = st * g * dh
        return big[:half].astype(BF16), big[half:].astype(BF16)

    def expand_w2(w2):
        return jnp.einsum('cd,gh->gchd', w2, eye).reshape(g * hid, g * dh).astype(BF16)

    def pe_rows(pe):
        return jnp.zeros((8, NSA_CMP_LEN * dh), F32).at[0].set(pe.reshape(-1))

    wak, wbk = expand_w1(w1_k)
    wav, wbv = expand_w1(w1_v)
    full = lambda shape: pl.BlockSpec(shape, lambda b: (0,) * len(shape))
    seg_spec = pl.BlockSpec((1, nseg, st * gw), lambda b: (b, 0, 0))
    out_spec = pl.BlockSpec((1, nseg, gw), lambda b: (b, 0, 0))
    return pl.pallas_call(
        _cmp_kernel,
        grid=(bsz,),
        in_specs=[seg_spec, seg_spec,
                  full(wak.shape), full(wbk.shape), full(wav.shape), full(wbv.shape),
                  full((8, NSA_CMP_LEN * dh)), full((8, NSA_CMP_LEN * dh)),
                  full((NSA_CMP_LEN * dh, g * hid)), full((NSA_CMP_LEN * dh, g * hid)),
                  full((g * hid, gw)), full((g * hid, gw))],
        out_specs=[out_spec, out_spec],
        out_shape=[jax.ShapeDtypeStruct((bsz, nseg, gw), F32)] * 2,
        compiler_params=_cparams(("parallel",)),
        name="nsa_compress",
    )(kc_tok.reshape(bsz, nseg, st * gw), vc_tok.reshape(bsz, nseg, st * gw),
      wak, wbk, wav, wbv, pe_rows(pe_k), pe_rows(pe_v),
      jnp.tile(w1_k, (1, g)), jnp.tile(w1_v, (1, g)), expand_w2(w2_k), expand_w2(w2_v))


def _nsa_attn_kernel(q_ref, kc_ref, vc_ref, ks_ref, vs_ref, kw_ref, vw_ref, gt_ref, sl_ref, o_ref,
                     s_scr, mx_scr, acc_scr):
    g = pl.program_id(1)
    qi = pl.program_id(2)
    qb, dh, rep = NSA_Q_BLOCK, NSA_DH, NSA_REP
    n_cmp_pad = kc_ref.shape[2]
    kt = LANES
    rows_all = rep * qb

    lane_q = _iota((1, LANES), 1)
    q_rows = []
    for r in range(rep):
        s1, s2, s3 = (t.astype(F32) for t in _split3(sl_ref[0, r:r + 1, :]))
        slope_cols = jnp.where((lane_q >= dh) & (lane_q < dh + 2), s1,
                               jnp.where((lane_q >= dh + 2) & (lane_q < dh + 4), s2,
                                         jnp.where((lane_q >= dh + 4) & (lane_q < dh + 6), s3, 0.0)))
        q_rows.append(q_ref[0, r].astype(F32) * (dh ** -0.5) + slope_cols)
    q = jnp.concatenate(q_rows, axis=0).astype(BF16)
    t0 = qi * qb

    row_l = _iota((rows_all, kt), 0) % qb
    lane_k = _iota((rows_all, kt), 1)
    causal = row_l >= lane_k
    win_lo = row_l < lane_k

    def lane_tiles(s):
        return [s[:, j * kt:(j + 1) * kt] for j in range(s.shape[1] // kt)]

    def tile_max(tiles, start):
        m = start
        for tile in tiles:
            m = jnp.maximum(m, tile)
        return m

    def normalised(acc):
        return acc[:, :dh] / acc[:, dh:dh + 1]

    n_win = NSA_WINDOW // kt + 1
    win_s, win_v = [], []
    for w in range(n_win):
        kb = qi - (n_win - 1) + w
        k0 = pl.multiple_of(jnp.maximum(kb, 0) * kt, kt)
        s = _dot_nt(q, kw_ref[0, 0, pl.ds(k0, kt), :])
        if w == 0:
            s = jnp.where(win_lo, s, NEG_BIG)
        if w == n_win - 1:
            s = jnp.where(causal, s, NEG_BIG)
        else:
            s = s + jnp.where(kb >= 0, 0.0, NEG_BIG)
        win_s.append(s)
        win_v.append(vw_ref[0, 0, pl.ds(k0, kt), :])
    m_win = jnp.broadcast_to(jnp.max(tile_max(win_s[1:], win_s[0]), axis=-1, keepdims=True), (rows_all, kt))
    e_win = jnp.concatenate([jnp.exp(s - m_win).astype(BF16) for s in win_s], axis=1)
    o_win = normalised(_dot(e_win, jnp.concatenate(win_v, axis=0)))

    gsel = jnp.where(_iota((LANES, LANES), 0) == g * (rep * 3) + _iota((LANES, LANES), 1), 1.0, 0.0)
    gates = _sigmoid(_mm_sel(gt_ref[0], gsel))

    n_idx = _iota((qb, n_cmp_pad), 1)
    dist_c = t0 + _iota((qb, n_cmp_pad), 0) - (n_idx * NSA_CMP_STRIDE + NSA_CMP_LEN - 1)
    valid_c = (dist_c >= 0) & (n_idx < n_cmp_pad - 1)
    dist_cf = dist_c.astype(F32)
    s_all = _dot_nt(q, kc_ref[0, 0])
    vcmp = vc_ref[0, 0]
    n_sel_blocks = ks_ref.shape[2] // NSA_SEL_LEN
    on, oj = _iota((n_cmp_pad, LANES), 0), _iota((n_cmp_pad, LANES), 1)
    overlap = jnp.where((on * NSA_CMP_STRIDE <= oj * NSA_SEL_LEN + NSA_SEL_LEN - 1)
                        & (on * NSA_CMP_STRIDE + NSA_CMP_LEN - 1 >= oj * NSA_SEL_LEN)
                        & (oj < n_sel_blocks) & (on < n_cmp_pad - 1), 1.0, 0.0)
    imp = jnp.zeros((qb, LANES), F32)
    o_cmp = []
    for r in range(rep):
        s = s_all[r * qb:(r + 1) * qb] - sl_ref[0, r:r + 1, 0:1] * dist_cf
        s = jnp.where(valid_c, s, NEG_BIG)
        m = jnp.max(s, axis=-1, keepdims=True)
        e = jnp.where(valid_c, jnp.exp(s - m), 0.0)
        den = jnp.sum(e, axis=-1, keepdims=True)
        pr = e / jnp.where(den > 0.0, den, 1.0)
        o_cmp.append(_mm(pr, vcmp))
        imp = imp + _mm_sel(pr, overlap)

    imp_t = imp.T[:n_sel_blocks]
    jj = _iota((n_sel_blocks, qb), 0)
    jf = jj.astype(F32)
    ahead = (t0 + _iota((n_sel_blocks, qb), 1)) // NSA_SEL_LEN - jj
    valid_b = ahead >= 0
    forced = (jj == 0) | (valid_b & (ahead < NSA_N_LOCAL))
    score = jnp.where(valid_b, imp_t + jnp.where(forced, NSA_FORCE, 0.0), -NSA_FORCE)
    sel = jnp.zeros((n_sel_blocks, qb), F32)
    for _ in range(NSA_N_SEL):
        best = jnp.max(score, axis=0, keepdims=True)
        first = jnp.min(jnp.where(score == best, jf, float(n_sel_blocks)), axis=0, keepdims=True)
        pick = jf == first
        sel = jnp.where(pick, 1.0, sel)
        score = jnp.where(pick, NEG_BIG, score)
    sel_q = jnp.concatenate([sel, jnp.zeros((LANES - n_sel_blocks, qb), F32)], axis=0).T.astype(BF16)

    tpg = s_scr.shape[2] // kt
    kg = tpg * kt

    sel_bias = ((sel_q.astype(F32) - 1.0) * (2.0 ** 100)).astype(BF16)
    q_sel = jnp.concatenate([q, jnp.concatenate([sel_bias] * rep, axis=0)], axis=1)
    n_past = qi // tpg
    mx_scr[...] = jnp.full((rows_all, kt), NEG_BIG, F32)

    def sel_scores(gi, carry):
        k0 = pl.multiple_of(gi * kg, kg)
        s = _dot_nt(q_sel, ks_ref[0, 0, pl.ds(k0, kg), :])
        s_scr[gi] = s
        mx_scr[...] = tile_max(lane_tiles(s), mx_scr[...])
        return carry

    lax.fori_loop(0, n_past, sel_scores, 0)
    s_last = _dot_nt(q_sel, ks_ref[0, 0, pl.ds(pl.multiple_of(n_past * kg, kg), kg), :])
    diag = qi - n_past * tpg
    masked = []
    for j, tile in enumerate(lane_tiles(s_last)):
        shift = jnp.where(j < diag, -kt, jnp.where(j == diag, 0, kt))
        masked.append(jnp.where(row_l >= lane_k + shift, tile, NEG_BIG))
    s_scr[n_past] = jnp.concatenate(masked, axis=1)
    m_sel = jnp.max(tile_max(masked, mx_scr[...]), axis=-1, keepdims=True)
    mx_scr[...] = jnp.broadcast_to(m_sel, (rows_all, kt))
    acc_scr[...] = jnp.zeros_like(acc_scr)

    def sel_pv(gi, carry):
        k0 = pl.multiple_of(gi * kg, kg)
        m_full = mx_scr[...]
        e = jnp.concatenate([jnp.exp(tile - m_full).astype(BF16) for tile in lane_tiles(s_scr[gi])], axis=1)
        acc_scr[...] += _dot(e, vs_ref[0, 0, pl.ds(k0, kg), :])
        return carry

    lax.fori_loop(0, n_past + 1, sel_pv, 0)
    o_sel = normalised(acc_scr[...])

    for r in range(rep):
        rows = slice(r * qb, (r + 1) * qb)
        o_ref[0, r] = (gates[:, 3 * r:3 * r + 1] * o_cmp[r] + gates[:, 3 * r + 1:3 * r + 2] * o_sel[rows]
                       + gates[:, 3 * r + 2:3 * r + 3] * o_win[rows])


def _nsa_attention(q_h, kcmp, vcmp, ks, vs, kw, vw, p, slopes):
    bsz, heads, t, _ = q_h.shape
    g, rep, qb, dh = NSA_GROUPS, NSA_REP, NSA_Q_BLOCK, NSA_DH
    n_pad = kcmp.shape[2]
    whole = lambda a: pl.BlockSpec((1, 1) + a.shape[2:], lambda b, gi, i: (b, gi, 0, 0))
    gate_col = p.shape[-1] // LANES - 1
    return pl.pallas_call(
        _nsa_attn_kernel,
        grid=(bsz, g, t // qb),
        in_specs=[
            pl.BlockSpec((1, rep, qb, LANES), lambda b, gi, i: (b, gi, i, 0)),
            whole(kcmp), whole(vcmp), whole(ks), whole(vs), whole(kw), whole(vw),
            pl.BlockSpec((1, qb, LANES), lambda b, gi, i: (b, i, gate_col)),
            pl.BlockSpec((1, 8, LANES), lambda b, gi, i: (gi, 0, 0)),
        ],
        out_specs=pl.BlockSpec((1, rep, qb, dh), lambda b, gi, i: (b, gi, i, 0)),
        out_shape=jax.ShapeDtypeStruct((bsz, heads, t, dh), F32),
        scratch_shapes=[pltpu.VMEM((t // (4 * LANES), rep * qb, 4 * LANES), F32),
                        pltpu.VMEM((rep * qb, LANES), F32), pltpu.VMEM((rep * qb, LANES), F32)],
        compiler_params=_cparams(("parallel", "parallel", "arbitrary")),
        name="nsa_attention",
    )(q_h, kcmp, vcmp, ks, vs, kw, vw, p, slopes)


def _even_mixer(x, norm_g, sc, sh, w_in, shift_mu, a_up, a_b, gla_g, w0, w2, a0, a2, g2, k_k, k_a, r_k, gn_w, gn_b):
    d = x.shape[-1]
    qk, vw, w = GLA_HEADS * GLA_DK, GLA_HEADS * GLA_DV, RWKV_HEADS * RWKV_N
    gla_cols = 2 * qk + 2 * vw + GLA_LOWRANK
    wg, wr = w_in[:, :gla_cols], w_in[:, gla_cols:]
    o_r, o_wl, o_k, o_v, o_al, o_gl = np.cumsum([0, w, RWKV_W_LORA, w, w, RWKV_A_LORA]).tolist()
    pad = lambda a, n: jnp.pad(a, ((0, 0), (0, n - a.shape[1])))
    w_perm = jnp.concatenate([
        wg[:, :2 * qk + 2 * vw],
        wr[:, o_r:o_r + w], wr[:, o_k:o_k + w], wr[:, o_v:o_v + w],
        pad(wg[:, 2 * qk + 2 * vw:], LANES),
        wr[:, o_wl:o_wl + RWKV_W_LORA], wr[:, o_al:o_al + RWKV_A_LORA],
        wr[:, o_gl:o_gl + RWKV_G_LORA]], axis=1).astype(BF16)
    p = _norm_proj(x, norm_g, sc, sh, w_perm)
    a_up_pad = jnp.zeros((LANES, qk), F32).at[:GLA_LOWRANK].set(a_up)
    o_gla = _gla(p, a_up_pad, a_b, gla_g)
    mu = shift_mu
    mus = [mu[o_r:o_r + w], mu[o_k:o_k + w], mu[o_v:o_v + w],
           jnp.concatenate([mu[o_wl:o_wl + RWKV_W_LORA], mu[o_al:o_al + RWKV_A_LORA]]), mu[o_gl:o_gl + RWKV_G_LORA]]
    w2p = jnp.zeros((LANES, w), F32).at[:RWKV_W_LORA].set(w2)
    a2p = jnp.zeros((LANES, w), F32).at[RWKV_W_LORA:RWKV_W_LORA + RWKV_A_LORA].set(a2)
    o_rw = _rwkv(p, mus, w0, w2p, a0, a2p, g2, k_k, k_a, r_k.reshape(-1), gn_w, gn_b)
    return jnp.concatenate([o_gla, o_rw], axis=-1)


def _nsa_mixer(x, norm_g, sc, sh, w_in, pe_k, w1_k, w2_k, pe_v, w1_v, w2_v):
    bsz, t, d = x.shape
    g, dh, heads = NSA_GROUPS, NSA_DH, NSA_HEADS
    n_cols = w_in.shape[1]
    n_pad = -(-n_cols // (3 * LANES)) * (3 * LANES)
    w_pad = jnp.pad(w_in, ((0, 0), (0, n_pad - n_cols))).astype(BF16)
    p = _norm_proj(x, norm_g, sc, sh, w_pad)
    kv = g * dh
    off = heads * dh
    seg = lambda i: p[..., off + i * kv: off + (i + 1) * kv]
    kcmp, vcmp = _nsa_compress(seg(0), seg(1), pe_k, w1_k, w2_k, pe_v, w1_v, w2_v)
    to_heads = lambda a, n: a.reshape(bsz, a.shape[1], n, dh).transpose(0, 2, 1, 3).astype(BF16)
    slopes = 2.0 ** (-8.0 * jnp.arange(1, heads + 1, dtype=F32) / heads)
    slopes = jnp.broadcast_to(jnp.pad(slopes.reshape(g, NSA_REP), ((0, 0), (0, 8 - NSA_REP)))[:, :, None], (g, 8, LANES))
    pos = jnp.arange(t)
    pos_cols = jnp.tile(jnp.stack([pos // LANES * LANES, pos % LANES], axis=-1), (1, 3)).astype(BF16)
    blk_cols = (pos[:, None] // NSA_SEL_LEN == jnp.arange(LANES)[None, :]).astype(BF16)
    ones_col = jnp.ones((t, 1), BF16)
    bcast = lambda cols, like: jnp.broadcast_to(cols, like.shape[:2] + cols.shape)
    pad_to = lambda a, n: jnp.pad(a, ((0, 0),) * (a.ndim - 1) + ((0, n - a.shape[-1]),))
    with_pos = lambda k: pad_to(jnp.concatenate([k, bcast(pos_cols, k)], axis=-1), LANES)
    with_ones = lambda v: pad_to(jnp.concatenate([v, bcast(ones_col, v)], axis=-1), LANES)
    ks, vs, kw, vw = (to_heads(seg(i), g) for i in (2, 3, 4, 5))
    ks_aug = jnp.concatenate([with_pos(ks), bcast(blk_cols, ks)], axis=-1)
    o = _nsa_attention(pad_to(to_heads(p[..., :off], heads), LANES), pad_to(to_heads(kcmp, g), LANES), to_heads(vcmp, g),
                       ks_aug, with_ones(vs), with_pos(kw), with_ones(vw), p, slopes)
    return o.transpose(0, 2, 1, 3).reshape(bsz, t, heads * dh)


def kernel(x, c, ada_w, ada_b, norm1_g, norm2_g, ffn_w_up, ffn_conv_w, ffn_conv_b, ffn_w_down, ev_w_in, ev_shift_mu, gla_a_up, gla_a_b, gla_norm_g, rw_w0, rw_w2, rw_a0, rw_a2, rw_g2, rw_k_k, rw_k_a, rw_r_k, rw_gn_w, rw_gn_b, ev_w_out, od_w_in, cmp_pe_k, cmp_w1_k, cmp_w2_k, cmp_pe_v, cmp_w1_v, cmp_w2_v, od_w_out, final_norm_g):
    bsz, t, d = x.shape
    depth = ada_w.shape[0]
    mod = _ada_mod(c, ada_w, ada_b)
    for layer in range(depth):
        sh1, sc1, g1, sh2, sc2, g2 = (mod[layer, :, i * d:(i + 1) * d].reshape(bsz, 1, d) for i in range(6))
        i = layer // 2
        if layer % 2 == 0:
            mix = _even_mixer(x, norm1_g[layer], sc1, sh1, ev_w_in[i], ev_shift_mu[i], gla_a_up[i], gla_a_b[i],
                              gla_norm_g[i], rw_w0[i], rw_w2[i], rw_a0[i], rw_a2[i], rw_g2[i], rw_k_k[i], rw_k_a[i],
                              rw_r_k[i], rw_gn_w[i], rw_gn_b[i])
            w_out = ev_w_out[i]
        else:
            mix = _nsa_mixer(x, norm1_g[layer], sc1, sh1, od_w_in[i], cmp_pe_k[i], cmp_w1_k[i], cmp_w2_k[i],
                             cmp_pe_v[i], cmp_w1_v[i], cmp_w2_v[i])
            w_out = od_w_out[i]
        x = _out_proj(mix, w_out.astype(BF16), x, g1)
        x = _conv_ffn(x, norm2_g[layer], sc2, sh2, g2, ffn_w_up[layer].astype(BF16), ffn_conv_w[layer],
                      ffn_conv_b[layer], ffn_w_down[layer].astype(BF16), final_norm_g,
                      final_norm=(layer == depth - 1))
    return x
```

```python
import functools

import numpy as np
import jax
import jax.numpy as jnp
from jax import lax
from jax.experimental import pallas as pl
from jax.experimental.pallas import tpu as pltpu

F32 = jnp.float32
BF16 = jnp.bfloat16

D_MODEL = 1024
NORM_EPS = 1e-6
GLA_HEADS, GLA_DK, GLA_DV, GLA_LOWRANK, GLA_GATE_NORM, GLA_CHUNK = 4, 64, 128, 16, 16.0, 16
RWKV_HEADS, RWKV_N, RWKV_GN_EPS = 8, 64, 64e-5
RWKV_W_LORA, RWKV_A_LORA, RWKV_G_LORA = 64, 64, 128
RWKV_CHUNK = 16
NSA_HEADS, NSA_GROUPS, NSA_DH = 16, 4, 64
NSA_REP = NSA_HEADS // NSA_GROUPS
NSA_CMP_LEN, NSA_CMP_STRIDE, NSA_CMP_HIDDEN = 32, 16, 64
NSA_SEL_LEN, NSA_N_SEL, NSA_N_LOCAL, NSA_WINDOW, NSA_Q_BLOCK, NSA_FORCE = 64, 8, 2, 512, 128, 100.0
NSA_SWEEP_TILES = 4
FFN_HIDDEN = 2816

LANES = 128
VMEM_LIMIT = 56 * 1024 * 1024
NEG_BIG = -1e30


def _cparams(sem):
    return pltpu.CompilerParams(dimension_semantics=sem, vmem_limit_bytes=VMEM_LIMIT)


def _dot(a, b):
    return jnp.dot(a, b, preferred_element_type=F32)


def _dot_nt(a, b):
    return lax.dot_general(a, b, (((1,), (1,)), ((), ())), preferred_element_type=F32)


def _mm(a, b):
    return _dot(a.astype(BF16), b.astype(BF16))


def _mm_nt(a, b):
    return _dot_nt(a.astype(BF16), b.astype(BF16))


def _split3(a):
    a1 = a.astype(BF16)
    r1 = a - a1.astype(F32)
    a2 = r1.astype(BF16)
    a3 = (r1 - a2.astype(F32)).astype(BF16)
    return a1, a2, a3


def _mm_sel(a, b01, terms=3):
    b = b01.astype(BF16)
    out = None
    for part in _split3(a)[:terms]:
        out = _dot(part, b) if out is None else out + _dot(part, b)
    return out


def _sel_mm(a01, b):
    b1, b2, b3 = _split3(b)
    a = a01.astype(BF16)
    return _dot(a, b1) + _dot(a, b2) + _dot(a, b3)


def _mm3(a, b):
    a1 = a.astype(BF16)
    a2 = (a - a1.astype(F32)).astype(BF16)
    b1 = b.astype(BF16)
    b2 = (b - b1.astype(F32)).astype(BF16)
    return _dot(a1, b1) + _dot(a1, b2) + _dot(a2, b1)


def _mm3_nt(a, b):
    a1 = a.astype(BF16)
    a2 = (a - a1.astype(F32)).astype(BF16)
    b1 = b.astype(BF16)
    b2 = (b - b1.astype(F32)).astype(BF16)
    return _dot_nt(a1, b1) + _dot_nt(a1, b2) + _dot_nt(a2, b1)


def _iota(shape, dim):
    return lax.broadcasted_iota(jnp.int32, shape, dim)


def _sigmoid(x):
    return 1.0 / (1.0 + jnp.exp(-x))


def _softplus(x):
    return jnp.maximum(x, 0.0) + jnp.log(1.0 + jnp.exp(-jnp.abs(x)))


def _gelu_tanh(x):
    return x * (0.5 * (1.0 + jnp.tanh(0.7978845608028654 * (x + 0.044715 * (x * x * x)))))


def _rms(x, eps):
    return x * lax.rsqrt(jnp.mean(x * x, axis=-1, keepdims=True) + eps)


def _mod_kernel(c_ref, w_ref, b_ref, o_ref):
    c = c_ref[...]
    cond = c * _sigmoid(c)
    o_ref[0] = _mm3(cond, w_ref[0]) + b_ref[0]


def _ada_mod(c, ada_w, ada_b):
    depth, d, n = ada_w.shape
    bsz = c.shape[0]
    rows = 8
    c8 = jnp.zeros((rows, d), F32).at[:bsz].set(c)
    tn = 1536
    out = pl.pallas_call(
        _mod_kernel,
        grid=(depth, n // tn),
        in_specs=[
            pl.BlockSpec((rows, d), lambda l, j: (0, 0)),
            pl.BlockSpec((1, d, tn), lambda l, j: (l, 0, j)),
            pl.BlockSpec((1, 1, tn), lambda l, j: (l, 0, j)),
        ],
        out_specs=pl.BlockSpec((1, rows, tn), lambda l, j: (l, 0, j)),
        out_shape=jax.ShapeDtypeStruct((depth, rows, n), F32),
        compiler_params=_cparams(("parallel", "parallel")),
        name="ada_mod",
    )(c8, ada_w, ada_b.reshape(depth, 1, n))
    return out[:, :bsz]


def _norm_proj_kernel(x_ref, g_ref, sc_ref, sh_ref, w_ref, o_ref, hn_ref):
    @pl.when(pl.program_id(2) == 0)
    def _():
        hn = _rms(x_ref[0], NORM_EPS) * g_ref[...]
        hn_ref[...] = (hn * (1.0 + sc_ref[0]) + sh_ref[0]).astype(BF16)

    o_ref[0] = _dot(hn_ref[...], w_ref[...])


def _norm_proj(x, g, sc, sh, w, *, tm=512, n_split=3):
    bsz, t, d = x.shape
    n = w.shape[1]
    tn = n // n_split
    return pl.pallas_call(
        _norm_proj_kernel,
        grid=(bsz, t // tm, n_split),
        in_specs=[
            pl.BlockSpec((1, tm, d), lambda b, i, j: (b, i, 0)),
            pl.BlockSpec((1, d), lambda b, i, j: (0, 0)),
            pl.BlockSpec((1, 1, d), lambda b, i, j: (b, 0, 0)),
            pl.BlockSpec((1, 1, d), lambda b, i, j: (b, 0, 0)),
            pl.BlockSpec((d, tn), lambda b, i, j: (0, j)),
        ],
        out_specs=pl.BlockSpec((1, tm, tn), lambda b, i, j: (b, i, j)),
        out_shape=jax.ShapeDtypeStruct((bsz, t, n), F32),
        scratch_shapes=[pltpu.VMEM((tm, d), BF16)],
        compiler_params=_cparams(("parallel", "parallel", "arbitrary")),
        name="norm_proj",
    )(x, g.reshape(1, d), sc, sh, w)


def _out_proj_kernel(m_ref, w_ref, x_ref, gate_ref, o_ref):
    o_ref[0] = x_ref[0] + gate_ref[0] * _mm(m_ref[0], w_ref[...])


def _out_proj(mix, w, x, gate, *, tm=512):
    bsz, t, d = x.shape
    kdim = mix.shape[-1]
    return pl.pallas_call(
        _out_proj_kernel,
        grid=(bsz, t // tm),
        in_specs=[
            pl.BlockSpec((1, tm, kdim), lambda b, i: (b, i, 0)),
            pl.BlockSpec((kdim, d), lambda b, i: (0, 0)),
            pl.BlockSpec((1, tm, d), lambda b, i: (b, i, 0)),
            pl.BlockSpec((1, 1, d), lambda b, i: (b, 0, 0)),
        ],
        out_specs=pl.BlockSpec((1, tm, d), lambda b, i: (b, i, 0)),
        out_shape=jax.ShapeDtypeStruct((bsz, t, d), F32),
        compiler_params=_cparams(("parallel", "parallel")),
        name="out_proj",
    )(mix, w, x, gate)


def _ffn_kernel(x_ref, g_ref, sc_ref, sh_ref, gate_ref, wu_ref, wv_ref, cw_ref, cb_ref, wd_ref, fg_ref,
                o_ref, hn_ref, acc_ref, halo_ref, *, final_norm):
    ti = pl.program_id(1)
    fj = pl.program_id(2)
    tm, fk = acc_ref.shape[0], wu_ref.shape[1]

    @pl.when(fj == 0)
    def _():
        hn = _rms(x_ref[0], NORM_EPS) * g_ref[...]
        hn_ref[...] = (hn * (1.0 + sc_ref[0]) + sh_ref[0]).astype(BF16)
        acc_ref[...] = jnp.zeros_like(acc_ref)

    @pl.when(ti == 0)
    def _():
        halo_ref[fj] = jnp.zeros((8, fk), F32)

    hn = hn_ref[...]
    u = _dot(hn, wu_ref[...])
    v = _dot(hn, wv_ref[...])
    prev = halo_ref[fj]
    row = _iota((tm, fk), 0)
    u1 = jnp.where(row == 0, prev[7:8], pltpu.roll(u, 1, 0))
    u2 = jnp.where(row == 0, prev[6:7], jnp.where(row == 1, prev[7:8], pltpu.roll(u, 2, 0)))
    halo_ref[fj] = u[tm - 8:tm]
    cw = cw_ref[...]
    uc = cw[0:1] * u2 + cw[1:2] * u1 + cw[2:3] * u + cb_ref[...]
    h = _gelu_tanh(uc) * v
    acc_ref[...] += _dot(h.astype(BF16), wd_ref[...])

    @pl.when(fj == pl.num_programs(2) - 1)
    def _():
        y = x_ref[0] + gate_ref[0] * acc_ref[...]
        if final_norm:
            y = _rms(y, NORM_EPS) * fg_ref[...]
        o_ref[0] = y


def _conv_ffn(x, g, sc, sh, gate, w_up, conv_w, conv_b, w_down, final_g, *, final_norm, tm=512, fk=1408):
    bsz, t, d = x.shape
    f = w_down.shape[0]
    nf = f // fk
    kern = functools.partial(_ffn_kernel, final_norm=final_norm)
    return pl.pallas_call(
        kern,
        grid=(bsz, t // tm, nf),
        in_specs=[
            pl.BlockSpec((1, tm, d), lambda b, i, j: (b, i, 0)),
            pl.BlockSpec((1, d), lambda b, i, j: (0, 0)),
            pl.BlockSpec((1, 1, d), lambda b, i, j: (b, 0, 0)),
            pl.BlockSpec((1, 1, d), lambda b, i, j: (b, 0, 0)),
            pl.BlockSpec((1, 1, d), lambda b, i, j: (b, 0, 0)),
            pl.BlockSpec((d, fk), lambda b, i, j: (0, j)),
            pl.BlockSpec((d, fk), lambda b, i, j: (0, j + nf)),
            pl.BlockSpec((3, fk), lambda b, i, j: (0, j)),
            pl.BlockSpec((1, fk), lambda b, i, j: (0, j)),
            pl.BlockSpec((fk, d), lambda b, i, j: (j, 0)),
            pl.BlockSpec((1, d), lambda b, i, j: (0, 0)),
        ],
        out_specs=pl.BlockSpec((1, tm, d), lambda b, i, j: (b, i, 0)),
        out_shape=jax.ShapeDtypeStruct((bsz, t, d), F32),
        scratch_shapes=[pltpu.VMEM((tm, d), BF16), pltpu.VMEM((tm, d), F32), pltpu.VMEM((nf, 8, fk), F32)],
        compiler_params=_cparams(("parallel", "arbitrary", "arbitrary")),
        name="conv_ffn",
    )(x, g.reshape(1, d), sc, sh, gate, w_up, w_up, conv_w, conv_b.reshape(1, f), w_down, final_g.reshape(1, d))


def _gla_kernel(q_ref, k_ref, v_ref, og_ref, lr_ref, aup_ref, ab_ref, gg_ref, o_ref,
                st_ref, b_scr, bend_scr, q_scr, k_scr, kd_scr, qg_scr, vt_scr, o_scr):
    tb = q_ref.shape[1]
    cs = GLA_CHUNK
    n_chunks = tb // cs

    @pl.when(pl.program_id(1) == 0)
    def _():
        st_ref[...] = jnp.zeros_like(st_ref)

    z = _mm3(lr_ref[0], aup_ref[...]) + ab_ref[...]
    la = -_softplus(-z) * (1.0 / GLA_GATE_NORM)
    rr, cc = _iota((tb, tb), 0), _iota((tb, tb), 1)
    same = (rr // cs) == (cc // cs)
    b = _sel_mm(jnp.where(same & (cc <= rr), 1.0, 0.0), la)
    bend = _sel_mm(jnp.where(same, 1.0, 0.0), la)
    q = q_ref[0] * (GLA_DK ** -0.5)
    k = k_ref[0]
    b_scr[...] = b
    bend_scr[...] = bend
    q_scr[...] = q
    k_scr[...] = k
    kd_scr[...] = (k * jnp.exp(bend - b)).astype(BF16)
    qg_scr[...] = q * jnp.exp(b)
    v_all = v_ref[0]
    for h in range(GLA_HEADS):
        vt_scr[h] = v_all[:, h * GLA_DV:(h + 1) * GLA_DV].T.astype(BF16)

    lane_c = _iota((cs, LANES), 1)
    row_c = _iota((cs, LANES), 0)
    lane_t = _iota((tb, LANES), 1)
    head_rows = _iota((LANES, LANES), 0) // GLA_DK

    def chunk(c, carry):
        r0 = pl.multiple_of(c * cs, cs)
        tmask = (lane_t // cs) == c
        for p in range(GLA_HEADS // 2):
            lanes = slice(p * LANES, (p + 1) * LANES)
            bc = b_scr[pl.ds(r0, cs), lanes]
            qc = q_scr[pl.ds(r0, cs), lanes]
            kc = k_scr[pl.ds(r0, cs), lanes]
            blocks = []
            for s in range(cs):
                m = row_c >= s
                rel = jnp.where(m, bc - bc[s:s + 1], 0.0)
                blocks.append(jnp.where(m, qc * kc[s:s + 1] * jnp.exp(rel), 0.0))
            a_st = jnp.concatenate(blocks, axis=0)
            qgc = qg_scr[pl.ds(r0, cs), lanes]
            lhs = jnp.concatenate([jnp.where(lane_c < GLA_DK, qgc, 0.0),
                                   jnp.where(lane_c >= GLA_DK, qgc, 0.0)], axis=0)
            st = st_ref[p]
            o_inter = _mm_nt(lhs, st)
            for hh in range(2):
                h = 2 * p + hh
                zsum = _mm_sel(a_st, jnp.where(head_rows == hh, 1.0, 0.0), terms=2)
                vc = v_ref[0, pl.ds(r0, cs), h * GLA_DV:(h + 1) * GLA_DV]
                o_h = o_inter[hh * cs:(hh + 1) * cs]
                for s in range(cs):
                    o_h = o_h + zsum[s * cs:(s + 1) * cs] * vc[s:s + 1]
                o_scr[pl.ds(r0, cs), h * GLA_DV:(h + 1) * GLA_DV] = o_h
            lhs_u = jnp.concatenate([jnp.where(tmask, vt_scr[2 * p], 0.0).astype(BF16),
                                     jnp.where(tmask, vt_scr[2 * p + 1], 0.0).astype(BF16)], axis=0)
            upd = _dot(lhs_u, kd_scr[:, lanes])
            upd = jnp.where(lane_t < GLA_DK, upd[:GLA_DV], upd[GLA_DV:])
            decay = jnp.exp(bend_scr[pl.ds(r0, 1), lanes])
            st_ref[p] = st * decay + upd
        return carry

    lax.fori_loop(0, n_chunks, chunk, 0)

    og = og_ref[0]
    for h in range(GLA_HEADS):
        sl = slice(h * GLA_DV, (h + 1) * GLA_DV)
        gate = og[:, sl]
        o_ref[0, :, sl] = (_rms(o_scr[:, sl], NORM_EPS) * gg_ref[...] * (gate * _sigmoid(gate))).astype(o_ref.dtype)


def _gla(p, a_up_pad, a_b, gla_g, *, tb=128):
    bsz, t, _ = p.shape
    qk = GLA_HEADS * GLA_DK
    vw = GLA_HEADS * GLA_DV
    assert tb == LANES and GLA_DV == LANES
    return pl.pallas_call(
        _gla_kernel,
        grid=(bsz, t // tb),
        in_specs=[
            pl.BlockSpec((1, tb, qk), lambda b, i: (b, i, 0)),
            pl.BlockSpec((1, tb, qk), lambda b, i: (b, i, 1)),
            pl.BlockSpec((1, tb, vw), lambda b, i: (b, i, 1)),
            pl.BlockSpec((1, tb, vw), lambda b, i: (b, i, 2)),
            pl.BlockSpec((1, tb, LANES), lambda b, i: (b, i, 24)),
            pl.BlockSpec((LANES, qk), lambda b, i: (0, 0)),
            pl.BlockSpec((1, qk), lambda b, i: (0, 0)),
            pl.BlockSpec((1, GLA_DV), lambda b, i: (0, 0)),
        ],
        out_specs=pl.BlockSpec((1, tb, vw), lambda b, i: (b, i, 0)),
        out_shape=jax.ShapeDtypeStruct((bsz, t, vw), BF16),
        scratch_shapes=[
            pltpu.VMEM((GLA_HEADS // 2, GLA_DV, LANES), F32),
            pltpu.VMEM((tb, qk), F32), pltpu.VMEM((tb, qk), F32),
            pltpu.VMEM((tb, qk), F32), pltpu.VMEM((tb, qk), F32),
            pltpu.VMEM((tb, qk), BF16), pltpu.VMEM((tb, qk), F32),
            pltpu.VMEM((GLA_HEADS, GLA_DV, tb), BF16),
            pltpu.VMEM((tb, vw), F32),
        ],
        compiler_params=_cparams(("parallel", "arbitrary")),
        name="gla",
    )(p, p, p, p, p, a_up_pad, a_b.reshape(1, qk), gla_g.reshape(1, GLA_DV))


def _rwkv_kernel(r_ref, k_ref, v_ref, wa_ref, gl_ref, mur_ref, muk_ref, muv_ref, muwa_ref, mugl_ref,
                 w0_ref, w2_ref, a0_ref, a2_ref, g2_ref, kkw_ref, ka_ref, rk_ref, gnw_ref, gnb_ref,
                 o_ref,
                 lr_scr, lk_scr, lv_scr, lwa_scr, lgl_scr, s_ref,
                 lw_p, r_p, kk_p, be_p, k2_p, v_p, y_p, pp_all, y0_all, ge_all, g_all, h_all):
    tb = r_ref.shape[1]
    cs = RWKV_CHUNK
    n_chunks = tb // cs
    n_pairs = RWKV_HEADS // 2
    ti = pl.program_id(1)
    lasts = (lr_scr, lk_scr, lv_scr, lwa_scr, lgl_scr)

    @pl.when(ti == 0)
    def _():
        s_ref[...] = jnp.zeros_like(s_ref)
        for ref in lasts:
            ref[...] = jnp.zeros_like(ref)

    def shifted(x_ref, last_ref, mu_ref):
        x = x_ref[0]
        row = _iota(x.shape, 0)
        prev = jnp.where(row == 0, last_ref[7:8], pltpu.roll(x, 1, 0))
        last_ref[...] = x[tb - 8:tb]
        return x + (prev - x) * mu_ref[...]

    r = shifted(r_ref, lr_scr, mur_ref)
    k = shifted(k_ref, lk_scr, muk_ref)
    v = shifted(v_ref, lv_scr, muv_ref)
    wa = shifted(wa_ref, lwa_scr, muwa_ref)
    gl = shifted(gl_ref, lgl_scr, mugl_ref)

    logw = -jnp.exp(-_softplus(-(w0_ref[...] + _mm3(jnp.tanh(wa), w2_ref[...]))) - 0.5)
    a = _sigmoid(a0_ref[...] + _mm3(wa, a2_ref[...]))
    g = _mm3(_sigmoid(gl), g2_ref[...])
    seg = jnp.where((_iota((LANES, LANES), 0) // RWKV_N) == (_iota((LANES, LANES), 1) // RWKV_N), 1.0, 0.0)

    def segsum(x):
        return jnp.concatenate([_mm_sel(x[:, i * LANES:(i + 1) * LANES], seg) for i in range(n_pairs)], axis=1)

    kk = k * kkw_ref[...]
    kk = kk * lax.rsqrt(jnp.maximum(segsum(kk * kk), 1e-24))
    k2 = k * (1.0 + (a - 1.0) * ka_ref[...])
    beta = kk * a
    for p in range(n_pairs):
        sl = slice(p * LANES, (p + 1) * LANES)
        lw_p[p] = logw[:, sl]
        r_p[p] = r[:, sl]
        kk_p[p] = kk[:, sl]
        be_p[p] = beta[:, sl]
        k2_p[p] = k2[:, sl]
        v_p[p] = v[:, sl]

    rr, cc = _iota((tb, tb), 0), _iota((tb, tb), 1)
    same = (rr // cs) == (cc // cs)
    tri_incl = same & (cc <= rr)
    tri_strict = same & (cc < rr)
    l_incl = jnp.where(tri_incl, 1.0, 0.0)
    l_all = jnp.where(same, 1.0, 0.0)
    eye = jnp.where(rr == cc, 1.0, 0.0)
    lane = _iota((tb, LANES), 1)
    half = lane < RWKV_N
    same_half = (rr // RWKV_N) == (cc // RWKV_N)
    lane_c = _iota((cs, LANES), 1)
    lane_s = _iota((RWKV_N, LANES), 1)

    def below_left(s):
        return ((rr // (2 * s)) == (cc // (2 * s))) & ((rr % (2 * s)) >= s) & ((cc % (2 * s)) < s)

    pairs = range(n_pairs)
    heads = [(p, hh) for p in pairs for hh in range(2)]
    l_both = jnp.concatenate([l_incl, l_all], axis=0)
    rt, bt, at, vp, br, ak = {}, {}, {}, {}, {}, {}
    for p in pairs:
        lw = lw_p[p]
        sums = _sel_mm(l_both, lw)
        cum = sums[:tb]
        ge_all[p] = jnp.exp(sums[tb:])
        ig = jnp.exp(-cum)
        rt[p] = r_p[p] * jnp.exp(cum)
        bt[p] = kk_p[p] * jnp.exp(cum - lw)
        at[p] = -(be_p[p] * ig)
        vp[p] = v_p[p]
        br[p] = jnp.concatenate([bt[p], rt[p]], axis=0)
        ak[p] = jnp.concatenate([at[p], k2_p[p] * ig], axis=0)
    a_ab, a_ak, a_ra, a_rk = {}, {}, {}, {}
    for h in heads:
        p, hh = h
        hm = half if hh == 0 else jnp.logical_not(half)
        s4 = _mm3_nt(jnp.where(jnp.concatenate([hm, hm], axis=0), br[p], 0.0), ak[p])
        a_ab[h] = jnp.where(tri_strict, s4[:tb, :tb], 0.0)
        a_ak[h] = jnp.where(tri_strict, s4[:tb, tb:], 0.0)
        a_ra[h] = jnp.where(tri_incl, s4[tb:, :tb], 0.0)
        a_rk[h] = jnp.where(tri_incl, s4[tb:, tb:], 0.0)
    m_inv = {h: eye + jnp.where(below_left(1), a_ab[h], 0.0) for h in heads}
    w1 = {h: _mm3(a_ak[h], vp[h[0]]) for h in heads}
    s = 2
    while s < cs:
        low = {h: _mm3(m_inv[h], jnp.where(below_left(s), a_ab[h], 0.0)) for h in heads}
        m_inv = {h: m_inv[h] + _mm3(low[h], m_inv[h]) for h in heads}
        s *= 2
    mw = {h: _mm3(m_inv[h], jnp.concatenate([w1[h], bt[h[0]]], axis=1)) for h in heads}
    p_h = {h: _mm3(a_ra[h], mw[h][:, LANES:]) for h in heads}
    y0_h = {h: _mm3(a_ra[h], mw[h][:, :LANES]) + _mm3(a_rk[h], vp[h[0]]) for h in heads}
    zeros = jnp.zeros((tb, LANES), F32)
    for p in pairs:
        u0 = jnp.where(half, mw[(p, 0)][:, :LANES], mw[(p, 1)][:, :LANES])
        mb = jnp.where(half, mw[(p, 0)][:, LANES:], mw[(p, 1)][:, LANES:])
        pp_all[p] = rt[p] + jnp.where(half, p_h[(p, 0)], p_h[(p, 1)])
        y0_all[p] = jnp.where(half, y0_h[(p, 0)], y0_h[(p, 1)])
        u0t, vt, mbt = u0.T, vp[p].T, mb.T
        for c in range(n_chunks):
            tmask = (lane // cs) == c
            lhs = jnp.concatenate([
                jnp.concatenate([jnp.where(tmask, u0t, 0.0), jnp.where(tmask, vt, 0.0)], axis=1),
                jnp.concatenate([jnp.where(tmask, mbt, 0.0), zeros], axis=1)], axis=0)
            hg = _mm3(lhs, ak[p])
            h_all[p * n_chunks + c] = jnp.where(lane_s < RWKV_N, hg[:RWKV_N], hg[RWKV_N:tb])
            g_all[p * n_chunks + c] = jnp.where(same_half, hg[tb:], 0.0) + eye

    state = [s_ref[p] for p in range(n_pairs)]
    for c in range(n_chunks):
        rows = slice(c * cs, (c + 1) * cs)
        for p in range(n_pairs):
            sp = state[p]
            pc = pp_all[p, rows, :]
            lhs_y = jnp.concatenate([jnp.where(lane_c < RWKV_N, pc, 0.0), jnp.where(lane_c >= RWKV_N, pc, 0.0)], axis=0)
            yy = _mm3_nt(lhs_y, jnp.concatenate([sp, sp], axis=0))
            y_p[p, rows, :] = jnp.where(lane_c < RWKV_N, yy[:cs], yy[cs:]) + y0_all[p, rows, :]
            state[p] = (_mm3(sp, g_all[p * n_chunks + c]) + h_all[p * n_chunks + c]) * ge_all[p, c * cs:c * cs + 1, :]
    for p in range(n_pairs):
        s_ref[p] = state[p]

    y = jnp.concatenate([y_p[p] for p in range(n_pairs)], axis=1)
    mu = segsum(y) * (1.0 / RWKV_N)
    yc = y - mu
    var = segsum(yc * yc) * (1.0 / RWKV_N)
    yn = yc * lax.rsqrt(var + RWKV_GN_EPS) * gnw_ref[...] + gnb_ref[...]
    bonus = segsum(r * k2 * rk_ref[...]) * v
    o_ref[0] = ((yn + bonus) * g).astype(o_ref.dtype)


def _rwkv(p, mus, w0, w2p, a0, a2p, g2, k_k, k_a, r_k, gn_w, gn_b, *, tb=128):
    bsz, t, _ = p.shape
    w = RWKV_HEADS * RWKV_N
    n_pairs = RWKV_HEADS // 2
    assert tb == LANES
    row = lambda a: a.reshape(1, -1)
    full = lambda shape: pl.BlockSpec(shape, lambda b, i: (0,) * len(shape))
    ptile = lambda: pltpu.VMEM((n_pairs, tb, LANES), F32)
    return pl.pallas_call(
        _rwkv_kernel,
        grid=(bsz, t // tb),
        in_specs=[
            pl.BlockSpec((1, tb, w), lambda b, i: (b, i, 3)),
            pl.BlockSpec((1, tb, w), lambda b, i: (b, i, 4)),
            pl.BlockSpec((1, tb, w), lambda b, i: (b, i, 5)),
            pl.BlockSpec((1, tb, LANES), lambda b, i: (b, i, 25)),
            pl.BlockSpec((1, tb, LANES), lambda b, i: (b, i, 26)),
            full((1, w)), full((1, w)), full((1, w)), full((1, LANES)), full((1, LANES)),
            full((1, w)), full((LANES, w)), full((1, w)), full((LANES, w)), full((LANES, w)),
            full((1, w)), full((1, w)), full((1, w)), full((1, w)), full((1, w)),
        ],
        out_specs=pl.BlockSpec((1, tb, w), lambda b, i: (b, i, 0)),
        out_shape=jax.ShapeDtypeStruct((bsz, t, w), BF16),
        scratch_shapes=[
            pltpu.VMEM((8, w), F32), pltpu.VMEM((8, w), F32), pltpu.VMEM((8, w), F32),
            pltpu.VMEM((8, LANES), F32), pltpu.VMEM((8, LANES), F32),
            pltpu.VMEM((n_pairs, RWKV_N, LANES), F32),
            ptile(), ptile(), ptile(), ptile(), ptile(), ptile(), ptile(), ptile(), ptile(), ptile(),
            pltpu.VMEM((n_pairs * (tb // RWKV_CHUNK), LANES, LANES), F32),
            pltpu.VMEM((n_pairs * (tb // RWKV_CHUNK), RWKV_N, LANES), F32),
        ],
        compiler_params=_cparams(("parallel", "arbitrary")),
        name="rwkv7",
    )(p, p, p, p, p, *[row(m) for m in mus], row(w0), w2p, row(a0), a2p, g2,
      row(k_k), row(k_a), row(r_k), row(gn_w), row(gn_b))


def _cmp_kernel(sk_ref, sv_ref, wak_ref, wbk_ref, wav_ref, wbv_ref, pek_ref, pev_ref, w1k_ref, w1v_ref,
                w2k_ref, w2v_ref, ok_ref, ov_ref):
    def one(seg_ref, wa_ref, wb_ref, pe_ref, w1_ref, w2_ref, o_ref):
        seg = seg_ref[0].astype(BF16)
        first = _dot(seg, wa_ref[...])
        second = _dot(seg, wb_ref[...])
        n = first.shape[0]
        pe_term = _mm3(pe_ref[...], w1_ref[...])[0:1]
        hidden = _gelu_tanh(first + pltpu.roll(second, n - 1, 0) + pe_term)
        o_ref[0] = _mm(hidden, w2_ref[...])

    one(sk_ref, wak_ref, wbk_ref, pek_ref, w1k_ref, w2k_ref, ok_ref)
    one(sv_ref, wav_ref, wbv_ref, pev_ref, w1v_ref, w2v_ref, ov_ref)


def _nsa_compress(kc_tok, vc_tok, pe_k, w1_k, w2_k, pe_v, w1_v, w2_v):
    bsz, t, gw = kc_tok.shape
    st, dh, hid, g = NSA_CMP_STRIDE, NSA_DH, NSA_CMP_HIDDEN, NSA_GROUPS
    nseg = t // st
    eye = jnp.eye(g, dtype=F32)

    def expand_w1(w1):
        w = w1.reshape(NSA_CMP_LEN, dh, hid)
        big = jnp.einsum('ldc,gh->lgdhc', w, eye).reshape(NSA_CMP_LEN * g * dh, g * hid)
        half = st * g * dh
        return big[:half].astype(BF16), big[half:].astype(BF16)

    def expand_w2(w2):
        return jnp.einsum('cd,gh->gchd', w2, eye).reshape(g * hid, g * dh).astype(BF16)

    def pe_rows(pe):
        return jnp.zeros((8, NSA_CMP_LEN * dh), F32).at[0].set(pe.reshape(-1))

    wak, wbk = expand_w1(w1_k)
    wav, wbv = expand_w1(w1_v)
    full = lambda shape: pl.BlockSpec(shape, lambda b: (0,) * len(shape))
    seg_spec = pl.BlockSpec((1, nseg, st * gw), lambda b: (b, 0, 0))
    out_spec = pl.BlockSpec((1, nseg, gw), lambda b: (b, 0, 0))
    return pl.pallas_call(
        _cmp_kernel,
        grid=(bsz,),
        in_specs=[seg_spec, seg_spec,
                  full(wak.shape), full(wbk.shape), full(wav.shape), full(wbv.shape),
                  full((8, NSA_CMP_LEN * dh)), full((8, NSA_CMP_LEN * dh)),
                  full((NSA_CMP_LEN * dh, g * hid)), full((NSA_CMP_LEN * dh, g * hid)),
                  full((g * hid, gw)), full((g * hid, gw))],
        out_specs=[out_spec, out_spec],
        out_shape=[jax.ShapeDtypeStruct((bsz, nseg, gw), F32)] * 2,
        compiler_params=_cparams(("parallel",)),
        name="nsa_compress",
    )(kc_tok.reshape(bsz, nseg, st * gw), vc_tok.reshape(bsz, nseg, st * gw),
      wak, wbk, wav, wbv, pe_rows(pe_k), pe_rows(pe_v),
      jnp.tile(w1_k, (1, g)), jnp.tile(w1_v, (1, g)), expand_w2(w2_k), expand_w2(w2_v))


def _nsa_attn_kernel(q_ref, kc_ref, vc_ref, ks_ref, vs_ref, kw_ref, vw_ref, gt_ref, sl_ref, o_ref, acc_scr):
    g = pl.program_id(1)
    qi = pl.program_id(2)
    qb, dh, rep = NSA_Q_BLOCK, NSA_DH, NSA_REP
    n_cmp_pad = kc_ref.shape[2]
    kt = LANES
    rows_all = rep * qb

    lane_q = _iota((1, LANES), 1)
    q_rows = []
    for r in range(rep):
        s1, s2, s3 = (t.astype(F32) for t in _split3(sl_ref[0, r:r + 1, :]))
        slope_cols = jnp.where((lane_q >= dh) & (lane_q < dh + 2), s1,
                               jnp.where((lane_q >= dh + 2) & (lane_q < dh + 4), s2,
                                         jnp.where((lane_q >= dh + 4) & (lane_q < dh + 6), s3, 0.0)))
        q_rows.append(q_ref[0, r].astype(F32) * (dh ** -0.5) + slope_cols)
    q = jnp.concatenate(q_rows, axis=0).astype(BF16)
    t0 = qi * qb

    row_l = _iota((rows_all, kt), 0) % qb
    lane_k = _iota((rows_all, kt), 1)
    causal = row_l >= lane_k
    win_lo = row_l < lane_k

    def lane_tiles(s):
        return [s[:, j * kt:(j + 1) * kt] for j in range(s.shape[1] // kt)]

    def tile_max(tiles, start):
        m = start
        for tile in tiles:
            m = jnp.maximum(m, tile)
        return m

    def normalised(acc):
        return acc[:, :dh] / acc[:, dh:dh + 1]

    n_win = NSA_WINDOW // kt + 1
    win_s, win_v = [], []
    for w in range(n_win):
        kb = qi - (n_win - 1) + w
        k0 = pl.multiple_of(jnp.maximum(kb, 0) * kt, kt)
        s = _dot_nt(q, kw_ref[0, 0, pl.ds(k0, kt), :])
        if w == 0:
            s = jnp.where(win_lo, s, NEG_BIG)
        if w == n_win - 1:
            s = jnp.where(causal, s, NEG_BIG)
        else:
            s = s + jnp.where(kb >= 0, 0.0, NEG_BIG)
        win_s.append(s)
        win_v.append(vw_ref[0, 0, pl.ds(k0, kt), :])
    m_win = jnp.broadcast_to(jnp.max(tile_max(win_s[1:], win_s[0]), axis=-1, keepdims=True), (rows_all, kt))
    e_win = jnp.concatenate([jnp.exp(s - m_win).astype(BF16) for s in win_s], axis=1)
    o_win = normalised(_dot(e_win, jnp.concatenate(win_v, axis=0)))

    gsel = jnp.where(_iota((LANES, LANES), 0) == g * (rep * 3) + _iota((LANES, LANES), 1), 1.0, 0.0)
    gates = _sigmoid(_mm_sel(gt_ref[0], gsel))

    n_idx = _iota((qb, n_cmp_pad), 1)
    dist_c = t0 + _iota((qb, n_cmp_pad), 0) - (n_idx * NSA_CMP_STRIDE + NSA_CMP_LEN - 1)
    valid_c = (dist_c >= 0) & (n_idx < n_cmp_pad - 1)
    dist_cf = dist_c.astype(F32)
    s_all = _dot_nt(q, kc_ref[0, 0])
    vcmp = vc_ref[0, 0]
    n_sel_blocks = ks_ref.shape[2] // NSA_SEL_LEN
    on, oj = _iota((n_cmp_pad, LANES), 0), _iota((n_cmp_pad, LANES), 1)
    overlap = jnp.where((on * NSA_CMP_STRIDE <= oj * NSA_SEL_LEN + NSA_SEL_LEN - 1)
                        & (on * NSA_CMP_STRIDE + NSA_CMP_LEN - 1 >= oj * NSA_SEL_LEN)
                        & (oj < n_sel_blocks) & (on < n_cmp_pad - 1), 1.0, 0.0)
    imp = jnp.zeros((qb, LANES), F32)
    o_cmp = []
    for r in range(rep):
        s = s_all[r * qb:(r + 1) * qb] - sl_ref[0, r:r + 1, 0:1] * dist_cf
        s = jnp.where(valid_c, s, NEG_BIG)
        m = jnp.max(s, axis=-1, keepdims=True)
        e = jnp.where(valid_c, jnp.exp(s - m), 0.0)
        den = jnp.sum(e, axis=-1, keepdims=True)
        pr = e / jnp.where(den > 0.0, den, 1.0)
        o_cmp.append(_mm(pr, vcmp))
        imp = imp + _mm_sel(pr, overlap)

    imp_t = imp.T[:n_sel_blocks]
    jj = _iota((n_sel_blocks, qb), 0)
    jf = jj.astype(F32)
    ahead = (t0 + _iota((n_sel_blocks, qb), 1)) // NSA_SEL_LEN - jj
    valid_b = ahead >= 0
    forced = (jj == 0) | (valid_b & (ahead < NSA_N_LOCAL))
    score = jnp.where(valid_b, imp_t + jnp.where(forced, NSA_FORCE, 0.0), -NSA_FORCE)
    sel = jnp.zeros((n_sel_blocks, qb), F32)
    for _ in range(NSA_N_SEL):
        best = jnp.max(score, axis=0, keepdims=True)
        first = jnp.min(jnp.where(score == best, jf, float(n_sel_blocks)), axis=0, keepdims=True)
        pick = jf == first
        sel = jnp.where(pick, 1.0, sel)
        score = jnp.where(pick, NEG_BIG, score)
    sel_q = jnp.concatenate([sel, jnp.zeros((LANES - n_sel_blocks, qb), F32)], axis=0).T.astype(BF16)

    tpg = NSA_SWEEP_TILES
    kg = tpg * kt

    sel_bias = ((sel_q.astype(F32) - 1.0) * (2.0 ** 100)).astype(BF16)
    q_sel = jnp.concatenate([q, jnp.concatenate([sel_bias] * rep, axis=0)], axis=1)
    n_past = qi // tpg
    diag = qi - n_past * tpg

    def sweep(n_before):
        def run():
            mx = jnp.full((rows_all, kt), NEG_BIG, F32)
            for gi in range(n_before):
                mx = tile_max(lane_tiles(_dot_nt(q_sel, ks_ref[0, 0, gi * kg:(gi + 1) * kg, :])), mx)
            s_last = _dot_nt(q_sel, ks_ref[0, 0, n_before * kg:(n_before + 1) * kg, :])
            masked = []
            for j, tile in enumerate(lane_tiles(s_last)):
                shift = jnp.where(j < diag, -kt, jnp.where(j == diag, 0, kt))
                masked.append(jnp.where(row_l >= lane_k + shift, tile, NEG_BIG))
            m_row = jnp.max(tile_max(masked, mx), axis=-1, keepdims=True)
            m_full = jnp.broadcast_to(m_row, (rows_all, kt))
            e_last = jnp.concatenate([jnp.exp(tile - m_full).astype(BF16) for tile in masked], axis=1)
            acc = _dot(e_last, vs_ref[0, 0, n_before * kg:(n_before + 1) * kg, :])
            if n_before:
                m1, m2, m3 = _split3(-m_row)
                q_shift = jnp.where(lane_k == dh + 6, m1, jnp.where(lane_k == dh + 7, m2, jnp.where(lane_k == dh + 8, m3, q)))
                q_sel2 = jnp.concatenate([q_shift, q_sel[:, LANES:]], axis=1)
                for gi in range(n_before):
                    e = jnp.exp(_dot_nt(q_sel2, ks_ref[0, 0, gi * kg:(gi + 1) * kg, :])).astype(BF16)
                    acc = acc + _dot(e, vs_ref[0, 0, gi * kg:(gi + 1) * kg, :])
            acc_scr[...] = acc
        return run

    lax.switch(n_past, [sweep(n) for n in range(ks_ref.shape[2] // kg)])
    o_sel = normalised(acc_scr[...])

    for r in range(rep):
        rows = slice(r * qb, (r + 1) * qb)
        o_ref[0, r] = (gates[:, 3 * r:3 * r + 1] * o_cmp[r] + gates[:, 3 * r + 1:3 * r + 2] * o_sel[rows]
                       + gates[:, 3 * r + 2:3 * r + 3] * o_win[rows])


def _nsa_attention(q_h, kcmp, vcmp, ks, vs, kw, vw, p, slopes):
    bsz, heads, t, _ = q_h.shape
    g, rep, qb, dh = NSA_GROUPS, NSA_REP, NSA_Q_BLOCK, NSA_DH
    n_pad = kcmp.shape[2]
    whole = lambda a: pl.BlockSpec((1, 1) + a.shape[2:], lambda b, gi, i: (b, gi, 0, 0))
    gate_col = p.shape[-1] // LANES - 1
    return pl.pallas_call(
        _nsa_attn_kernel,
        grid=(bsz, g, t // qb),
        in_specs=[
            pl.BlockSpec((1, rep, qb, LANES), lambda b, gi, i: (b, gi, i, 0)),
            whole(kcmp), whole(vcmp), whole(ks), whole(vs), whole(kw), whole(vw),
            pl.BlockSpec((1, qb, LANES), lambda b, gi, i: (b, i, gate_col)),
            pl.BlockSpec((1, 8, LANES), lambda b, gi, i: (gi, 0, 0)),
        ],
        out_specs=pl.BlockSpec((1, rep, qb, dh), lambda b, gi, i: (b, gi, i, 0)),
        out_shape=jax.ShapeDtypeStruct((bsz, heads, t, dh), F32),
        scratch_shapes=[pltpu.VMEM((rep * qb, LANES), F32)],
        compiler_params=_cparams(("parallel", "parallel", "arbitrary")),
        name="nsa_attention",
    )(q_h, kcmp, vcmp, ks, vs, kw, vw, p, slopes)


def _even_mixer(x, norm_g, sc, sh, w_in, shift_mu, a_up, a_b, gla_g, w0, w2, a0, a2, g2, k_k, k_a, r_k, gn_w, gn_b):
    d = x.shape[-1]
    qk, vw, w = GLA_HEADS * GLA_DK, GLA_HEADS * GLA_DV, RWKV_HEADS * RWKV_N
    gla_cols = 2 * qk + 2 * vw + GLA_LOWRANK
    wg, wr = w_in[:, :gla_cols], w_in[:, gla_cols:]
    o_r, o_wl, o_k, o_v, o_al, o_gl = np.cumsum([0, w, RWKV_W_LORA, w, w, RWKV_A_LORA]).tolist()
    pad = lambda a, n: jnp.pad(a, ((0, 0), (0, n - a.shape[1])))
    w_perm = jnp.concatenate([
        wg[:, :2 * qk + 2 * vw],
        wr[:, o_r:o_r + w], wr[:, o_k:o_k + w], wr[:, o_v:o_v + w],
        pad(wg[:, 2 * qk + 2 * vw:], LANES),
        wr[:, o_wl:o_wl + RWKV_W_LORA], wr[:, o_al:o_al + RWKV_A_LORA],
        wr[:, o_gl:o_gl + RWKV_G_LORA]], axis=1).astype(BF16)
    p = _norm_proj(x, norm_g, sc, sh, w_perm)
    a_up_pad = jnp.zeros((LANES, qk), F32).at[:GLA_LOWRANK].set(a_up)
    o_gla = _gla(p, a_up_pad, a_b, gla_g)
    mu = shift_mu
    mus = [mu[o_r:o_r + w], mu[o_k:o_k + w], mu[o_v:o_v + w],
           jnp.concatenate([mu[o_wl:o_wl + RWKV_W_LORA], mu[o_al:o_al + RWKV_A_LORA]]), mu[o_gl:o_gl + RWKV_G_LORA]]
    w2p = jnp.zeros((LANES, w), F32).at[:RWKV_W_LORA].set(w2)
    a2p = jnp.zeros((LANES, w), F32).at[RWKV_W_LORA:RWKV_W_LORA + RWKV_A_LORA].set(a2)
    o_rw = _rwkv(p, mus, w0, w2p, a0, a2p, g2, k_k, k_a, r_k.reshape(-1), gn_w, gn_b)
    return jnp.concatenate([o_gla, o_rw], axis=-1)


def _nsa_mixer(x, norm_g, sc, sh, w_in, pe_k, w1_k, w2_k, pe_v, w1_v, w2_v):
    bsz, t, d = x.shape
    g, dh, heads = NSA_GROUPS, NSA_DH, NSA_HEADS
    n_cols = w_in.shape[1]
    n_pad = -(-n_cols // (3 * LANES)) * (3 * LANES)
    w_pad = jnp.pad(w_in, ((0, 0), (0, n_pad - n_cols))).astype(BF16)
    p = _norm_proj(x, norm_g, sc, sh, w_pad)
    kv = g * dh
    off = heads * dh
    seg = lambda i: p[..., off + i * kv: off + (i + 1) * kv]
    kcmp, vcmp = _nsa_compress(seg(0), seg(1), pe_k, w1_k, w2_k, pe_v, w1_v, w2_v)
    to_heads = lambda a, n: a.reshape(bsz, a.shape[1], n, dh).transpose(0, 2, 1, 3).astype(BF16)
    slopes = 2.0 ** (-8.0 * jnp.arange(1, heads + 1, dtype=F32) / heads)
    slopes = jnp.broadcast_to(jnp.pad(slopes.reshape(g, NSA_REP), ((0, 0), (0, 8 - NSA_REP)))[:, :, None], (g, 8, LANES))
    pos = jnp.arange(t)
    pos_cols = jnp.tile(jnp.stack([pos // LANES * LANES, pos % LANES], axis=-1), (1, 3)).astype(BF16)
    blk_cols = (pos[:, None] // NSA_SEL_LEN == jnp.arange(LANES)[None, :]).astype(BF16)
    ones_col = jnp.ones((t, 1), BF16)
    bcast = lambda cols, like: jnp.broadcast_to(cols, like.shape[:2] + cols.shape)
    pad_to = lambda a, n: jnp.pad(a, ((0, 0),) * (a.ndim - 1) + ((0, n - a.shape[-1]),))
    with_pos = lambda k: pad_to(jnp.concatenate([k, bcast(pos_cols, k), bcast(jnp.ones((t, 3), BF16), k)], axis=-1), LANES)
    with_ones = lambda v: pad_to(jnp.concatenate([v, bcast(ones_col, v)], axis=-1), LANES)
    ks, vs, kw, vw = (to_heads(seg(i), g) for i in (2, 3, 4, 5))
    ks_aug = jnp.concatenate([with_pos(ks), bcast(blk_cols, ks)], axis=-1)
    o = _nsa_attention(pad_to(to_heads(p[..., :off], heads), LANES), pad_to(to_heads(kcmp, g), LANES), to_heads(vcmp, g),
                       ks_aug, with_ones(vs), with_pos(kw), with_ones(vw), p, slopes)
    return o.transpose(0, 2, 1, 3).reshape(bsz, t, heads * dh)


def kernel(x, c, ada_w, ada_b, norm1_g, norm2_g, ffn_w_up, ffn_conv_w, ffn_conv_b, ffn_w_down, ev_w_in, ev_shift_mu, gla_a_up, gla_a_b, gla_norm_g, rw_w0, rw_w2, rw_a0, rw_a2, rw_g2, rw_k_k, rw_k_a, rw_r_k, rw_gn_w, rw_gn_b, ev_w_out, od_w_in, cmp_pe_k, cmp_w1_k, cmp_w2_k, cmp_pe_v, cmp_w1_v, cmp_w2_v, od_w_out, final_norm_g):
    bsz, t, d = x.shape
    depth = ada_w.shape[0]
    mod = _ada_mod(c, ada_w, ada_b)
    for layer in range(depth):
        sh1, sc1, g1, sh2, sc2, g2 = (mod[layer, :, i * d:(i + 1) * d].reshape(bsz, 1, d) for i in range(6))
        i = layer // 2
        if layer % 2 == 0:
            mix = _even_mixer(x, norm1_g[layer], sc1, sh1, ev_w_in[i], ev_shift_mu[i], gla_a_up[i], gla_a_b[i],
                              gla_norm_g[i], rw_w0[i], rw_w2[i], rw_a0[i], rw_a2[i], rw_g2[i], rw_k_k[i], rw_k_a[i],
                              rw_r_k[i], rw_gn_w[i], rw_gn_b[i])
            w_out = ev_w_out[i]
        else:
            mix = _nsa_mixer(x, norm1_g[layer], sc1, sh1, od_w_in[i], cmp_pe_k[i], cmp_w1_k[i], cmp_w2_k[i],
                             cmp_pe_v[i], cmp_w1_v[i], cmp_w2_v[i])
            w_out = od_w_out[i]
        x = _out_proj(mix, w_out.astype(BF16), x, g1)
        x = _conv_ffn(x, norm2_g[layer], sc2, sh2, g2, ffn_w_up[layer].astype(BF16), ffn_conv_w[layer],
                      ffn_conv_b[layer], ffn_w_down[layer].astype(BF16), final_norm_g,
                      final_norm=(layer == depth - 1))
    return x
```

```python
import functools

import numpy as np
import jax
import jax.numpy as jnp
from jax import lax
from jax.experimental import pallas as pl
from jax.experimental.pallas import tpu as pltpu

F32 = jnp.float32
BF16 = jnp.bfloat16

D_MODEL = 1024
NORM_EPS = 1e-6
GLA_HEADS, GLA_DK, GLA_DV, GLA_LOWRANK, GLA_GATE_NORM, GLA_CHUNK = 4, 64, 128, 16, 16.0, 16
RWKV_HEADS, RWKV_N, RWKV_GN_EPS = 8, 64, 64e-5
RWKV_W_LORA, RWKV_A_LORA, RWKV_G_LORA = 64, 64, 128
RWKV_CHUNK = 16
NSA_HEADS, NSA_GROUPS, NSA_DH = 16, 4, 64
NSA_REP = NSA_HEADS // NSA_GROUPS
NSA_CMP_LEN, NSA_CMP_STRIDE, NSA_CMP_HIDDEN = 32, 16, 64
NSA_SEL_LEN, NSA_N_SEL, NSA_N_LOCAL, NSA_WINDOW, NSA_Q_BLOCK, NSA_FORCE = 64, 8, 2, 512, 128, 100.0
NSA_SWEEP_TILES = 4
FFN_HIDDEN = 2816

LANES = 128
VMEM_LIMIT = 56 * 1024 * 1024
NEG_BIG = -1e30


def _cparams(sem):
    return pltpu.CompilerParams(dimension_semantics=sem, vmem_limit_bytes=VMEM_LIMIT)


def _dot(a, b):
    return jnp.dot(a, b, preferred_element_type=F32)


def _dot_nt(a, b):
    return lax.dot_general(a, b, (((1,), (1,)), ((), ())), preferred_element_type=F32)


def _mm(a, b):
    return _dot(a.astype(BF16), b.astype(BF16))


def _mm_nt(a, b):
    return _dot_nt(a.astype(BF16), b.astype(BF16))


def _split3(a):
    a1 = a.astype(BF16)
    r1 = a - a1.astype(F32)
    a2 = r1.astype(BF16)
    a3 = (r1 - a2.astype(F32)).astype(BF16)
    return a1, a2, a3


def _mm_sel(a, b01, terms=3):
    b = b01.astype(BF16)
    out = None
    for part in _split3(a)[:terms]:
        out = _dot(part, b) if out is None else out + _dot(part, b)
    return out


def _sel_mm(a01, b):
    b1, b2, b3 = _split3(b)
    a = a01.astype(BF16)
    return _dot(a, b1) + _dot(a, b2) + _dot(a, b3)


def _mm3(a, b):
    a1 = a.astype(BF16)
    a2 = (a - a1.astype(F32)).astype(BF16)
    b1 = b.astype(BF16)
    b2 = (b - b1.astype(F32)).astype(BF16)
    return _dot(a1, b1) + _dot(a1, b2) + _dot(a2, b1)


def _mm3_nt(a, b):
    a1 = a.astype(BF16)
    a2 = (a - a1.astype(F32)).astype(BF16)
    b1 = b.astype(BF16)
    b2 = (b - b1.astype(F32)).astype(BF16)
    return _dot_nt(a1, b1) + _dot_nt(a1, b2) + _dot_nt(a2, b1)


def _iota(shape, dim):
    return lax.broadcasted_iota(jnp.int32, shape, dim)


def _sigmoid(x):
    return 1.0 / (1.0 + jnp.exp(-x))


def _softplus(x):
    return jnp.maximum(x, 0.0) + jnp.log(1.0 + jnp.exp(-jnp.abs(x)))


def _gelu_tanh(x):
    return x * (0.5 * (1.0 + jnp.tanh(0.7978845608028654 * (x + 0.044715 * (x * x * x)))))


def _rms(x, eps):
    return x * lax.rsqrt(jnp.mean(x * x, axis=-1, keepdims=True) + eps)


def _mod_kernel(c_ref, w_ref, b_ref, o_ref):
    c = c_ref[...]
    cond = c * _sigmoid(c)
    o_ref[0] = _mm3(cond, w_ref[0]) + b_ref[0]


def _ada_mod(c, ada_w, ada_b):
    depth, d, n = ada_w.shape
    bsz = c.shape[0]
    rows = 8
    c8 = jnp.zeros((rows, d), F32).at[:bsz].set(c)
    tn = 1536
    out = pl.pallas_call(
        _mod_kernel,
        grid=(depth, n // tn),
        in_specs=[
            pl.BlockSpec((rows, d), lambda l, j: (0, 0)),
            pl.BlockSpec((1, d, tn), lambda l, j: (l, 0, j)),
            pl.BlockSpec((1, 1, tn), lambda l, j: (l, 0, j)),
        ],
        out_specs=pl.BlockSpec((1, rows, tn), lambda l, j: (l, 0, j)),
        out_shape=jax.ShapeDtypeStruct((depth, rows, n), F32),
        compiler_params=_cparams(("parallel", "parallel")),
        name="ada_mod",
    )(c8, ada_w, ada_b.reshape(depth, 1, n))
    return out[:, :bsz]


def _norm_proj_kernel(x_ref, g_ref, sc_ref, sh_ref, w_ref, o_ref, hn_ref):
    @pl.when(pl.program_id(2) == 0)
    def _():
        hn = _rms(x_ref[0], NORM_EPS) * g_ref[...]
        hn_ref[...] = (hn * (1.0 + sc_ref[0]) + sh_ref[0]).astype(BF16)

    o_ref[0] = _dot(hn_ref[...], w_ref[...])


def _norm_proj(x, g, sc, sh, w, *, tm=512, n_split=1):
    bsz, t, d = x.shape
    n = w.shape[1]
    tn = n // n_split
    return pl.pallas_call(
        _norm_proj_kernel,
        grid=(bsz, t // tm, n_split),
        in_specs=[
            pl.BlockSpec((1, tm, d), lambda b, i, j: (b, i, 0)),
            pl.BlockSpec((1, d), lambda b, i, j: (0, 0)),
            pl.BlockSpec((1, 1, d), lambda b, i, j: (b, 0, 0)),
            pl.BlockSpec((1, 1, d), lambda b, i, j: (b, 0, 0)),
            pl.BlockSpec((d, tn), lambda b, i, j: (0, j)),
        ],
        out_specs=pl.BlockSpec((1, tm, tn), lambda b, i, j: (b, i, j)),
        out_shape=jax.ShapeDtypeStruct((bsz, t, n), F32),
        scratch_shapes=[pltpu.VMEM((tm, d), BF16)],
        compiler_params=_cparams(("parallel", "parallel", "arbitrary")),
        name="norm_proj",
    )(x, g.reshape(1, d), sc, sh, w)


def _out_proj_kernel(*refs):
    *mw_refs, x_ref, gate_ref, o_ref = refs
    n = len(mw_refs) // 2
    proj = _mm(mw_refs[0][0], mw_refs[n][...])
    for m_ref, w_ref in zip(mw_refs[1:n], mw_refs[n + 1:]):
        proj = proj + _mm(m_ref[0], w_ref[...])
    o_ref[0] = x_ref[0] + gate_ref[0] * proj


def _out_proj(mixes, w, x, gate, *, tm=512):
    bsz, t, d = x.shape
    widths = [m.shape[-1] for m in mixes]
    offs = np.cumsum([0] + widths).tolist()
    ws = [w[o:o + k] for o, k in zip(offs, widths)]
    return pl.pallas_call(
        _out_proj_kernel,
        grid=(bsz, t // tm),
        in_specs=[pl.BlockSpec((1, tm, k), lambda b, i: (b, i, 0)) for k in widths]
        + [pl.BlockSpec((k, d), lambda b, i: (0, 0)) for k in widths]
        + [pl.BlockSpec((1, tm, d), lambda b, i: (b, i, 0)), pl.BlockSpec((1, 1, d), lambda b, i: (b, 0, 0))],
        out_specs=pl.BlockSpec((1, tm, d), lambda b, i: (b, i, 0)),
        out_shape=jax.ShapeDtypeStruct((bsz, t, d), F32),
        compiler_params=_cparams(("parallel", "parallel")),
        name="out_proj",
    )(*mixes, *ws, x, gate)


def _ffn_kernel(x_ref, g_ref, sc_ref, sh_ref, gate_ref, wu_ref, wv_ref, cw_ref, cb_ref, wd_ref, fg_ref,
                o_ref, hn_ref, acc_ref, halo_ref, *, final_norm):
    ti = pl.program_id(1)
    fj = pl.program_id(2)
    tm, fk = acc_ref.shape[0], wu_ref.shape[1]

    @pl.when(fj == 0)
    def _():
        hn = _rms(x_ref[0], NORM_EPS) * g_ref[...]
        hn_ref[...] = (hn * (1.0 + sc_ref[0]) + sh_ref[0]).astype(BF16)
        acc_ref[...] = jnp.zeros_like(acc_ref)

    @pl.when(ti == 0)
    def _():
        halo_ref[fj] = jnp.zeros((8, fk), F32)

    hn = hn_ref[...]
    u = _dot(hn, wu_ref[...])
    v = _dot(hn, wv_ref[...])
    prev = halo_ref[fj]
    row = _iota((tm, fk), 0)
    u1 = jnp.where(row == 0, prev[7:8], pltpu.roll(u, 1, 0))
    u2 = jnp.where(row == 0, prev[6:7], jnp.where(row == 1, prev[7:8], pltpu.roll(u, 2, 0)))
    halo_ref[fj] = u[tm - 8:tm]
    cw = cw_ref[...]
    uc = cw[0:1] * u2 + cw[1:2] * u1 + cw[2:3] * u + cb_ref[...]
    h = _gelu_tanh(uc) * v
    acc_ref[...] += _dot(h.astype(BF16), wd_ref[...])

    @pl.when(fj == pl.num_programs(2) - 1)
    def _():
        y = x_ref[0] + gate_ref[0] * acc_ref[...]
        if final_norm:
            y = _rms(y, NORM_EPS) * fg_ref[...]
        o_ref[0] = y


def _conv_ffn(x, g, sc, sh, gate, w_up, conv_w, conv_b, w_down, final_g, *, final_norm, tm=512, fk=1408):
    bsz, t, d = x.shape
    f = w_down.shape[0]
    nf = f // fk
    kern = functools.partial(_ffn_kernel, final_norm=final_norm)
    return pl.pallas_call(
        kern,
        grid=(bsz, t // tm, nf),
        in_specs=[
            pl.BlockSpec((1, tm, d), lambda b, i, j: (b, i, 0)),
            pl.BlockSpec((1, d), lambda b, i, j: (0, 0)),
            pl.BlockSpec((1, 1, d), lambda b, i, j: (b, 0, 0)),
            pl.BlockSpec((1, 1, d), lambda b, i, j: (b, 0, 0)),
            pl.BlockSpec((1, 1, d), lambda b, i, j: (b, 0, 0)),
            pl.BlockSpec((d, fk), lambda b, i, j: (0, j)),
            pl.BlockSpec((d, fk), lambda b, i, j: (0, j + nf)),
            pl.BlockSpec((3, fk), lambda b, i, j: (0, j)),
            pl.BlockSpec((1, fk), lambda b, i, j: (0, j)),
            pl.BlockSpec((fk, d), lambda b, i, j: (j, 0)),
            pl.BlockSpec((1, d), lambda b, i, j: (0, 0)),
        ],
        out_specs=pl.BlockSpec((1, tm, d), lambda b, i, j: (b, i, 0)),
        out_shape=jax.ShapeDtypeStruct((bsz, t, d), F32),
        scratch_shapes=[pltpu.VMEM((tm, d), BF16), pltpu.VMEM((tm, d), F32), pltpu.VMEM((nf, 8, fk), F32)],
        compiler_params=_cparams(("parallel", "arbitrary", "arbitrary")),
        name="conv_ffn",
    )(x, g.reshape(1, d), sc, sh, gate, w_up, w_up, conv_w, conv_b.reshape(1, f), w_down, final_g.reshape(1, d))


def _gla_kernel(q_ref, k_ref, v_ref, og_ref, lr_ref, aup_ref, ab_ref, gg_ref, o_ref,
                st_ref, b_scr, bend_scr, q_scr, k_scr, kd_scr, qg_scr, vt_scr, o_scr):
    tb = q_ref.shape[1]
    cs = GLA_CHUNK
    n_chunks = tb // cs

    @pl.when(pl.program_id(1) == 0)
    def _():
        st_ref[...] = jnp.zeros_like(st_ref)

    z = _mm3(lr_ref[0], aup_ref[...]) + ab_ref[...]
    la = -_softplus(-z) * (1.0 / GLA_GATE_NORM)
    rr, cc = _iota((tb, tb), 0), _iota((tb, tb), 1)
    same = (rr // cs) == (cc // cs)
    b = _sel_mm(jnp.where(same & (cc <= rr), 1.0, 0.0), la)
    bend = _sel_mm(jnp.where(same, 1.0, 0.0), la)
    q = q_ref[0] * (GLA_DK ** -0.5)
    k = k_ref[0]
    b_scr[...] = b
    bend_scr[...] = bend
    q_scr[...] = q
    k_scr[...] = k
    kd_scr[...] = (k * jnp.exp(bend - b)).astype(BF16)
    qg_scr[...] = q * jnp.exp(b)
    v_all = v_ref[0]
    for h in range(GLA_HEADS):
        vt_scr[h] = v_all[:, h * GLA_DV:(h + 1) * GLA_DV].T.astype(BF16)

    lane_c = _iota((cs, LANES), 1)
    row_c = _iota((cs, LANES), 0)
    lane_t = _iota((tb, LANES), 1)
    head_rows = _iota((LANES, LANES), 0) // GLA_DK

    def chunk(c, carry):
        r0 = c * cs
        tmask = (lane_t // cs) == c
        for p in range(GLA_HEADS // 2):
            lanes = slice(p * LANES, (p + 1) * LANES)
            bc = b_scr[pl.ds(r0, cs), lanes]
            qc = q_scr[pl.ds(r0, cs), lanes]
            kc = k_scr[pl.ds(r0, cs), lanes]
            blocks = []
            for s in range(cs):
                m = row_c >= s
                rel = jnp.where(m, bc - bc[s:s + 1], 0.0)
                blocks.append(jnp.where(m, qc * kc[s:s + 1] * jnp.exp(rel), 0.0))
            a_st = jnp.concatenate(blocks, axis=0)
            qgc = qg_scr[pl.ds(r0, cs), lanes]
            lhs = jnp.concatenate([jnp.where(lane_c < GLA_DK, qgc, 0.0),
                                   jnp.where(lane_c >= GLA_DK, qgc, 0.0)], axis=0)
            st = st_ref[p]
            o_inter = _mm_nt(lhs, st)
            for hh in range(2):
                h = 2 * p + hh
                zsum = _mm_sel(a_st, jnp.where(head_rows == hh, 1.0, 0.0), terms=2)
                vc = v_ref[0, pl.ds(r0, cs), h * GLA_DV:(h + 1) * GLA_DV]
                o_h = o_inter[hh * cs:(hh + 1) * cs]
                for s in range(cs):
                    o_h = o_h + zsum[s * cs:(s + 1) * cs] * vc[s:s + 1]
                o_scr[pl.ds(r0, cs), h * GLA_DV:(h + 1) * GLA_DV] = o_h
            lhs_u = jnp.concatenate([jnp.where(tmask, vt_scr[2 * p], 0.0).astype(BF16),
                                     jnp.where(tmask, vt_scr[2 * p + 1], 0.0).astype(BF16)], axis=0)
            upd = _dot(lhs_u, kd_scr[:, lanes])
            upd = jnp.where(lane_t < GLA_DK, upd[:GLA_DV], upd[GLA_DV:])
            decay = jnp.exp(bend_scr[pl.ds(r0, 1), lanes])
            st_ref[p] = st * decay + upd
        return carry

    for c in range(n_chunks):
        chunk(c, 0)

    og = og_ref[0]
    for h in range(GLA_HEADS):
        sl = slice(h * GLA_DV, (h + 1) * GLA_DV)
        gate = og[:, sl]
        o_ref[0, :, sl] = (_rms(o_scr[:, sl], NORM_EPS) * gg_ref[...] * (gate * _sigmoid(gate))).astype(o_ref.dtype)


def _gla(p, a_up_pad, a_b, gla_g, *, tb=128):
    bsz, t, _ = p.shape
    qk = GLA_HEADS * GLA_DK
    vw = GLA_HEADS * GLA_DV
    assert tb == LANES and GLA_DV == LANES
    return pl.pallas_call(
        _gla_kernel,
        grid=(bsz, t // tb),
        in_specs=[
            pl.BlockSpec((1, tb, qk), lambda b, i: (b, i, 0)),
            pl.BlockSpec((1, tb, qk), lambda b, i: (b, i, 1)),
            pl.BlockSpec((1, tb, vw), lambda b, i: (b, i, 1)),
            pl.BlockSpec((1, tb, vw), lambda b, i: (b, i, 2)),
            pl.BlockSpec((1, tb, LANES), lambda b, i: (b, i, 24)),
            pl.BlockSpec((LANES, qk), lambda b, i: (0, 0)),
            pl.BlockSpec((1, qk), lambda b, i: (0, 0)),
            pl.BlockSpec((1, GLA_DV), lambda b, i: (0, 0)),
        ],
        out_specs=pl.BlockSpec((1, tb, vw), lambda b, i: (b, i, 0)),
        out_shape=jax.ShapeDtypeStruct((bsz, t, vw), BF16),
        scratch_shapes=[
            pltpu.VMEM((GLA_HEADS // 2, GLA_DV, LANES), F32),
            pltpu.VMEM((tb, qk), F32), pltpu.VMEM((tb, qk), F32),
            pltpu.VMEM((tb, qk), F32), pltpu.VMEM((tb, qk), F32),
            pltpu.VMEM((tb, qk), BF16), pltpu.VMEM((tb, qk), F32),
            pltpu.VMEM((GLA_HEADS, GLA_DV, tb), BF16),
            pltpu.VMEM((tb, vw), F32),
        ],
        compiler_params=_cparams(("parallel", "arbitrary")),
        name="gla",
    )(p, p, p, p, p, a_up_pad, a_b.reshape(1, qk), gla_g.reshape(1, GLA_DV))


def _rwkv_kernel(r_ref, k_ref, v_ref, wa_ref, gl_ref, mur_ref, muk_ref, muv_ref, muwa_ref, mugl_ref,
                 w0_ref, w2_ref, a0_ref, a2_ref, g2_ref, kkw_ref, ka_ref, rk_ref, gnw_ref, gnb_ref,
                 o_ref,
                 lr_scr, lk_scr, lv_scr, lwa_scr, lgl_scr, s_ref,
                 lw_p, r_p, kk_p, be_p, k2_p, v_p, y_p, pp_all, y0_all, ge_all, g_all, h_all):
    tb = r_ref.shape[1]
    cs = RWKV_CHUNK
    n_chunks = tb // cs
    n_pairs = RWKV_HEADS // 2
    ti = pl.program_id(1)
    lasts = (lr_scr, lk_scr, lv_scr, lwa_scr, lgl_scr)

    @pl.when(ti == 0)
    def _():
        s_ref[...] = jnp.zeros_like(s_ref)
        for ref in lasts:
            ref[...] = jnp.zeros_like(ref)

    def shifted(x_ref, last_ref, mu_ref):
        x = x_ref[0]
        row = _iota(x.shape, 0)
        prev = jnp.where(row == 0, last_ref[7:8], pltpu.roll(x, 1, 0))
        last_ref[...] = x[tb - 8:tb]
        return x + (prev - x) * mu_ref[...]

    r = shifted(r_ref, lr_scr, mur_ref)
    k = shifted(k_ref, lk_scr, muk_ref)
    v = shifted(v_ref, lv_scr, muv_ref)
    wa = shifted(wa_ref, lwa_scr, muwa_ref)
    gl = shifted(gl_ref, lgl_scr, mugl_ref)

    logw = -jnp.exp(-_softplus(-(w0_ref[...] + _mm3(jnp.tanh(wa), w2_ref[...]))) - 0.5)
    a = _sigmoid(a0_ref[...] + _mm3(wa, a2_ref[...]))
    g = _mm3(_sigmoid(gl), g2_ref[...])
    seg = jnp.where((_iota((LANES, LANES), 0) // RWKV_N) == (_iota((LANES, LANES), 1) // RWKV_N), 1.0, 0.0)

    def segsum(x):
        return jnp.concatenate([_mm_sel(x[:, i * LANES:(i + 1) * LANES], seg) for i in range(n_pairs)], axis=1)

    kk = k * kkw_ref[...]
    kk = kk * lax.rsqrt(jnp.maximum(segsum(kk * kk), 1e-24))
    k2 = k * (1.0 + (a - 1.0) * ka_ref[...])
    beta = kk * a
    for p in range(n_pairs):
        sl = slice(p * LANES, (p + 1) * LANES)
        lw_p[p] = logw[:, sl]
        r_p[p] = r[:, sl]
        kk_p[p] = kk[:, sl]
        be_p[p] = beta[:, sl]
        k2_p[p] = k2[:, sl]
        v_p[p] = v[:, sl]

    rr, cc = _iota((tb, tb), 0), _iota((tb, tb), 1)
    same = (rr // cs) == (cc // cs)
    tri_incl = same & (cc <= rr)
    tri_strict = same & (cc < rr)
    l_incl = jnp.where(tri_incl, 1.0, 0.0)
    l_all = jnp.where(same, 1.0, 0.0)
    eye = jnp.where(rr == cc, 1.0, 0.0)
    lane = _iota((tb, LANES), 1)
    half = lane < RWKV_N
    same_half = (rr // RWKV_N) == (cc // RWKV_N)
    lane_c = _iota((cs, LANES), 1)
    lane_s = _iota((RWKV_N, LANES), 1)

    def below_left(s):
        return ((rr // (2 * s)) == (cc // (2 * s))) & ((rr % (2 * s)) >= s) & ((cc % (2 * s)) < s)

    pairs = range(n_pairs)
    heads = [(p, hh) for p in pairs for hh in range(2)]
    l_both = jnp.concatenate([l_incl, l_all], axis=0)
    rt, bt, at, vp, br, ak = {}, {}, {}, {}, {}, {}
    for p in pairs:
        lw = lw_p[p]
        sums = _sel_mm(l_both, lw)
        cum = sums[:tb]
        ge_all[p] = jnp.exp(sums[tb:])
        ig = jnp.exp(-cum)
        rt[p] = r_p[p] * jnp.exp(cum)
        bt[p] = kk_p[p] * jnp.exp(cum - lw)
        at[p] = -(be_p[p] * ig)
        vp[p] = v_p[p]
        br[p] = jnp.concatenate([bt[p], rt[p]], axis=0)
        ak[p] = jnp.concatenate([at[p], k2_p[p] * ig], axis=0)
    a_ab, a_ak, a_ra, a_rk = {}, {}, {}, {}
    for h in heads:
        p, hh = h
        hm = half if hh == 0 else jnp.logical_not(half)
        s4 = _mm3_nt(jnp.where(jnp.concatenate([hm, hm], axis=0), br[p], 0.0), ak[p])
        a_ab[h] = jnp.where(tri_strict, s4[:tb, :tb], 0.0)
        a_ak[h] = jnp.where(tri_strict, s4[:tb, tb:], 0.0)
        a_ra[h] = jnp.where(tri_incl, s4[tb:, :tb], 0.0)
        a_rk[h] = jnp.where(tri_incl, s4[tb:, tb:], 0.0)
    m_inv = {h: eye + jnp.where(below_left(1), a_ab[h], 0.0) for h in heads}
    w1 = {h: _mm3(a_ak[h], vp[h[0]]) for h in heads}
    s = 2
    while s < cs:
        low = {h: _mm3(m_inv[h], jnp.where(below_left(s), a_ab[h], 0.0)) for h in heads}
        m_inv = {h: m_inv[h] + _mm3(low[h], m_inv[h]) for h in heads}
        s *= 2
    mw = {h: _mm3(m_inv[h], jnp.concatenate([w1[h], bt[h[0]]], axis=1)) for h in heads}
    p_h = {h: _mm3(a_ra[h], mw[h][:, LANES:]) for h in heads}
    y0_h = {h: _mm3(a_ra[h], mw[h][:, :LANES]) + _mm3(a_rk[h], vp[h[0]]) for h in heads}
    zeros = jnp.zeros((tb, LANES), F32)
    for p in pairs:
        u0 = jnp.where(half, mw[(p, 0)][:, :LANES], mw[(p, 1)][:, :LANES])
        mb = jnp.where(half, mw[(p, 0)][:, LANES:], mw[(p, 1)][:, LANES:])
        pp_all[p] = rt[p] + jnp.where(half, p_h[(p, 0)], p_h[(p, 1)])
        y0_all[p] = jnp.where(half, y0_h[(p, 0)], y0_h[(p, 1)])
        u0t, vt, mbt = u0.T, vp[p].T, mb.T
        for c in range(n_chunks):
            tmask = (lane // cs) == c
            lhs = jnp.concatenate([
                jnp.concatenate([jnp.where(tmask, u0t, 0.0), jnp.where(tmask, vt, 0.0)], axis=1),
                jnp.concatenate([jnp.where(tmask, mbt, 0.0), zeros], axis=1)], axis=0)
            hg = _mm3(lhs, ak[p])
            h_all[p * n_chunks + c] = jnp.where(lane_s < RWKV_N, hg[:RWKV_N], hg[RWKV_N:tb])
            g_all[p * n_chunks + c] = jnp.where(same_half, hg[tb:], 0.0) + eye

    state = [s_ref[p] for p in range(n_pairs)]
    for c in range(n_chunks):
        rows = slice(c * cs, (c + 1) * cs)
        for p in range(n_pairs):
            sp = state[p]
            pc = pp_all[p, rows, :]
            lhs_y = jnp.concatenate([jnp.where(lane_c < RWKV_N, pc, 0.0), jnp.where(lane_c >= RWKV_N, pc, 0.0)], axis=0)
            yy = _mm3_nt(lhs_y, jnp.concatenate([sp, sp], axis=0))
            y_p[p, rows, :] = jnp.where(lane_c < RWKV_N, yy[:cs], yy[cs:]) + y0_all[p, rows, :]
            state[p] = (_mm3(sp, g_all[p * n_chunks + c]) + h_all[p * n_chunks + c]) * ge_all[p, c * cs:c * cs + 1, :]
    for p in range(n_pairs):
        s_ref[p] = state[p]

    y = jnp.concatenate([y_p[p] for p in range(n_pairs)], axis=1)
    mu = segsum(y) * (1.0 / RWKV_N)
    yc = y - mu
    var = segsum(yc * yc) * (1.0 / RWKV_N)
    yn = yc * lax.rsqrt(var + RWKV_GN_EPS) * gnw_ref[...] + gnb_ref[...]
    bonus = segsum(r * k2 * rk_ref[...]) * v
    o_ref[0] = ((yn + bonus) * g).astype(o_ref.dtype)


def _rwkv(p, mus, w0, w2p, a0, a2p, g2, k_k, k_a, r_k, gn_w, gn_b, *, tb=128):
    bsz, t, _ = p.shape
    w = RWKV_HEADS * RWKV_N
    n_pairs = RWKV_HEADS // 2
    assert tb == LANES
    row = lambda a: a.reshape(1, -1)
    full = lambda shape: pl.BlockSpec(shape, lambda b, i: (0,) * len(shape))
    ptile = lambda: pltpu.VMEM((n_pairs, tb, LANES), F32)
    return pl.pallas_call(
        _rwkv_kernel,
        grid=(bsz, t // tb),
        in_specs=[
            pl.BlockSpec((1, tb, w), lambda b, i: (b, i, 3)),
            pl.BlockSpec((1, tb, w), lambda b, i: (b, i, 4)),
            pl.BlockSpec((1, tb, w), lambda b, i: (b, i, 5)),
            pl.BlockSpec((1, tb, LANES), lambda b, i: (b, i, 25)),
            pl.BlockSpec((1, tb, LANES), lambda b, i: (b, i, 26)),
            full((1, w)), full((1, w)), full((1, w)), full((1, LANES)), full((1, LANES)),
            full((1, w)), full((LANES, w)), full((1, w)), full((LANES, w)), full((LANES, w)),
            full((1, w)), full((1, w)), full((1, w)), full((1, w)), full((1, w)),
        ],
        out_specs=pl.BlockSpec((1, tb, w), lambda b, i: (b, i, 0)),
        out_shape=jax.ShapeDtypeStruct((bsz, t, w), BF16),
        scratch_shapes=[
            pltpu.VMEM((8, w), F32), pltpu.VMEM((8, w), F32), pltpu.VMEM((8, w), F32),
            pltpu.VMEM((8, LANES), F32), pltpu.VMEM((8, LANES), F32),
            pltpu.VMEM((n_pairs, RWKV_N, LANES), F32),
            ptile(), ptile(), ptile(), ptile(), ptile(), ptile(), ptile(), ptile(), ptile(), ptile(),
            pltpu.VMEM((n_pairs * (tb // RWKV_CHUNK), LANES, LANES), F32),
            pltpu.VMEM((n_pairs * (tb // RWKV_CHUNK), RWKV_N, LANES), F32),
        ],
        compiler_params=_cparams(("parallel", "arbitrary")),
        name="rwkv7",
    )(p, p, p, p, p, *[row(m) for m in mus], row(w0), w2p, row(a0), a2p, g2,
      row(k_k), row(k_a), row(r_k), row(gn_w), row(gn_b))


def _cmp_kernel(sk_ref, sv_ref, wak_ref, wbk_ref, wav_ref, wbv_ref, pek_ref, pev_ref, w1k_ref, w1v_ref,
                w2k_ref, w2v_ref, ok_ref, ov_ref):
    def one(seg_ref, wa_ref, wb_ref, pe_ref, w1_ref, w2_ref, o_ref):
        seg = seg_ref[0].astype(BF16)
        first = _dot(seg, wa_ref[...])
        second = _dot(seg, wb_ref[...])
        n = first.shape[0]
        pe_term = _mm3(pe_ref[...], w1_ref[...])[0:1]
        hidden = _gelu_tanh(first + pltpu.roll(second, n - 1, 0) + pe_term)
        o_ref[0] = _mm(hidden, w2_ref[...])

    one(sk_ref, wak_ref, wbk_ref, pek_ref, w1k_ref, w2k_ref, ok_ref)
    one(sv_ref, wav_ref, wbv_ref, pev_ref, w1v_ref, w2v_ref, ov_ref)


def _nsa_compress(kc_tok, vc_tok, pe_k, w1_k, w2_k, pe_v, w1_v, w2_v):
    bsz, t, gw = kc_tok.shape
    st, dh, hid, g = NSA_CMP_STRIDE, NSA_DH, NSA_CMP_HIDDEN, NSA_GROUPS
    nseg = t // st
    eye = jnp.eye(g, dtype=F32)

    def expand_w1(w1):
        w = w1.reshape(NSA_CMP_LEN, dh, hid)
        big = jnp.einsum('ldc,gh->lgdhc', w, eye).reshape(NSA_CMP_LEN * g * dh, g * hid)
        half = st * g * dh
        return big[:half].astype(BF16), big[half:].astype(BF16)

    def expand_w2(w2):
        return jnp.einsum('cd,gh->gchd', w2, eye).reshape(g * hid, g * dh).astype(BF16)

    def pe_rows(pe):
        return jnp.zeros((8, NSA_CMP_LEN * dh), F32).at[0].set(pe.reshape(-1))

    wak, wbk = expand_w1(w1_k)
    wav, wbv = expand_w1(w1_v)
    full = lambda shape: pl.BlockSpec(shape, lambda b: (0,) * len(shape))
    seg_spec = pl.BlockSpec((1, nseg, st * gw), lambda b: (b, 0, 0))
    out_spec = pl.BlockSpec((1, nseg, gw), lambda b: (b, 0, 0))
    return pl.pallas_call(
        _cmp_kernel,
        grid=(bsz,),
        in_specs=[seg_spec, seg_spec,
                  full(wak.shape), full(wbk.shape), full(wav.shape), full(wbv.shape),
                  full((8, NSA_CMP_LEN * dh)), full((8, NSA_CMP_LEN * dh)),
                  full((NSA_CMP_LEN * dh, g * hid)), full((NSA_CMP_LEN * dh, g * hid)),
                  full((g * hid, gw)), full((g * hid, gw))],
        out_specs=[out_spec, out_spec],
        out_shape=[jax.ShapeDtypeStruct((bsz, nseg, gw), F32)] * 2,
        compiler_params=_cparams(("parallel",)),
        name="nsa_compress",
    )(kc_tok.reshape(bsz, nseg, st * gw), vc_tok.reshape(bsz, nseg, st * gw),
      wak, wbk, wav, wbv, pe_rows(pe_k), pe_rows(pe_v),
      jnp.tile(w1_k, (1, g)), jnp.tile(w1_v, (1, g)), expand_w2(w2_k), expand_w2(w2_v))


def _nsa_attn_kernel(q_ref, kc_ref, vc_ref, ks_ref, vs_ref, kw_ref, vw_ref, gt_ref, sl_ref, o_ref, acc_scr):
    g = pl.program_id(1)
    qi = pl.program_id(2)
    qb, dh, rep = NSA_Q_BLOCK, NSA_DH, NSA_REP
    n_cmp_pad = kc_ref.shape[2]
    kt = LANES
    rows_all = rep * qb

    lane_q = _iota((1, LANES), 1)
    lane_b = _iota((qb, LANES), 1)
    q_rows = []
    for r in range(rep):
        s1, s2, s3 = (t.astype(F32) for t in _split3(sl_ref[0, r:r + 1, :]))
        slope_cols = jnp.where((lane_q >= dh) & (lane_q < dh + 2), s1,
                               jnp.where((lane_q >= dh + 2) & (lane_q < dh + 4), s2,
                                         jnp.where((lane_q >= dh + 4) & (lane_q < dh + 6), s3, 0.0)))
        two_heads = q_ref[0, :, (r // 2) * LANES:(r // 2 + 1) * LANES]
        if r % 2:
            two_heads = pltpu.roll(two_heads, dh, 1)
        q_rows.append(jnp.where(lane_b < dh, two_heads * (dh ** -0.5), slope_cols))
    q = jnp.concatenate(q_rows, axis=0).astype(BF16)
    t0 = qi * qb

    row_l = _iota((rows_all, kt), 0) % qb
    lane_k = _iota((rows_all, kt), 1)
    causal = row_l >= lane_k
    win_lo = row_l < lane_k

    def lane_tiles(s):
        return [s[:, j * kt:(j + 1) * kt] for j in range(s.shape[1] // kt)]

    def tile_max(tiles, start):
        m = start
        for tile in tiles:
            m = jnp.maximum(m, tile)
        return m

    def normalised(acc):
        return acc / acc[:, dh:dh + 1]

    gsel = jnp.where(_iota((LANES, LANES), 0) == g * (rep * 3) + _iota((LANES, LANES), 1), 1.0, 0.0)
    gates = _sigmoid(_mm_sel(gt_ref[0], gsel))

    n_idx = _iota((rows_all, n_cmp_pad), 1)
    row_c = _iota((rows_all, n_cmp_pad), 0) % qb
    valid_c = (t0 + row_c >= n_idx * NSA_CMP_STRIDE + NSA_CMP_LEN - 1) & (n_idx < n_cmp_pad - 1)
    s_cmp = jnp.where(valid_c, _dot_nt(q, kc_ref[0, 0]), NEG_BIG)
    e_cmp = jnp.where(valid_c, jnp.exp(s_cmp - jnp.max(s_cmp, axis=-1, keepdims=True)), 0.0)
    den = jnp.sum(e_cmp, axis=-1, keepdims=True)
    p_cmp = e_cmp / jnp.where(den > 0.0, den, 1.0)
    o_cmp_all = _mm(p_cmp, vc_ref[0, 0])
    o_cmp = [o_cmp_all[r * qb:(r + 1) * qb] for r in range(rep)]
    n_sel_blocks = ks_ref.shape[2] // NSA_SEL_LEN
    on, oj = _iota((n_cmp_pad, LANES), 0), _iota((n_cmp_pad, LANES), 1)
    overlap = jnp.where((on * NSA_CMP_STRIDE <= oj * NSA_SEL_LEN + NSA_SEL_LEN - 1)
                        & (on * NSA_CMP_STRIDE + NSA_CMP_LEN - 1 >= oj * NSA_SEL_LEN)
                        & (oj < n_sel_blocks) & (on < n_cmp_pad - 1), 1.0, 0.0)
    p_group = p_cmp[0:qb]
    for r in range(1, rep):
        p_group = p_group + p_cmp[r * qb:(r + 1) * qb]
    imp = _mm_sel(p_group, overlap)

    imp_t = imp.T[:n_sel_blocks]
    jj = _iota((n_sel_blocks, qb), 0)
    jf = jj.astype(F32)
    ahead = (t0 + _iota((n_sel_blocks, qb), 1)) // NSA_SEL_LEN - jj
    valid_b = ahead >= 0
    forced = (jj == 0) | (valid_b & (ahead < NSA_N_LOCAL))
    score = jnp.where(valid_b, imp_t + jnp.where(forced, NSA_FORCE, 0.0), -NSA_FORCE)
    sel = jnp.zeros((n_sel_blocks, qb), F32)
    for _ in range(NSA_N_SEL):
        best = jnp.max(score, axis=0, keepdims=True)
        first = jnp.min(jnp.where(score == best, jf, float(n_sel_blocks)), axis=0, keepdims=True)
        pick = jf == first
        sel = jnp.where(pick, 1.0, sel)
        score = jnp.where(pick, NEG_BIG, score)
    sel_q = jnp.concatenate([sel, jnp.zeros((LANES - n_sel_blocks, qb), F32)], axis=0).T.astype(BF16)

    n_win = NSA_WINDOW // kt + 1
    win_s, win_v = [], []
    for w in range(n_win):
        kb = qi - (n_win - 1) + w
        k0 = pl.multiple_of(jnp.maximum(kb, 0) * kt, kt)
        s = _dot_nt(q, kw_ref[0, 0, pl.ds(k0, kt), :])
        if w == 0:
            s = jnp.where(win_lo, s, NEG_BIG)
        if w == n_win - 1:
            s = jnp.where(causal, s, NEG_BIG)
        else:
            s = s + jnp.where(kb >= 0, 0.0, NEG_BIG)
        win_s.append(s)
        win_v.append(vw_ref[0, 0, pl.ds(k0, kt), :])
    m_win = jnp.broadcast_to(jnp.max(tile_max(win_s[1:], win_s[0]), axis=-1, keepdims=True), (rows_all, kt))
    e_win = jnp.concatenate([jnp.exp(s - m_win).astype(BF16) for s in win_s], axis=1)
    o_win = normalised(_dot(e_win, jnp.concatenate(win_v, axis=0)))

    gate_of = lambda r, branch: jnp.broadcast_to(gates[:, 3 * r + branch:3 * r + branch + 1], (qb, LANES))
    partial = [gate_of(r, 0) * o_cmp[r] + gate_of(r, 2) * o_win[r * qb:(r + 1) * qb] for r in range(rep)]
    gate_sel = jnp.concatenate([gate_of(r, 1) for r in range(rep)], axis=0)

    tpg = NSA_SWEEP_TILES
    kg = tpg * kt

    sel_bias = ((sel_q.astype(F32) - 1.0) * (2.0 ** 100)).astype(BF16)
    q_sel = jnp.concatenate([q, jnp.concatenate([sel_bias] * rep, axis=0)], axis=1)
    n_past = qi // tpg
    diag = qi - n_past * tpg

    def sweep(n_before):
        def run():
            m_run, acc = None, None
            for gi in range(n_before + 1):
                tiles = lane_tiles(_dot_nt(q_sel, ks_ref[0, 0, gi * kg:(gi + 1) * kg, :]))
                if gi == n_before:
                    shifts = [jnp.where(j < diag, -kt, jnp.where(j == diag, 0, kt)) for j in range(tpg)]
                    tiles = [jnp.where(row_l >= lane_k + sh, tile, NEG_BIG) for sh, tile in zip(shifts, tiles)]
                m_grp = jnp.broadcast_to(jnp.max(tile_max(tiles[1:], tiles[0]), axis=-1, keepdims=True), (rows_all, kt))
                m_new = m_grp if m_run is None else jnp.maximum(m_run, m_grp)
                e = jnp.concatenate([jnp.exp(tile - m_new).astype(BF16) for tile in tiles], axis=1)
                pv = _dot(e, vs_ref[0, 0, gi * kg:(gi + 1) * kg, :])
                acc = pv if acc is None else jnp.exp(m_run - m_new) * acc + pv
                m_run = m_new
            acc_scr[...] = gate_sel * normalised(acc)
        return run

    lax.switch(n_past, [sweep(n) for n in range(ks_ref.shape[2] // kg)])
    mixed = [partial[r] + acc_scr[r * qb:(r + 1) * qb, :] for r in range(rep)]
    o_ref[0] = jnp.concatenate([jnp.where(lane_b < dh, mixed[r], pltpu.roll(mixed[r + 1], dh, 1))
                                for r in range(0, rep, 2)], axis=1).astype(o_ref.dtype)


def _nsa_attention(p, kcmp, vcmp, ks, vs, kw, vw, slopes):
    bsz, t, _ = p.shape
    g, rep, qb, dh = NSA_GROUPS, NSA_REP, NSA_Q_BLOCK, NSA_DH
    whole = lambda a: pl.BlockSpec((1, 1) + a.shape[2:], lambda b, gi, i: (b, gi, 0, 0))
    gate_col = p.shape[-1] // LANES - 1
    return pl.pallas_call(
        _nsa_attn_kernel,
        grid=(bsz, g, t // qb),
        in_specs=[
            pl.BlockSpec((1, qb, rep * dh), lambda b, gi, i: (b, i, gi)),
            whole(kcmp), whole(vcmp), whole(ks), whole(vs), whole(kw), whole(vw),
            pl.BlockSpec((1, qb, LANES), lambda b, gi, i: (b, i, gate_col)),
            pl.BlockSpec((1, 8, LANES), lambda b, gi, i: (gi, 0, 0)),
        ],
        out_specs=pl.BlockSpec((1, qb, rep * dh), lambda b, gi, i: (b, i, gi)),
        out_shape=jax.ShapeDtypeStruct((bsz, t, g * rep * dh), BF16),
        scratch_shapes=[pltpu.VMEM((rep * qb, LANES), F32)],
        compiler_params=_cparams(("parallel", "parallel", "arbitrary")),
        name="nsa_attention",
    )(p, kcmp, vcmp, ks, vs, kw, vw, p, slopes)


def _even_mixer(x, norm_g, sc, sh, w_in, shift_mu, a_up, a_b, gla_g, w0, w2, a0, a2, g2, k_k, k_a, r_k, gn_w, gn_b):
    d = x.shape[-1]
    qk, vw, w = GLA_HEADS * GLA_DK, GLA_HEADS * GLA_DV, RWKV_HEADS * RWKV_N
    gla_cols = 2 * qk + 2 * vw + GLA_LOWRANK
    wg, wr = w_in[:, :gla_cols], w_in[:, gla_cols:]
    o_r, o_wl, o_k, o_v, o_al, o_gl = np.cumsum([0, w, RWKV_W_LORA, w, w, RWKV_A_LORA]).tolist()
    pad = lambda a, n: jnp.pad(a, ((0, 0), (0, n - a.shape[1])))
    w_perm = jnp.concatenate([
        wg[:, :2 * qk + 2 * vw],
        wr[:, o_r:o_r + w], wr[:, o_k:o_k + w], wr[:, o_v:o_v + w],
        pad(wg[:, 2 * qk + 2 * vw:], LANES),
        wr[:, o_wl:o_wl + RWKV_W_LORA], wr[:, o_al:o_al + RWKV_A_LORA],
        wr[:, o_gl:o_gl + RWKV_G_LORA]], axis=1).astype(BF16)
    p = _norm_proj(x, norm_g, sc, sh, w_perm)
    a_up_pad = jnp.zeros((LANES, qk), F32).at[:GLA_LOWRANK].set(a_up)
    o_gla = _gla(p, a_up_pad, a_b, gla_g)
    mu = shift_mu
    mus = [mu[o_r:o_r + w], mu[o_k:o_k + w], mu[o_v:o_v + w],
           jnp.concatenate([mu[o_wl:o_wl + RWKV_W_LORA], mu[o_al:o_al + RWKV_A_LORA]]), mu[o_gl:o_gl + RWKV_G_LORA]]
    w2p = jnp.zeros((LANES, w), F32).at[:RWKV_W_LORA].set(w2)
    a2p = jnp.zeros((LANES, w), F32).at[RWKV_W_LORA:RWKV_W_LORA + RWKV_A_LORA].set(a2)
    o_rw = _rwkv(p, mus, w0, w2p, a0, a2p, g2, k_k, k_a, r_k.reshape(-1), gn_w, gn_b)
    return [o_gla, o_rw]


def _nsa_mixer(x, norm_g, sc, sh, w_in, pe_k, w1_k, w2_k, pe_v, w1_v, w2_v):
    bsz, t, d = x.shape
    g, dh, heads = NSA_GROUPS, NSA_DH, NSA_HEADS
    n_cols = w_in.shape[1]
    n_pad = -(-n_cols // (3 * LANES)) * (3 * LANES)
    w_pad = jnp.pad(w_in, ((0, 0), (0, n_pad - n_cols))).astype(BF16)
    p = _norm_proj(x, norm_g, sc, sh, w_pad)
    kv = g * dh
    off = heads * dh
    seg = lambda i: p[..., off + i * kv: off + (i + 1) * kv]
    kcmp, vcmp = _nsa_compress(seg(0), seg(1), pe_k, w1_k, w2_k, pe_v, w1_v, w2_v)
    to_heads = lambda a, n: a.reshape(bsz, a.shape[1], n, dh).transpose(0, 2, 1, 3).astype(BF16)
    slopes = 2.0 ** (-8.0 * jnp.arange(1, heads + 1, dtype=F32) / heads)
    slopes = jnp.broadcast_to(jnp.pad(slopes.reshape(g, NSA_REP), ((0, 0), (0, 8 - NSA_REP)))[:, :, None], (g, 8, LANES))
    pos = jnp.arange(t)
    split_pos = lambda q: jnp.tile(jnp.stack([q // LANES * LANES, q % LANES], axis=-1), (1, 3)).astype(BF16)
    pos_cols = split_pos(pos)
    blk_cols = (pos[:, None] // NSA_SEL_LEN == jnp.arange(LANES)[None, :]).astype(BF16)
    ones_col = jnp.ones((t, 1), BF16)
    bcast = lambda cols, like: jnp.broadcast_to(cols, like.shape[:2] + cols.shape)
    pad_to = lambda a, n: jnp.pad(a, ((0, 0),) * (a.ndim - 1) + ((0, n - a.shape[-1]),))
    with_pos = lambda k, cols=pos_cols: pad_to(jnp.concatenate([k, bcast(cols, k)], axis=-1), LANES)
    cmp_end_cols = split_pos(jnp.arange(kcmp.shape[1]) * NSA_CMP_STRIDE + NSA_CMP_LEN - 1)
    with_ones = lambda v: pad_to(jnp.concatenate([v, bcast(ones_col, v)], axis=-1), LANES)
    ks, vs, kw, vw = (to_heads(seg(i), g) for i in (2, 3, 4, 5))
    ks_aug = jnp.concatenate([with_pos(ks), bcast(blk_cols, ks)], axis=-1)
    return [_nsa_attention(p, with_pos(to_heads(kcmp, g), cmp_end_cols), pad_to(to_heads(vcmp, g), LANES),
                           ks_aug, with_ones(vs), with_pos(kw), with_ones(vw), slopes)]


def kernel(x, c, ada_w, ada_b, norm1_g, norm2_g, ffn_w_up, ffn_conv_w, ffn_conv_b, ffn_w_down, ev_w_in, ev_shift_mu, gla_a_up, gla_a_b, gla_norm_g, rw_w0, rw_w2, rw_a0, rw_a2, rw_g2, rw_k_k, rw_k_a, rw_r_k, rw_gn_w, rw_gn_b, ev_w_out, od_w_in, cmp_pe_k, cmp_w1_k, cmp_w2_k, cmp_pe_v, cmp_w1_v, cmp_w2_v, od_w_out, final_norm_g):
    bsz, t, d = x.shape
    depth = ada_w.shape[0]
    mod = _ada_mod(c, ada_w, ada_b)
    for layer in range(depth):
        sh1, sc1, g1, sh2, sc2, g2 = (mod[layer, :, i * d:(i + 1) * d].reshape(bsz, 1, d) for i in range(6))
        i = layer // 2
        if layer % 2 == 0:
            mix = _even_mixer(x, norm1_g[layer], sc1, sh1, ev_w_in[i], ev_shift_mu[i], gla_a_up[i], gla_a_b[i],
                              gla_norm_g[i], rw_w0[i], rw_w2[i], rw_a0[i], rw_a2[i], rw_g2[i], rw_k_k[i], rw_k_a[i],
                              rw_r_k[i], rw_gn_w[i], rw_gn_b[i])
            w_out = ev_w_out[i]
        else:
            mix = _nsa_mixer(x, norm1_g[layer], sc1, sh1, od_w_in[i], cmp_pe_k[i], cmp_w1_k[i], cmp_w2_k[i],
                             cmp_pe_v[i], cmp_w1_v[i], cmp_w2_v[i])
            w_out = od_w_out[i]
        x = _out_proj(mix, w_out.astype(BF16), x, g1)
        x = _conv_ffn(x, norm2_g[layer], sc2, sh2, g2, ffn_w_up[layer].astype(BF16), ffn_conv_w[layer],
                      ffn_conv_b[layer], ffn_w_down[layer].astype(BF16), final_norm_g,
                      final_norm=(layer == depth - 1))
    return x
```

```python
import functools

import numpy as np
import jax
import jax.numpy as jnp
from jax import lax
from jax.experimental import pallas as pl
from jax.experimental.pallas import tpu as pltpu

F32 = jnp.float32
BF16 = jnp.bfloat16

D_MODEL = 1024
NORM_EPS = 1e-6
GLA_HEADS, GLA_DK, GLA_DV, GLA_LOWRANK, GLA_GATE_NORM, GLA_CHUNK = 4, 64, 128, 16, 16.0, 16
RWKV_HEADS, RWKV_N, RWKV_GN_EPS = 8, 64, 64e-5
RWKV_W_LORA, RWKV_A_LORA, RWKV_G_LORA = 64, 64, 128
RWKV_CHUNK = 64
NSA_HEADS, NSA_GROUPS, NSA_DH = 16, 4, 64
NSA_REP = NSA_HEADS // NSA_GROUPS
NSA_CMP_LEN, NSA_CMP_STRIDE, NSA_CMP_HIDDEN = 32, 16, 64
NSA_SEL_LEN, NSA_N_SEL, NSA_N_LOCAL, NSA_WINDOW, NSA_Q_BLOCK, NSA_FORCE = 64, 8, 2, 512, 128, 100.0
NSA_SWEEP_TILES = 4
FFN_HIDDEN = 2816

LANES = 128
VMEM_LIMIT = 56 * 1024 * 1024
NEG_BIG = -1e30


def _cparams(sem):
    return pltpu.CompilerParams(dimension_semantics=sem, vmem_limit_bytes=VMEM_LIMIT)


def _dot(a, b):
    return jnp.dot(a, b, preferred_element_type=F32)


def _dot_nt(a, b):
    return lax.dot_general(a, b, (((1,), (1,)), ((), ())), preferred_element_type=F32)


def _mm(a, b):
    return _dot(a.astype(BF16), b.astype(BF16))


def _mm_nt(a, b):
    return _dot_nt(a.astype(BF16), b.astype(BF16))


def _split3(a):
    a1 = a.astype(BF16)
    r1 = a - a1.astype(F32)
    a2 = r1.astype(BF16)
    a3 = (r1 - a2.astype(F32)).astype(BF16)
    return a1, a2, a3


def _mm_sel(a, b01, terms=3):
    b = b01.astype(BF16)
    out = None
    for part in _split3(a)[:terms]:
        out = _dot(part, b) if out is None else out + _dot(part, b)
    return out


def _sel_mm(a01, b):
    b1, b2, b3 = _split3(b)
    a = a01.astype(BF16)
    return _dot(a, b1) + _dot(a, b2) + _dot(a, b3)


def _mm3(a, b):
    a1 = a.astype(BF16)
    a2 = (a - a1.astype(F32)).astype(BF16)
    b1 = b.astype(BF16)
    b2 = (b - b1.astype(F32)).astype(BF16)
    return _dot(a1, b1) + _dot(a1, b2) + _dot(a2, b1)


def _mm3_nt(a, b):
    a1 = a.astype(BF16)
    a2 = (a - a1.astype(F32)).astype(BF16)
    b1 = b.astype(BF16)
    b2 = (b - b1.astype(F32)).astype(BF16)
    return _dot_nt(a1, b1) + _dot_nt(a1, b2) + _dot_nt(a2, b1)


def _iota(shape, dim):
    return lax.broadcasted_iota(jnp.int32, shape, dim)


def _sigmoid(x):
    return 1.0 / (1.0 + jnp.exp(-x))


def _softplus(x):
    return jnp.maximum(x, 0.0) + jnp.log(1.0 + jnp.exp(-jnp.abs(x)))


def _gelu_tanh(x):
    return x * (0.5 * (1.0 + jnp.tanh(0.7978845608028654 * (x + 0.044715 * (x * x * x)))))


def _rms(x, eps):
    return x * lax.rsqrt(jnp.mean(x * x, axis=-1, keepdims=True) + eps)


def _mod_kernel(c_ref, w_ref, b_ref, o_ref):
    c = c_ref[...]
    cond = c * _sigmoid(c)
    o_ref[0] = _mm3(cond, w_ref[0]) + b_ref[0]


def _ada_mod(c, ada_w, ada_b):
    depth, d, n = ada_w.shape
    bsz = c.shape[0]
    rows = 8
    c8 = jnp.zeros((rows, d), F32).at[:bsz].set(c)
    tn = 1536
    out = pl.pallas_call(
        _mod_kernel,
        grid=(depth, n // tn),
        in_specs=[
            pl.BlockSpec((rows, d), lambda l, j: (0, 0)),
            pl.BlockSpec((1, d, tn), lambda l, j: (l, 0, j)),
            pl.BlockSpec((1, 1, tn), lambda l, j: (l, 0, j)),
        ],
        out_specs=pl.BlockSpec((1, rows, tn), lambda l, j: (l, 0, j)),
        out_shape=jax.ShapeDtypeStruct((depth, rows, n), F32),
        compiler_params=_cparams(("parallel", "parallel")),
        name="ada_mod",
    )(c8, ada_w, ada_b.reshape(depth, 1, n))
    return out[:, :bsz]


def _norm_proj_kernel(x_ref, g_ref, sc_ref, sh_ref, w_ref, o_ref, *copy_ref, copy_cols):
    hn = _rms(x_ref[0], NORM_EPS) * g_ref[...]
    hn = (hn * (1.0 + sc_ref[0]) + sh_ref[0]).astype(BF16)
    out = _dot(hn, w_ref[...])
    o_ref[0] = out
    if copy_cols:
        start, width = copy_cols
        copy_ref[0][0] = out[:, start:start + width].astype(BF16)


def _norm_proj(x, g, sc, sh, w, *, tm=512, copy_cols=None):
    bsz, t, d = x.shape
    n = w.shape[1]
    out_specs = [pl.BlockSpec((1, tm, n), lambda b, i: (b, i, 0))]
    out_shape = [jax.ShapeDtypeStruct((bsz, t, n), F32)]
    if copy_cols:
        out_specs.append(pl.BlockSpec((1, tm, copy_cols[1]), lambda b, i: (b, i, 0)))
        out_shape.append(jax.ShapeDtypeStruct((bsz, t, copy_cols[1]), BF16))
    outs = pl.pallas_call(
        functools.partial(_norm_proj_kernel, copy_cols=copy_cols),
        grid=(bsz, t // tm),
        in_specs=[
            pl.BlockSpec((1, tm, d), lambda b, i: (b, i, 0)),
            pl.BlockSpec((1, d), lambda b, i: (0, 0)),
            pl.BlockSpec((1, 1, d), lambda b, i: (b, 0, 0)),
            pl.BlockSpec((1, 1, d), lambda b, i: (b, 0, 0)),
            pl.BlockSpec((d, n), lambda b, i: (0, 0)),
        ],
        out_specs=out_specs,
        out_shape=out_shape,
        compiler_params=_cparams(("parallel", "parallel")),
        name="norm_proj",
    )(x, g.reshape(1, d), sc, sh, w)
    return outs if copy_cols else outs[0]


def _out_proj_kernel(*refs):
    *mw_refs, x_ref, gate_ref, o_ref = refs
    n = len(mw_refs) // 2
    proj = _mm(mw_refs[0][0], mw_refs[n][...])
    for m_ref, w_ref in zip(mw_refs[1:n], mw_refs[n + 1:]):
        proj = proj + _mm(m_ref[0], w_ref[...])
    o_ref[0] = x_ref[0] + gate_ref[0] * proj


def _out_proj(mixes, w, x, gate, *, tm=512):
    bsz, t, d = x.shape
    widths = [m.shape[-1] for m in mixes]
    offs = np.cumsum([0] + widths).tolist()
    ws = [w[o:o + k] for o, k in zip(offs, widths)]
    return pl.pallas_call(
        _out_proj_kernel,
        grid=(bsz, t // tm),
        in_specs=[pl.BlockSpec((1, tm, k), lambda b, i: (b, i, 0)) for k in widths]
        + [pl.BlockSpec((k, d), lambda b, i: (0, 0)) for k in widths]
        + [pl.BlockSpec((1, tm, d), lambda b, i: (b, i, 0)), pl.BlockSpec((1, 1, d), lambda b, i: (b, 0, 0))],
        out_specs=pl.BlockSpec((1, tm, d), lambda b, i: (b, i, 0)),
        out_shape=jax.ShapeDtypeStruct((bsz, t, d), F32),
        compiler_params=_cparams(("parallel", "parallel")),
        name="out_proj",
    )(*mixes, *ws, x, gate)


def _ffn_kernel(x_ref, g_ref, sc_ref, sh_ref, gate_ref, wu_ref, wv_ref, cw_ref, cb_ref, wd_ref, fg_ref,
                o_ref, hn_ref, acc_ref, halo_ref, *, final_norm):
    ti = pl.program_id(1)
    fj = pl.program_id(2)
    tm, fk = acc_ref.shape[0], wu_ref.shape[1]

    @pl.when(fj == 0)
    def _():
        hn = _rms(x_ref[0], NORM_EPS) * g_ref[...]
        hn_ref[...] = (hn * (1.0 + sc_ref[0]) + sh_ref[0]).astype(BF16)
        acc_ref[...] = jnp.zeros_like(acc_ref)

    @pl.when(ti == 0)
    def _():
        halo_ref[fj] = jnp.zeros((8, fk), F32)

    hn = hn_ref[...]
    u = _dot(hn, wu_ref[...])
    v = _dot(hn, wv_ref[...])
    prev = halo_ref[fj]
    row = _iota((tm, fk), 0)
    u1 = jnp.where(row == 0, prev[7:8], pltpu.roll(u, 1, 0))
    u2 = jnp.where(row == 0, prev[6:7], jnp.where(row == 1, prev[7:8], pltpu.roll(u, 2, 0)))
    halo_ref[fj] = u[tm - 8:tm]
    cw = cw_ref[...]
    uc = cw[0:1] * u2 + cw[1:2] * u1 + cw[2:3] * u + cb_ref[...]
    h = _gelu_tanh(uc) * v
    acc_ref[...] += _dot(h.astype(BF16), wd_ref[...])

    @pl.when(fj == pl.num_programs(2) - 1)
    def _():
        y = x_ref[0] + gate_ref[0] * acc_ref[...]
        if final_norm:
            y = _rms(y, NORM_EPS) * fg_ref[...]
        o_ref[0] = y


def _conv_ffn(x, g, sc, sh, gate, w_up, conv_w, conv_b, w_down, final_g, *, final_norm, tm=512, fk=1408):
    bsz, t, d = x.shape
    f = w_down.shape[0]
    nf = f // fk
    kern = functools.partial(_ffn_kernel, final_norm=final_norm)
    return pl.pallas_call(
        kern,
        grid=(bsz, t // tm, nf),
        in_specs=[
            pl.BlockSpec((1, tm, d), lambda b, i, j: (b, i, 0)),
            pl.BlockSpec((1, d), lambda b, i, j: (0, 0)),
            pl.BlockSpec((1, 1, d), lambda b, i, j: (b, 0, 0)),
            pl.BlockSpec((1, 1, d), lambda b, i, j: (b, 0, 0)),
            pl.BlockSpec((1, 1, d), lambda b, i, j: (b, 0, 0)),
            pl.BlockSpec((d, fk), lambda b, i, j: (0, j)),
            pl.BlockSpec((d, fk), lambda b, i, j: (0, j + nf)),
            pl.BlockSpec((3, fk), lambda b, i, j: (0, j)),
            pl.BlockSpec((1, fk), lambda b, i, j: (0, j)),
            pl.BlockSpec((fk, d), lambda b, i, j: (j, 0)),
            pl.BlockSpec((1, d), lambda b, i, j: (0, 0)),
        ],
        out_specs=pl.BlockSpec((1, tm, d), lambda b, i, j: (b, i, 0)),
        out_shape=jax.ShapeDtypeStruct((bsz, t, d), F32),
        scratch_shapes=[pltpu.VMEM((tm, d), BF16), pltpu.VMEM((tm, d), F32), pltpu.VMEM((nf, 8, fk), F32)],
        compiler_params=_cparams(("parallel", "arbitrary", "arbitrary")),
        name="conv_ffn",
    )(x, g.reshape(1, d), sc, sh, gate, w_up, w_up, conv_w, conv_b.reshape(1, f), w_down, final_g.reshape(1, d))


def _gla_kernel(q_ref, k_ref, v_ref, og_ref, lr_ref, aup_ref, ab_ref, gg_ref, o_ref,
                st_ref, b_scr, bend_scr, q_scr, k_scr, kd_scr, qg_scr, vt_scr, o_scr):
    tb = q_ref.shape[1]
    cs = GLA_CHUNK
    n_chunks = tb // cs

    @pl.when(pl.program_id(1) == 0)
    def _():
        st_ref[...] = jnp.zeros_like(st_ref)

    z = _mm3(lr_ref[0], aup_ref[...]) + ab_ref[...]
    la = -_softplus(-z) * (1.0 / GLA_GATE_NORM)
    rr, cc = _iota((tb, tb), 0), _iota((tb, tb), 1)
    same = (rr // cs) == (cc // cs)
    b = _sel_mm(jnp.where(same & (cc <= rr), 1.0, 0.0), la)
    bend = _sel_mm(jnp.where(same, 1.0, 0.0), la)
    q = q_ref[0] * (GLA_DK ** -0.5)
    k = k_ref[0]
    b_scr[...] = b
    bend_scr[...] = bend
    q_scr[...] = q
    k_scr[...] = k
    kd_scr[...] = (k * jnp.exp(bend - b)).astype(BF16)
    qg_scr[...] = q * jnp.exp(b)
    v_all = v_ref[0]
    for h in range(GLA_HEADS):
        vt_scr[h] = v_all[:, h * GLA_DV:(h + 1) * GLA_DV].T.astype(BF16)

    lane_c = _iota((cs, LANES), 1)
    row_c = _iota((cs, LANES), 0)
    lane_t = _iota((tb, LANES), 1)
    head_rows = _iota((LANES, LANES), 0) // GLA_DK

    def chunk(c, carry):
        r0 = c * cs
        tmask = (lane_t // cs) == c
        for p in range(GLA_HEADS // 2):
            lanes = slice(p * LANES, (p + 1) * LANES)
            bc = b_scr[pl.ds(r0, cs), lanes]
            qc = q_scr[pl.ds(r0, cs), lanes]
            kc = k_scr[pl.ds(r0, cs), lanes]
            blocks = []
            for s in range(cs):
                m = row_c >= s
                rel = jnp.where(m, bc - bc[s:s + 1], 0.0)
                blocks.append(jnp.where(m, qc * kc[s:s + 1] * jnp.exp(rel), 0.0))
            a_st = jnp.concatenate(blocks, axis=0)
            qgc = qg_scr[pl.ds(r0, cs), lanes]
            lhs = jnp.concatenate([jnp.where(lane_c < GLA_DK, qgc, 0.0),
                                   jnp.where(lane_c >= GLA_DK, qgc, 0.0)], axis=0)
            st = st_ref[p]
            o_inter = _mm_nt(lhs, st)
            for hh in range(2):
                h = 2 * p + hh
                zsum = _mm_sel(a_st, jnp.where(head_rows == hh, 1.0, 0.0), terms=2)
                vc = v_ref[0, pl.ds(r0, cs), h * GLA_DV:(h + 1) * GLA_DV]
                o_h = o_inter[hh * cs:(hh + 1) * cs]
                for s in range(cs):
                    o_h = o_h + zsum[s * cs:(s + 1) * cs] * vc[s:s + 1]
                o_scr[pl.ds(r0, cs), h * GLA_DV:(h + 1) * GLA_DV] = o_h
            lhs_u = jnp.concatenate([jnp.where(tmask, vt_scr[2 * p], 0.0).astype(BF16),
                                     jnp.where(tmask, vt_scr[2 * p + 1], 0.0).astype(BF16)], axis=0)
            upd = _dot(lhs_u, kd_scr[:, lanes])
            upd = jnp.where(lane_t < GLA_DK, upd[:GLA_DV], upd[GLA_DV:])
            decay = jnp.exp(bend_scr[pl.ds(r0, 1), lanes])
            st_ref[p] = st * decay + upd
        return carry

    for c in range(n_chunks):
        chunk(c, 0)

    og = og_ref[0]
    for h in range(GLA_HEADS):
        sl = slice(h * GLA_DV, (h + 1) * GLA_DV)
        gate = og[:, sl]
        o_ref[0, :, sl] = (_rms(o_scr[:, sl], NORM_EPS) * gg_ref[...] * (gate * _sigmoid(gate))).astype(o_ref.dtype)


def _gla(p, a_up_pad, a_b, gla_g, *, tb=128):
    bsz, t, _ = p.shape
    qk = GLA_HEADS * GLA_DK
    vw = GLA_HEADS * GLA_DV
    assert tb == LANES and GLA_DV == LANES
    return pl.pallas_call(
        _gla_kernel,
        grid=(bsz, t // tb),
        in_specs=[
            pl.BlockSpec((1, tb, qk), lambda b, i: (b, i, 0)),
            pl.BlockSpec((1, tb, qk), lambda b, i: (b, i, 1)),
            pl.BlockSpec((1, tb, vw), lambda b, i: (b, i, 1)),
            pl.BlockSpec((1, tb, vw), lambda b, i: (b, i, 2)),
            pl.BlockSpec((1, tb, LANES), lambda b, i: (b, i, 24)),
            pl.BlockSpec((LANES, qk), lambda b, i: (0, 0)),
            pl.BlockSpec((1, qk), lambda b, i: (0, 0)),
            pl.BlockSpec((1, GLA_DV), lambda b, i: (0, 0)),
        ],
        out_specs=pl.BlockSpec((1, tb, vw), lambda b, i: (b, i, 0)),
        out_shape=jax.ShapeDtypeStruct((bsz, t, vw), BF16),
        scratch_shapes=[
            pltpu.VMEM((GLA_HEADS // 2, GLA_DV, LANES), F32),
            pltpu.VMEM((tb, qk), F32), pltpu.VMEM((tb, qk), F32),
            pltpu.VMEM((tb, qk), F32), pltpu.VMEM((tb, qk), F32),
            pltpu.VMEM((tb, qk), BF16), pltpu.VMEM((tb, qk), F32),
            pltpu.VMEM((GLA_HEADS, GLA_DV, tb), BF16),
            pltpu.VMEM((tb, vw), F32),
        ],
        compiler_params=_cparams(("parallel", "arbitrary")),
        name="gla",
    )(p, p, p, p, p, a_up_pad, a_b.reshape(1, qk), gla_g.reshape(1, GLA_DV))


def _rwkv_kernel(r_ref, k_ref, v_ref, wa_ref, gl_ref, mur_ref, muk_ref, muv_ref, muwa_ref, mugl_ref,
                 w0_ref, w2_ref, a0_ref, a2_ref, g2_ref, kkw_ref, ka_ref, rk_ref, gnw_ref, gnb_ref,
                 o_ref,
                 lr_scr, lk_scr, lv_scr, lwa_scr, lgl_scr, s_ref,
                 lw_p, r_p, kk_p, be_p, k2_p, v_p, y_p, pp_all, y0_all, ge_all, g_all, h_all):
    tb = r_ref.shape[1]
    cs = RWKV_CHUNK
    n_chunks = tb // cs
    n_pairs = RWKV_HEADS // 2
    ti = pl.program_id(1)
    lasts = (lr_scr, lk_scr, lv_scr, lwa_scr, lgl_scr)

    @pl.when(ti == 0)
    def _():
        s_ref[...] = jnp.zeros_like(s_ref)
        for ref in lasts:
            ref[...] = jnp.zeros_like(ref)

    def shifted(x_ref, last_ref, mu_ref):
        x = x_ref[0]
        row = _iota(x.shape, 0)
        prev = jnp.where(row == 0, last_ref[7:8], pltpu.roll(x, 1, 0))
        last_ref[...] = x[tb - 8:tb]
        return x + (prev - x) * mu_ref[...]

    r = shifted(r_ref, lr_scr, mur_ref)
    k = shifted(k_ref, lk_scr, muk_ref)
    v = shifted(v_ref, lv_scr, muv_ref)
    wa = shifted(wa_ref, lwa_scr, muwa_ref)
    gl = shifted(gl_ref, lgl_scr, mugl_ref)

    logw = -jnp.exp(-_softplus(-(w0_ref[...] + _mm3(jnp.tanh(wa), w2_ref[...]))) - 0.5)
    a = _sigmoid(a0_ref[...] + _mm3(wa, a2_ref[...]))
    g = _mm(_sigmoid(gl), g2_ref[...])
    seg = jnp.where((_iota((LANES, LANES), 0) // RWKV_N) == (_iota((LANES, LANES), 1) // RWKV_N), 1.0, 0.0)

    def segsum(x):
        return jnp.concatenate([_mm_sel(x[:, i * LANES:(i + 1) * LANES], seg, terms=2) for i in range(n_pairs)], axis=1)

    kk = k * kkw_ref[...]
    kk = kk * lax.rsqrt(jnp.maximum(segsum(kk * kk), 1e-24))
    k2 = k * (1.0 + (a - 1.0) * ka_ref[...])
    beta = kk * a
    for p in range(n_pairs):
        sl = slice(p * LANES, (p + 1) * LANES)
        lw_p[p] = logw[:, sl]
        r_p[p] = r[:, sl]
        kk_p[p] = kk[:, sl]
        be_p[p] = beta[:, sl]
        k2_p[p] = k2[:, sl]
        v_p[p] = v[:, sl]

    rr, cc = _iota((tb, tb), 0), _iota((tb, tb), 1)
    same = (rr // cs) == (cc // cs)
    tri_incl = same & (cc <= rr)
    tri_strict = same & (cc < rr)
    l_incl = jnp.where(tri_incl, 1.0, 0.0)
    l_all = jnp.where(same, 1.0, 0.0)
    eye = jnp.where(rr == cc, 1.0, 0.0)
    lane = _iota((tb, LANES), 1)
    half = lane < RWKV_N
    same_half = (rr // RWKV_N) == (cc // RWKV_N)
    lane_c = _iota((cs, LANES), 1)
    lane_s = _iota((RWKV_N, LANES), 1)

    def below_left(s):
        return ((rr // (2 * s)) == (cc // (2 * s))) & ((rr % (2 * s)) >= s) & ((cc % (2 * s)) < s)

    pairs = range(n_pairs)
    heads = [(p, hh) for p in pairs for hh in range(2)]
    l_both = jnp.concatenate([l_incl, l_all], axis=0)
    rt, bt, at, vp, ak = {}, {}, {}, {}, {}
    for p in pairs:
        lw = lw_p[p]
        sums = _sel_mm(l_both, lw)
        cum = sums[:tb]
        ge_all[p] = jnp.exp(sums[tb:])
        ig = jnp.exp(-cum)
        rt[p] = r_p[p] * jnp.exp(cum)
        bt[p] = kk_p[p] * jnp.exp(cum - lw)
        at[p] = -(be_p[p] * ig)
        vp[p] = v_p[p]
        ak[p] = jnp.concatenate([at[p], k2_p[p] * ig], axis=0)
    a_ab, a_ak, a_ra, a_rk = {}, {}, {}, {}
    for h in heads:
        p, hh = h
        hm = half if hh == 0 else jnp.logical_not(half)
        s_b = _mm3_nt(jnp.where(hm, bt[p], 0.0), ak[p])
        s_r = _mm_nt(jnp.where(hm, rt[p], 0.0), ak[p])
        a_ab[h] = jnp.where(tri_strict, s_b[:, :tb], 0.0)
        a_ak[h] = jnp.where(tri_strict, s_b[:, tb:], 0.0)
        a_ra[h] = jnp.where(tri_incl, s_r[:, :tb], 0.0)
        a_rk[h] = jnp.where(tri_incl, s_r[:, tb:], 0.0)
    m_inv = {h: eye + jnp.where(below_left(1), a_ab[h], 0.0) for h in heads}
    w1 = {h: _mm3(a_ak[h], vp[h[0]]) for h in heads}
    s = 2
    while s < cs:
        low = {h: _mm3(m_inv[h], jnp.where(below_left(s), a_ab[h], 0.0)) for h in heads}
        m_inv = {h: m_inv[h] + _mm3(low[h], m_inv[h]) for h in heads}
        s *= 2
    mw = {h: _mm3(m_inv[h], jnp.concatenate([w1[h], bt[h[0]]], axis=1)) for h in heads}
    p_h = {h: _mm(a_ra[h], mw[h][:, LANES:]) for h in heads}
    y0_h = {h: _mm(a_ra[h], mw[h][:, :LANES]) + _mm(a_rk[h], vp[h[0]]) for h in heads}
    zeros = jnp.zeros((tb, LANES), F32)
    for p in pairs:
        u0 = jnp.where(half, mw[(p, 0)][:, :LANES], mw[(p, 1)][:, :LANES])
        mb = jnp.where(half, mw[(p, 0)][:, LANES:], mw[(p, 1)][:, LANES:])
        pp_all[p] = rt[p] + jnp.where(half, p_h[(p, 0)], p_h[(p, 1)])
        y0_all[p] = jnp.where(half, y0_h[(p, 0)], y0_h[(p, 1)])
        u0t, vt, mbt = u0.T, vp[p].T, mb.T
        for c in range(n_chunks):
            tmask = (lane // cs) == c
            lhs = jnp.concatenate([
                jnp.concatenate([jnp.where(tmask, u0t, 0.0), jnp.where(tmask, vt, 0.0)], axis=1),
                jnp.concatenate([jnp.where(tmask, mbt, 0.0), zeros], axis=1)], axis=0)
            hg = _mm3(lhs, ak[p])
            h_all[p * n_chunks + c] = jnp.where(lane_s < RWKV_N, hg[:RWKV_N], hg[RWKV_N:tb])
            g_all[p * n_chunks + c] = jnp.where(same_half, hg[tb:], 0.0) + eye

    state = [s_ref[p] for p in range(n_pairs)]
    for c in range(n_chunks):
        rows = slice(c * cs, (c + 1) * cs)
        for p in range(n_pairs):
            sp = state[p]
            pc = pp_all[p, rows, :]
            lhs_y = jnp.concatenate([jnp.where(lane_c < RWKV_N, pc, 0.0), jnp.where(lane_c >= RWKV_N, pc, 0.0)], axis=0)
            yy = _mm_nt(lhs_y, jnp.concatenate([sp, sp], axis=0))
            y_p[p, rows, :] = jnp.where(lane_c < RWKV_N, yy[:cs], yy[cs:]) + y0_all[p, rows, :]
            state[p] = (_mm3(sp, g_all[p * n_chunks + c]) + h_all[p * n_chunks + c]) * ge_all[p, c * cs:c * cs + 1, :]
    for p in range(n_pairs):
        s_ref[p] = state[p]

    y = jnp.concatenate([y_p[p] for p in range(n_pairs)], axis=1)
    mu = segsum(y) * (1.0 / RWKV_N)
    yc = y - mu
    var = segsum(yc * yc) * (1.0 / RWKV_N)
    yn = yc * lax.rsqrt(var + RWKV_GN_EPS) * gnw_ref[...] + gnb_ref[...]
    bonus = segsum(r * k2 * rk_ref[...]) * v
    o_ref[0] = ((yn + bonus) * g).astype(o_ref.dtype)


def _rwkv(p, mus, w0, w2p, a0, a2p, g2, k_k, k_a, r_k, gn_w, gn_b, *, tb=128):
    bsz, t, _ = p.shape
    w = RWKV_HEADS * RWKV_N
    n_pairs = RWKV_HEADS // 2
    assert tb == LANES
    row = lambda a: a.reshape(1, -1)
    full = lambda shape: pl.BlockSpec(shape, lambda b, i: (0,) * len(shape))
    ptile = lambda: pltpu.VMEM((n_pairs, tb, LANES), F32)
    return pl.pallas_call(
        _rwkv_kernel,
        grid=(bsz, t // tb),
        in_specs=[
            pl.BlockSpec((1, tb, w), lambda b, i: (b, i, 3)),
            pl.BlockSpec((1, tb, w), lambda b, i: (b, i, 4)),
            pl.BlockSpec((1, tb, w), lambda b, i: (b, i, 5)),
            pl.BlockSpec((1, tb, LANES), lambda b, i: (b, i, 25)),
            pl.BlockSpec((1, tb, LANES), lambda b, i: (b, i, 26)),
            full((1, w)), full((1, w)), full((1, w)), full((1, LANES)), full((1, LANES)),
            full((1, w)), full((LANES, w)), full((1, w)), full((LANES, w)), full((LANES, w)),
            full((1, w)), full((1, w)), full((1, w)), full((1, w)), full((1, w)),
        ],
        out_specs=pl.BlockSpec((1, tb, w), lambda b, i: (b, i, 0)),
        out_shape=jax.ShapeDtypeStruct((bsz, t, w), BF16),
        scratch_shapes=[
            pltpu.VMEM((8, w), F32), pltpu.VMEM((8, w), F32), pltpu.VMEM((8, w), F32),
            pltpu.VMEM((8, LANES), F32), pltpu.VMEM((8, LANES), F32),
            pltpu.VMEM((n_pairs, RWKV_N, LANES), F32),
            ptile(), ptile(), ptile(), ptile(), ptile(), ptile(), ptile(), ptile(), ptile(), ptile(),
            pltpu.VMEM((n_pairs * (tb // RWKV_CHUNK), LANES, LANES), F32),
            pltpu.VMEM((n_pairs * (tb // RWKV_CHUNK), RWKV_N, LANES), F32),
        ],
        compiler_params=_cparams(("parallel", "arbitrary")),
        name="rwkv7",
    )(p, p, p, p, p, *[row(m) for m in mus], row(w0), w2p, row(a0), a2p, g2,
      row(k_k), row(k_a), row(r_k), row(gn_w), row(gn_b))


def _cmp_kernel(sk_ref, sv_ref, wak_ref, wbk_ref, wav_ref, wbv_ref, pek_ref, pev_ref, w1k_ref, w1v_ref,
                w2k_ref, w2v_ref, ok_ref, ov_ref):
    def one(seg_ref, wa_ref, wb_ref, pe_ref, w1_ref, w2_ref, o_ref):
        seg = seg_ref[0].astype(BF16)
        first = _dot(seg, wa_ref[...])
        second = _dot(seg, wb_ref[...])
        n = first.shape[0]
        pe_term = _mm3(pe_ref[...], w1_ref[...])[0:1]
        hidden = _gelu_tanh(first + pltpu.roll(second, n - 1, 0) + pe_term)
        o_ref[0] = _mm(hidden, w2_ref[...])

    one(sk_ref, wak_ref, wbk_ref, pek_ref, w1k_ref, w2k_ref, ok_ref)
    one(sv_ref, wav_ref, wbv_ref, pev_ref, w1v_ref, w2v_ref, ov_ref)


def _nsa_compress(kc_tok, vc_tok, pe_k, w1_k, w2_k, pe_v, w1_v, w2_v):
    bsz, t, gw = kc_tok.shape
    st, dh, hid, g = NSA_CMP_STRIDE, NSA_DH, NSA_CMP_HIDDEN, NSA_GROUPS
    nseg = t // st
    eye = jnp.eye(g, dtype=F32)

    def expand_w1(w1):
        w = w1.reshape(NSA_CMP_LEN, dh, hid)
        big = jnp.einsum('ldc,gh->lgdhc', w, eye).reshape(NSA_CMP_LEN * g * dh, g * hid)
        half = st * g * dh
        return big[:half].astype(BF16), big[half:].astype(BF16)

    def expand_w2(w2):
        return jnp.einsum('cd,gh->gchd', w2, eye).reshape(g * hid, g * dh).astype(BF16)

    def pe_rows(pe):
        return jnp.zeros((8, NSA_CMP_LEN * dh), F32).at[0].set(pe.reshape(-1))

    wak, wbk = expand_w1(w1_k)
    wav, wbv = expand_w1(w1_v)
    full = lambda shape: pl.BlockSpec(shape, lambda b: (0,) * len(shape))
    seg_spec = pl.BlockSpec((1, nseg, st * gw), lambda b: (b, 0, 0))
    out_spec = pl.BlockSpec((1, nseg, gw), lambda b: (b, 0, 0))
    return pl.pallas_call(
        _cmp_kernel,
        grid=(bsz,),
        in_specs=[seg_spec, seg_spec,
                  full(wak.shape), full(wbk.shape), full(wav.shape), full(wbv.shape),
                  full((8, NSA_CMP_LEN * dh)), full((8, NSA_CMP_LEN * dh)),
                  full((NSA_CMP_LEN * dh, g * hid)), full((NSA_CMP_LEN * dh, g * hid)),
                  full((g * hid, gw)), full((g * hid, gw))],
        out_specs=[out_spec, out_spec],
        out_shape=[jax.ShapeDtypeStruct((bsz, nseg, gw), F32)] * 2,
        compiler_params=_cparams(("parallel",)),
        name="nsa_compress",
    )(kc_tok.reshape(bsz, nseg, st * gw), vc_tok.reshape(bsz, nseg, st * gw),
      wak, wbk, wav, wbv, pe_rows(pe_k), pe_rows(pe_v),
      jnp.tile(w1_k, (1, g)), jnp.tile(w1_v, (1, g)), expand_w2(w2_k), expand_w2(w2_v))


def _nsa_attn_kernel(q_ref, kc_ref, vc_ref, ks_ref, vs_ref, kw_ref, vw_ref, gt_ref, sl_ref, o_ref, acc_scr):
    g = pl.program_id(1)
    qi = pl.program_id(2)
    qb, dh, rep = NSA_Q_BLOCK, NSA_DH, NSA_REP
    n_cmp_pad = kc_ref.shape[1]
    kt = LANES
    rows_all = rep * qb

    lane_q = _iota((1, LANES), 1)
    lane_b = _iota((qb, LANES), 1)
    q_rows = []
    for r in range(rep):
        s1, s2, s3 = (t.astype(F32) for t in _split3(sl_ref[0, r:r + 1, :]))
        slope_cols = jnp.where((lane_q >= dh) & (lane_q < dh + 2), s1,
                               jnp.where((lane_q >= dh + 2) & (lane_q < dh + 4), s2,
                                         jnp.where((lane_q >= dh + 4) & (lane_q < dh + 6), s3, 0.0)))
        two_heads = q_ref[0, :, (r // 2) * LANES:(r // 2 + 1) * LANES]
        if r % 2:
            two_heads = pltpu.roll(two_heads, dh, 1)
        q_rows.append(jnp.where(lane_b < dh, two_heads * (dh ** -0.5), slope_cols))
    q = jnp.concatenate(q_rows, axis=0).astype(BF16)
    t0 = qi * qb

    row_l = _iota((rows_all, kt), 0) % qb
    lane_k = _iota((rows_all, kt), 1)
    causal = row_l >= lane_k
    win_lo = row_l < lane_k

    def lane_tiles(s):
        return [s[:, j * kt:(j + 1) * kt] for j in range(s.shape[1] // kt)]

    def tile_max(tiles, start):
        m = start
        for tile in tiles:
            m = jnp.maximum(m, tile)
        return m

    def normalised(acc):
        return acc / acc[:, dh:dh + 1]

    gsel = jnp.where(_iota((LANES, LANES), 0) == g * (rep * 3) + _iota((LANES, LANES), 1), 1.0, 0.0)
    gates = _sigmoid(_mm_sel(gt_ref[0], gsel))

    n_idx = _iota((rows_all, n_cmp_pad), 1)
    row_c = _iota((rows_all, n_cmp_pad), 0) % qb
    valid_c = (t0 + row_c >= n_idx * NSA_CMP_STRIDE + NSA_CMP_LEN - 1) & (n_idx < n_cmp_pad - 1)
    s_cmp = jnp.where(valid_c, _dot_nt(q, kc_ref[0]), NEG_BIG)
    e_cmp = jnp.where(valid_c, jnp.exp(s_cmp - jnp.max(s_cmp, axis=-1, keepdims=True)), 0.0)
    den = jnp.sum(e_cmp, axis=-1, keepdims=True)
    p_cmp = e_cmp / jnp.where(den > 0.0, den, 1.0)
    o_cmp_all = _mm(p_cmp, vc_ref[0])
    o_cmp = [o_cmp_all[r * qb:(r + 1) * qb] for r in range(rep)]
    n_sel_blocks = ks_ref.shape[1] // NSA_SEL_LEN
    on, oj = _iota((n_cmp_pad, LANES), 0), _iota((n_cmp_pad, LANES), 1)
    overlap = jnp.where((on * NSA_CMP_STRIDE <= oj * NSA_SEL_LEN + NSA_SEL_LEN - 1)
                        & (on * NSA_CMP_STRIDE + NSA_CMP_LEN - 1 >= oj * NSA_SEL_LEN)
                        & (oj < n_sel_blocks) & (on < n_cmp_pad - 1), 1.0, 0.0)
    p_group = p_cmp[0:qb]
    for r in range(1, rep):
        p_group = p_group + p_cmp[r * qb:(r + 1) * qb]
    imp = _mm_sel(p_group, overlap)

    imp_t = imp.T[:n_sel_blocks]
    jj = _iota((n_sel_blocks, qb), 0)
    jf = jj.astype(F32)
    ahead = (t0 + _iota((n_sel_blocks, qb), 1)) // NSA_SEL_LEN - jj
    valid_b = ahead >= 0
    forced = (jj == 0) | (valid_b & (ahead < NSA_N_LOCAL))
    score = jnp.where(valid_b, imp_t + jnp.where(forced, NSA_FORCE, 0.0), -NSA_FORCE)
    sel = jnp.zeros((n_sel_blocks, qb), F32)
    for _ in range(NSA_N_SEL):
        best = jnp.max(score, axis=0, keepdims=True)
        first = jnp.min(jnp.where(score == best, jf, float(n_sel_blocks)), axis=0, keepdims=True)
        pick = jf == first
        sel = jnp.where(pick, 1.0, sel)
        score = jnp.where(pick, NEG_BIG, score)
    sel_q = jnp.concatenate([sel, jnp.zeros((LANES - n_sel_blocks, qb), F32)], axis=0).T.astype(BF16)

    n_win = NSA_WINDOW // kt + 1
    win_s, win_v = [], []
    for w in range(n_win):
        kb = qi - (n_win - 1) + w
        k0 = pl.multiple_of(jnp.maximum(kb, 0) * kt, kt)
        s = _dot_nt(q, kw_ref[0, pl.ds(k0, kt), :])
        if w == 0:
            s = jnp.where(win_lo, s, NEG_BIG)
        if w == n_win - 1:
            s = jnp.where(causal, s, NEG_BIG)
        else:
            s = s + jnp.where(kb >= 0, 0.0, NEG_BIG)
        win_s.append(s)
        win_v.append(vw_ref[0, pl.ds(k0, kt), :])
    m_win = jnp.broadcast_to(jnp.max(tile_max(win_s[1:], win_s[0]), axis=-1, keepdims=True), (rows_all, kt))
    e_win = jnp.concatenate([jnp.exp(s - m_win).astype(BF16) for s in win_s], axis=1)
    o_win = normalised(_dot(e_win, jnp.concatenate(win_v, axis=0)))

    gate_of = lambda r, branch: jnp.broadcast_to(gates[:, 3 * r + branch:3 * r + branch + 1], (qb, LANES))
    partial = [gate_of(r, 0) * o_cmp[r] + gate_of(r, 2) * o_win[r * qb:(r + 1) * qb] for r in range(rep)]
    gate_sel = jnp.concatenate([gate_of(r, 1) for r in range(rep)], axis=0)

    tpg = NSA_SWEEP_TILES
    kg = tpg * kt

    sel_bias = ((sel_q.astype(F32) - 1.0) * (2.0 ** 100)).astype(BF16)
    q_sel = jnp.concatenate([q, jnp.concatenate([sel_bias] * rep, axis=0)], axis=1)
    n_past = qi // tpg
    diag = qi - n_past * tpg

    def sweep(n_before):
        def run():
            m_run, acc = None, None
            for gi in range(n_before + 1):
                tiles = lane_tiles(_dot_nt(q_sel, ks_ref[0, gi * kg:(gi + 1) * kg, :]))
                if gi == n_before:
                    shifts = [jnp.where(j < diag, -kt, jnp.where(j == diag, 0, kt)) for j in range(tpg)]
                    tiles = [jnp.where(row_l >= lane_k + sh, tile, NEG_BIG) for sh, tile in zip(shifts, tiles)]
                m_grp = jnp.broadcast_to(jnp.max(tile_max(tiles[1:], tiles[0]), axis=-1, keepdims=True), (rows_all, kt))
                m_new = m_grp if m_run is None else jnp.maximum(m_run, m_grp)
                e = jnp.concatenate([jnp.exp(tile - m_new).astype(BF16) for tile in tiles], axis=1)
                pv = _dot(e, vs_ref[0, gi * kg:(gi + 1) * kg, :])
                acc = pv if acc is None else jnp.exp(m_run - m_new) * acc + pv
                m_run = m_new
            acc_scr[...] = gate_sel * normalised(acc)
        return run

    lax.switch(n_past, [sweep(n) for n in range(ks_ref.shape[1] // kg)])
    mixed = [partial[r] + acc_scr[r * qb:(r + 1) * qb, :] for r in range(rep)]
    o_ref[0] = jnp.concatenate([jnp.where(lane_b < dh, mixed[r], pltpu.roll(mixed[r + 1], dh, 1))
                                for r in range(0, rep, 2)], axis=1).astype(o_ref.dtype)


def _nsa_attention(p, kcmp, vcmp, ks, vs, kw, vw, slopes):
    bsz, t, _ = p.shape
    g, rep, qb, dh = NSA_GROUPS, NSA_REP, NSA_Q_BLOCK, NSA_DH
    whole = lambda a: pl.BlockSpec((1, a.shape[1], a.shape[2] // g), lambda b, gi, i: (b, 0, gi))
    gate_col = p.shape[-1] // LANES - 1
    return pl.pallas_call(
        _nsa_attn_kernel,
        grid=(bsz, g, t // qb),
        in_specs=[
            pl.BlockSpec((1, qb, rep * dh), lambda b, gi, i: (b, i, gi)),
            whole(kcmp), whole(vcmp), whole(ks), whole(vs), whole(kw), whole(vw),
            pl.BlockSpec((1, qb, LANES), lambda b, gi, i: (b, i, gate_col)),
            pl.BlockSpec((1, 8, LANES), lambda b, gi, i: (gi, 0, 0)),
        ],
        out_specs=pl.BlockSpec((1, qb, rep * dh), lambda b, gi, i: (b, i, gi)),
        out_shape=jax.ShapeDtypeStruct((bsz, t, g * rep * dh), BF16),
        scratch_shapes=[pltpu.VMEM((rep * qb, LANES), F32)],
        compiler_params=_cparams(("parallel", "parallel", "arbitrary")),
        name="nsa_attention",
    )(p, kcmp, vcmp, ks, vs, kw, vw, p, slopes)


def _even_mixer(x, norm_g, sc, sh, w_in, shift_mu, a_up, a_b, gla_g, w0, w2, a0, a2, g2, k_k, k_a, r_k, gn_w, gn_b):
    d = x.shape[-1]
    qk, vw, w = GLA_HEADS * GLA_DK, GLA_HEADS * GLA_DV, RWKV_HEADS * RWKV_N
    gla_cols = 2 * qk + 2 * vw + GLA_LOWRANK
    wg, wr = w_in[:, :gla_cols], w_in[:, gla_cols:]
    o_r, o_wl, o_k, o_v, o_al, o_gl = np.cumsum([0, w, RWKV_W_LORA, w, w, RWKV_A_LORA]).tolist()
    pad = lambda a, n: jnp.pad(a, ((0, 0), (0, n - a.shape[1])))
    w_perm = jnp.concatenate([
        wg[:, :2 * qk + 2 * vw],
        wr[:, o_r:o_r + w], wr[:, o_k:o_k + w], wr[:, o_v:o_v + w],
        pad(wg[:, 2 * qk + 2 * vw:], LANES),
        wr[:, o_wl:o_wl + RWKV_W_LORA], wr[:, o_al:o_al + RWKV_A_LORA],
        wr[:, o_gl:o_gl + RWKV_G_LORA]], axis=1).astype(BF16)
    p = _norm_proj(x, norm_g, sc, sh, w_perm)
    a_up_pad = jnp.zeros((LANES, qk), F32).at[:GLA_LOWRANK].set(a_up)
    o_gla = _gla(p, a_up_pad, a_b, gla_g)
    mu = shift_mu
    mus = [mu[o_r:o_r + w], mu[o_k:o_k + w], mu[o_v:o_v + w],
           jnp.concatenate([mu[o_wl:o_wl + RWKV_W_LORA], mu[o_al:o_al + RWKV_A_LORA]]), mu[o_gl:o_gl + RWKV_G_LORA]]
    w2p = jnp.zeros((LANES, w), F32).at[:RWKV_W_LORA].set(w2)
    a2p = jnp.zeros((LANES, w), F32).at[RWKV_W_LORA:RWKV_W_LORA + RWKV_A_LORA].set(a2)
    o_rw = _rwkv(p, mus, w0, w2p, a0, a2p, g2, k_k, k_a, r_k.reshape(-1), gn_w, gn_b)
    return [o_gla, o_rw]


def _nsa_mixer(x, norm_g, sc, sh, w_in, pe_k, w1_k, w2_k, pe_v, w1_v, w2_v):
    bsz, t, d = x.shape
    g, dh, heads = NSA_GROUPS, NSA_DH, NSA_HEADS
    n_cols = w_in.shape[1]
    n_pad = -(-n_cols // (3 * LANES)) * (3 * LANES)
    w_pad = jnp.pad(w_in, ((0, 0), (0, n_pad - n_cols))).astype(BF16)
    kv = g * dh
    off = heads * dh
    p, p_kv = _norm_proj(x, norm_g, sc, sh, w_pad, copy_cols=(off + 2 * kv, 4 * kv))
    seg = lambda i: p[..., off + i * kv: off + (i + 1) * kv]
    kcmp, vcmp = _nsa_compress(seg(0), seg(1), pe_k, w1_k, w2_k, pe_v, w1_v, w2_v)
    slopes = 2.0 ** (-8.0 * jnp.arange(1, heads + 1, dtype=F32) / heads)
    slopes = jnp.broadcast_to(jnp.pad(slopes.reshape(g, NSA_REP), ((0, 0), (0, 8 - NSA_REP)))[:, :, None], (g, 8, LANES))

    def per_group(a, *cols, width):
        rows = a.shape[1]
        parts = [a.reshape(bsz, rows, g, dh).astype(BF16)]
        parts += [jnp.broadcast_to(c.astype(BF16)[None, :, None, :], (bsz, rows, g, c.shape[-1])) for c in cols]
        used = sum(q.shape[-1] for q in parts)
        parts.append(jnp.zeros((bsz, rows, g, width - used), BF16))
        return jnp.concatenate(parts, axis=-1).reshape(bsz, rows, g * width)

    split_pos = lambda q: jnp.tile(jnp.stack([q // LANES * LANES, q % LANES], axis=-1), (1, 3))
    pos = jnp.arange(t)
    pos_cols = split_pos(pos)
    gap = jnp.zeros((t, LANES - dh - pos_cols.shape[1]), BF16)
    blk_cols = pos[:, None] // NSA_SEL_LEN == jnp.arange(LANES)[None, :]
    ones_col = jnp.ones((t, 1), BF16)
    cmp_end_cols = split_pos(jnp.arange(kcmp.shape[1]) * NSA_CMP_STRIDE + NSA_CMP_LEN - 1)
    kv_seg = lambda i: p_kv[..., i * kv:(i + 1) * kv]
    return [_nsa_attention(p, per_group(kcmp, cmp_end_cols, width=LANES), per_group(vcmp, width=LANES),
                           per_group(kv_seg(0), pos_cols, gap, blk_cols, width=2 * LANES),
                           per_group(kv_seg(1), ones_col, width=LANES),
                           per_group(kv_seg(2), pos_cols, width=LANES),
                           per_group(kv_seg(3), ones_col, width=LANES), slopes)]


def kernel(x, c, ada_w, ada_b, norm1_g, norm2_g, ffn_w_up, ffn_conv_w, ffn_conv_b, ffn_w_down, ev_w_in, ev_shift_mu, gla_a_up, gla_a_b, gla_norm_g, rw_w0, rw_w2, rw_a0, rw_a2, rw_g2, rw_k_k, rw_k_a, rw_r_k, rw_gn_w, rw_gn_b, ev_w_out, od_w_in, cmp_pe_k, cmp_w1_k, cmp_w2_k, cmp_pe_v, cmp_w1_v, cmp_w2_v, od_w_out, final_norm_g):
    bsz, t, d = x.shape
    depth = ada_w.shape[0]
    mod = _ada_mod(c, ada_w, ada_b)
    for layer in range(depth):
        sh1, sc1, g1, sh2, sc2, g2 = (mod[layer, :, i * d:(i + 1) * d].reshape(bsz, 1, d) for i in range(6))
        i = layer // 2
        if layer % 2 == 0:
            mix = _even_mixer(x, norm1_g[layer], sc1, sh1, ev_w_in[i], ev_shift_mu[i], gla_a_up[i], gla_a_b[i],
                              gla_norm_g[i], rw_w0[i], rw_w2[i], rw_a0[i], rw_a2[i], rw_g2[i], rw_k_k[i], rw_k_a[i],
                              rw_r_k[i], rw_gn_w[i], rw_gn_b[i])
            w_out = ev_w_out[i]
        else:
            mix = _nsa_mixer(x, norm1_g[layer], sc1, sh1, od_w_in[i], cmp_pe_k[i], cmp_w1_k[i], cmp_w2_k[i],
                             cmp_pe_v[i], cmp_w1_v[i], cmp_w2_v[i])
            w_out = od_w_out[i]
        x = _out_proj(mix, w_out.astype(BF16), x, g1)
        x = _conv_ffn(x, norm2_g[layer], sc2, sh2, g2, ffn_w_up[layer].astype(BF16), ffn_conv_w[layer],
                      ffn_conv_b[layer], ffn_w_down[layer].astype(BF16), final_norm_g,
                      final_norm=(layer == depth - 1))
    return x
```

```python
import functools

import numpy as np
import jax
import jax.numpy as jnp
from jax import lax
from jax.experimental import pallas as pl
from jax.experimental.pallas import tpu as pltpu

F32 = jnp.float32
BF16 = jnp.bfloat16

D_MODEL = 1024
NORM_EPS = 1e-6
GLA_HEADS, GLA_DK, GLA_DV, GLA_LOWRANK, GLA_GATE_NORM, GLA_CHUNK = 4, 64, 128, 16, 16.0, 16
RWKV_HEADS, RWKV_N, RWKV_GN_EPS = 8, 64, 64e-5
RWKV_W_LORA, RWKV_A_LORA, RWKV_G_LORA = 64, 64, 128
RWKV_CHUNK = 64
NSA_HEADS, NSA_GROUPS, NSA_DH = 16, 4, 64
NSA_REP = NSA_HEADS // NSA_GROUPS
NSA_CMP_LEN, NSA_CMP_STRIDE, NSA_CMP_HIDDEN = 32, 16, 64
NSA_SEL_LEN, NSA_N_SEL, NSA_N_LOCAL, NSA_WINDOW, NSA_Q_BLOCK, NSA_FORCE = 64, 8, 2, 512, 128, 100.0
NSA_SWEEP_TILES = 4
FFN_HIDDEN = 2816

LANES = 128
VMEM_LIMIT = 56 * 1024 * 1024
NEG_BIG = -1e30


def _cparams(sem):
    return pltpu.CompilerParams(dimension_semantics=sem, vmem_limit_bytes=VMEM_LIMIT)


def _dot(a, b):
    return jnp.dot(a, b, preferred_element_type=F32)


def _dot_nt(a, b):
    return lax.dot_general(a, b, (((1,), (1,)), ((), ())), preferred_element_type=F32)


def _mm(a, b):
    return _dot(a.astype(BF16), b.astype(BF16))


def _mm_nt(a, b):
    return _dot_nt(a.astype(BF16), b.astype(BF16))


def _split3(a):
    a1 = a.astype(BF16)
    r1 = a - a1.astype(F32)
    a2 = r1.astype(BF16)
    a3 = (r1 - a2.astype(F32)).astype(BF16)
    return a1, a2, a3


def _mm_sel(a, b01, terms=3):
    b = b01.astype(BF16)
    out = None
    for part in _split3(a)[:terms]:
        out = _dot(part, b) if out is None else out + _dot(part, b)
    return out


def _sel_mm(a01, b):
    b1, b2, b3 = _split3(b)
    a = a01.astype(BF16)
    return _dot(a, b1) + _dot(a, b2) + _dot(a, b3)


def _mm3(a, b):
    a1 = a.astype(BF16)
    a2 = (a - a1.astype(F32)).astype(BF16)
    b1 = b.astype(BF16)
    b2 = (b - b1.astype(F32)).astype(BF16)
    return _dot(a1, b1) + _dot(a1, b2) + _dot(a2, b1)


def _mm3_nt(a, b):
    a1 = a.astype(BF16)
    a2 = (a - a1.astype(F32)).astype(BF16)
    b1 = b.astype(BF16)
    b2 = (b - b1.astype(F32)).astype(BF16)
    return _dot_nt(a1, b1) + _dot_nt(a1, b2) + _dot_nt(a2, b1)


def _iota(shape, dim):
    return lax.broadcasted_iota(jnp.int32, shape, dim)


def _sigmoid(x):
    return 1.0 / (1.0 + jnp.exp(-x))


def _softplus(x):
    return jnp.maximum(x, 0.0) + jnp.log(1.0 + jnp.exp(-jnp.abs(x)))


def _gelu_tanh(x):
    return x * (0.5 * (1.0 + jnp.tanh(0.7978845608028654 * (x + 0.044715 * (x * x * x)))))


def _rms(x, eps):
    return x * lax.rsqrt(jnp.mean(x * x, axis=-1, keepdims=True) + eps)


def _mod_kernel(c_ref, w_ref, b_ref, o_ref):
    c = c_ref[...]
    cond = c * _sigmoid(c)
    o_ref[0] = _mm3(cond, w_ref[0]) + b_ref[0]


def _ada_mod(c, ada_w, ada_b):
    depth, d, n = ada_w.shape
    bsz = c.shape[0]
    rows = 8
    c8 = jnp.zeros((rows, d), F32).at[:bsz].set(c)
    tn = 1536
    out = pl.pallas_call(
        _mod_kernel,
        grid=(depth, n // tn),
        in_specs=[
            pl.BlockSpec((rows, d), lambda l, j: (0, 0)),
            pl.BlockSpec((1, d, tn), lambda l, j: (l, 0, j)),
            pl.BlockSpec((1, 1, tn), lambda l, j: (l, 0, j)),
        ],
        out_specs=pl.BlockSpec((1, rows, tn), lambda l, j: (l, 0, j)),
        out_shape=jax.ShapeDtypeStruct((depth, rows, n), F32),
        compiler_params=_cparams(("parallel", "parallel")),
        name="ada_mod",
    )(c8, ada_w, ada_b.reshape(depth, 1, n))
    return out[:, :bsz]


def _norm_proj_kernel(x_ref, g_ref, sc_ref, sh_ref, w_ref, o_ref, *copy_ref, copy_cols):
    hn = _rms(x_ref[0], NORM_EPS) * g_ref[...]
    hn = (hn * (1.0 + sc_ref[0]) + sh_ref[0]).astype(BF16)
    out = _dot(hn, w_ref[...])
    o_ref[0] = out
    if copy_cols:
        start, width = copy_cols
        copy_ref[0][0] = out[:, start:start + width].astype(BF16)


def _norm_proj(x, g, sc, sh, w, *, tm=512, copy_cols=None):
    bsz, t, d = x.shape
    n = w.shape[1]
    out_specs = [pl.BlockSpec((1, tm, n), lambda b, i: (b, i, 0))]
    out_shape = [jax.ShapeDtypeStruct((bsz, t, n), F32)]
    if copy_cols:
        out_specs.append(pl.BlockSpec((1, tm, copy_cols[1]), lambda b, i: (b, i, 0)))
        out_shape.append(jax.ShapeDtypeStruct((bsz, t, copy_cols[1]), BF16))
    outs = pl.pallas_call(
        functools.partial(_norm_proj_kernel, copy_cols=copy_cols),
        grid=(bsz, t // tm),
        in_specs=[
            pl.BlockSpec((1, tm, d), lambda b, i: (b, i, 0)),
            pl.BlockSpec((1, d), lambda b, i: (0, 0)),
            pl.BlockSpec((1, 1, d), lambda b, i: (b, 0, 0)),
            pl.BlockSpec((1, 1, d), lambda b, i: (b, 0, 0)),
            pl.BlockSpec((d, n), lambda b, i: (0, 0)),
        ],
        out_specs=out_specs,
        out_shape=out_shape,
        compiler_params=_cparams(("parallel", "parallel")),
        name="norm_proj",
    )(x, g.reshape(1, d), sc, sh, w)
    return outs if copy_cols else outs[0]


def _out_proj_kernel(*refs):
    *mw_refs, x_ref, gate_ref, o_ref = refs
    n = len(mw_refs) // 2
    proj = _mm(mw_refs[0][0], mw_refs[n][...])
    for m_ref, w_ref in zip(mw_refs[1:n], mw_refs[n + 1:]):
        proj = proj + _mm(m_ref[0], w_ref[...])
    o_ref[0] = x_ref[0] + gate_ref[0] * proj


def _out_proj(mixes, w, x, gate, *, tm=512):
    bsz, t, d = x.shape
    widths = [m.shape[-1] for m in mixes]
    offs = np.cumsum([0] + widths).tolist()
    ws = [w[o:o + k] for o, k in zip(offs, widths)]
    return pl.pallas_call(
        _out_proj_kernel,
        grid=(bsz, t // tm),
        in_specs=[pl.BlockSpec((1, tm, k), lambda b, i: (b, i, 0)) for k in widths]
        + [pl.BlockSpec((k, d), lambda b, i: (0, 0)) for k in widths]
        + [pl.BlockSpec((1, tm, d), lambda b, i: (b, i, 0)), pl.BlockSpec((1, 1, d), lambda b, i: (b, 0, 0))],
        out_specs=pl.BlockSpec((1, tm, d), lambda b, i: (b, i, 0)),
        out_shape=jax.ShapeDtypeStruct((bsz, t, d), F32),
        compiler_params=_cparams(("parallel", "parallel")),
        name="out_proj",
    )(*mixes, *ws, x, gate)


def _ffn_kernel(x_ref, g_ref, sc_ref, sh_ref, gate_ref, wu_ref, wv_ref, cw_ref, cb_ref, wd_ref, fg_ref,
                o_ref, hn_ref, acc_ref, halo_ref, *, final_norm):
    ti = pl.program_id(1)
    fj = pl.program_id(2)
    tm, fk = acc_ref.shape[0], wu_ref.shape[1]

    @pl.when(fj == 0)
    def _():
        hn = _rms(x_ref[0], NORM_EPS) * g_ref[...]
        hn_ref[...] = (hn * (1.0 + sc_ref[0]) + sh_ref[0]).astype(BF16)
        acc_ref[...] = jnp.zeros_like(acc_ref)

    @pl.when(ti == 0)
    def _():
        halo_ref[fj] = jnp.zeros((8, fk), F32)

    hn = hn_ref[...]
    u = _dot(hn, wu_ref[...])
    v = _dot(hn, wv_ref[...])
    prev = halo_ref[fj]
    row = _iota((tm, fk), 0)
    u1 = jnp.where(row == 0, prev[7:8], pltpu.roll(u, 1, 0))
    u2 = jnp.where(row == 0, prev[6:7], jnp.where(row == 1, prev[7:8], pltpu.roll(u, 2, 0)))
    halo_ref[fj] = u[tm - 8:tm]
    cw = cw_ref[...]
    uc = cw[0:1] * u2 + cw[1:2] * u1 + cw[2:3] * u + cb_ref[...]
    h = _gelu_tanh(uc) * v
    acc_ref[...] += _dot(h.astype(BF16), wd_ref[...])

    @pl.when(fj == pl.num_programs(2) - 1)
    def _():
        y = x_ref[0] + gate_ref[0] * acc_ref[...]
        if final_norm:
            y = _rms(y, NORM_EPS) * fg_ref[...]
        o_ref[0] = y


def _conv_ffn(x, g, sc, sh, gate, w_up, conv_w, conv_b, w_down, final_g, *, final_norm, tm=512, fk=2816):
    bsz, t, d = x.shape
    f = w_down.shape[0]
    nf = f // fk
    kern = functools.partial(_ffn_kernel, final_norm=final_norm)
    return pl.pallas_call(
        kern,
        grid=(bsz, t // tm, nf),
        in_specs=[
            pl.BlockSpec((1, tm, d), lambda b, i, j: (b, i, 0)),
            pl.BlockSpec((1, d), lambda b, i, j: (0, 0)),
            pl.BlockSpec((1, 1, d), lambda b, i, j: (b, 0, 0)),
            pl.BlockSpec((1, 1, d), lambda b, i, j: (b, 0, 0)),
            pl.BlockSpec((1, 1, d), lambda b, i, j: (b, 0, 0)),
            pl.BlockSpec((d, fk), lambda b, i, j: (0, j)),
            pl.BlockSpec((d, fk), lambda b, i, j: (0, j + nf)),
            pl.BlockSpec((3, fk), lambda b, i, j: (0, j)),
            pl.BlockSpec((1, fk), lambda b, i, j: (0, j)),
            pl.BlockSpec((fk, d), lambda b, i, j: (j, 0)),
            pl.BlockSpec((1, d), lambda b, i, j: (0, 0)),
        ],
        out_specs=pl.BlockSpec((1, tm, d), lambda b, i, j: (b, i, 0)),
        out_shape=jax.ShapeDtypeStruct((bsz, t, d), F32),
        scratch_shapes=[pltpu.VMEM((tm, d), BF16), pltpu.VMEM((tm, d), F32), pltpu.VMEM((nf, 8, fk), F32)],
        compiler_params=_cparams(("parallel", "arbitrary", "arbitrary")),
        name="conv_ffn",
    )(x, g.reshape(1, d), sc, sh, gate, w_up, w_up, conv_w, conv_b.reshape(1, f), w_down, final_g.reshape(1, d))


def _gla_kernel(q_ref, k_ref, v_ref, og_ref, lr_ref, aup_ref, ab_ref, gg_ref, o_ref,
                st_ref, b_scr, bend_scr, q_scr, k_scr, kd_scr, qg_scr, vt_scr, o_scr):
    tb = q_ref.shape[1]
    cs = GLA_CHUNK
    n_chunks = tb // cs

    @pl.when(pl.program_id(1) == 0)
    def _():
        st_ref[...] = jnp.zeros_like(st_ref)

    z = _mm3(lr_ref[0], aup_ref[...]) + ab_ref[...]
    la = -_softplus(-z) * (1.0 / GLA_GATE_NORM)
    rr, cc = _iota((tb, tb), 0), _iota((tb, tb), 1)
    same = (rr // cs) == (cc // cs)
    b = _sel_mm(jnp.where(same & (cc <= rr), 1.0, 0.0), la)
    bend = _sel_mm(jnp.where(same, 1.0, 0.0), la)
    q = q_ref[0] * (GLA_DK ** -0.5)
    k = k_ref[0]
    b_scr[...] = b
    bend_scr[...] = bend
    q_scr[...] = q
    k_scr[...] = k
    kd_scr[...] = (k * jnp.exp(bend - b)).astype(BF16)
    qg_scr[...] = q * jnp.exp(b)
    v_all = v_ref[0]
    for h in range(GLA_HEADS):
        vt_scr[h] = v_all[:, h * GLA_DV:(h + 1) * GLA_DV].T.astype(BF16)

    lane_c = _iota((cs, LANES), 1)
    row_c = _iota((cs, LANES), 0)
    lane_t = _iota((tb, LANES), 1)
    head_rows = _iota((LANES, LANES), 0) // GLA_DK

    def chunk(c, carry):
        r0 = c * cs
        tmask = (lane_t // cs) == c
        for p in range(GLA_HEADS // 2):
            lanes = slice(p * LANES, (p + 1) * LANES)
            bc = b_scr[pl.ds(r0, cs), lanes]
            qc = q_scr[pl.ds(r0, cs), lanes]
            kc = k_scr[pl.ds(r0, cs), lanes]
            blocks = []
            for s in range(cs):
                m = row_c >= s
                rel = jnp.where(m, bc - bc[s:s + 1], 0.0)
                blocks.append(jnp.where(m, qc * kc[s:s + 1] * jnp.exp(rel), 0.0))
            a_st = jnp.concatenate(blocks, axis=0)
            qgc = qg_scr[pl.ds(r0, cs), lanes]
            lhs = jnp.concatenate([jnp.where(lane_c < GLA_DK, qgc, 0.0),
                                   jnp.where(lane_c >= GLA_DK, qgc, 0.0)], axis=0)
            st = st_ref[p]
            o_inter = _mm_nt(lhs, st)
            for hh in range(2):
                h = 2 * p + hh
                zsum = _mm_sel(a_st, jnp.where(head_rows == hh, 1.0, 0.0), terms=1)
                vc = v_ref[0, pl.ds(r0, cs), h * GLA_DV:(h + 1) * GLA_DV]
                o_h = o_inter[hh * cs:(hh + 1) * cs]
                for s in range(cs):
                    o_h = o_h + zsum[s * cs:(s + 1) * cs] * vc[s:s + 1]
                o_scr[pl.ds(r0, cs), h * GLA_DV:(h + 1) * GLA_DV] = o_h
            lhs_u = jnp.concatenate([jnp.where(tmask, vt_scr[2 * p], 0.0).astype(BF16),
                                     jnp.where(tmask, vt_scr[2 * p + 1], 0.0).astype(BF16)], axis=0)
            upd = _dot(lhs_u, kd_scr[:, lanes])
            upd = jnp.where(lane_t < GLA_DK, upd[:GLA_DV], upd[GLA_DV:])
            decay = jnp.exp(bend_scr[pl.ds(r0, 1), lanes])
            st_ref[p] = st * decay + upd
        return carry

    for c in range(n_chunks):
        chunk(c, 0)

    og = og_ref[0]
    for h in range(GLA_HEADS):
        sl = slice(h * GLA_DV, (h + 1) * GLA_DV)
        gate = og[:, sl]
        o_ref[0, :, sl] = (_rms(o_scr[:, sl], NORM_EPS) * gg_ref[...] * (gate * _sigmoid(gate))).astype(o_ref.dtype)


def _gla(p, a_up_pad, a_b, gla_g, *, tb=128):
    bsz, t, _ = p.shape
    qk = GLA_HEADS * GLA_DK
    vw = GLA_HEADS * GLA_DV
    assert tb == LANES and GLA_DV == LANES
    return pl.pallas_call(
        _gla_kernel,
        grid=(bsz, t // tb),
        in_specs=[
            pl.BlockSpec((1, tb, qk), lambda b, i: (b, i, 0)),
            pl.BlockSpec((1, tb, qk), lambda b, i: (b, i, 1)),
            pl.BlockSpec((1, tb, vw), lambda b, i: (b, i, 1)),
            pl.BlockSpec((1, tb, vw), lambda b, i: (b, i, 2)),
            pl.BlockSpec((1, tb, LANES), lambda b, i: (b, i, 24)),
            pl.BlockSpec((LANES, qk), lambda b, i: (0, 0)),
            pl.BlockSpec((1, qk), lambda b, i: (0, 0)),
            pl.BlockSpec((1, GLA_DV), lambda b, i: (0, 0)),
        ],
        out_specs=pl.BlockSpec((1, tb, vw), lambda b, i: (b, i, 0)),
        out_shape=jax.ShapeDtypeStruct((bsz, t, vw), BF16),
        scratch_shapes=[
            pltpu.VMEM((GLA_HEADS // 2, GLA_DV, LANES), F32),
            pltpu.VMEM((tb, qk), F32), pltpu.VMEM((tb, qk), F32),
            pltpu.VMEM((tb, qk), F32), pltpu.VMEM((tb, qk), F32),
            pltpu.VMEM((tb, qk), BF16), pltpu.VMEM((tb, qk), F32),
            pltpu.VMEM((GLA_HEADS, GLA_DV, tb), BF16),
            pltpu.VMEM((tb, vw), F32),
        ],
        compiler_params=_cparams(("parallel", "arbitrary")),
        name="gla",
    )(p, p, p, p, p, a_up_pad, a_b.reshape(1, qk), gla_g.reshape(1, GLA_DV))


def _rwkv_kernel(r_ref, k_ref, v_ref, wa_ref, gl_ref, mur_ref, muk_ref, muv_ref, muwa_ref, mugl_ref,
                 w0_ref, w2_ref, a0_ref, a2_ref, g2_ref, kkw_ref, ka_ref, rk_ref, gnw_ref, gnb_ref,
                 o_ref,
                 lr_scr, lk_scr, lv_scr, lwa_scr, lgl_scr, s_ref,
                 lw_p, r_p, kk_p, be_p, k2_p, v_p, y_p, pp_all, y0_all, ge_all, g_all, h_all):
    tb = r_ref.shape[1]
    cs = RWKV_CHUNK
    n_chunks = tb // cs
    n_pairs = RWKV_HEADS // 2
    ti = pl.program_id(1)
    lasts = (lr_scr, lk_scr, lv_scr, lwa_scr, lgl_scr)

    @pl.when(ti == 0)
    def _():
        s_ref[...] = jnp.zeros_like(s_ref)
        for ref in lasts:
            ref[...] = jnp.zeros_like(ref)

    def shifted(x_ref, last_ref, mu_ref):
        x = x_ref[0]
        row = _iota(x.shape, 0)
        prev = jnp.where(row == 0, last_ref[7:8], pltpu.roll(x, 1, 0))
        last_ref[...] = x[tb - 8:tb]
        return x + (prev - x) * mu_ref[...]

    r = shifted(r_ref, lr_scr, mur_ref)
    k = shifted(k_ref, lk_scr, muk_ref)
    v = shifted(v_ref, lv_scr, muv_ref)
    wa = shifted(wa_ref, lwa_scr, muwa_ref)
    gl = shifted(gl_ref, lgl_scr, mugl_ref)

    logw = -jnp.exp(-_softplus(-(w0_ref[...] + _mm3(jnp.tanh(wa), w2_ref[...]))) - 0.5)
    a = _sigmoid(a0_ref[...] + _mm3(wa, a2_ref[...]))
    g = _mm(_sigmoid(gl), g2_ref[...])
    seg = jnp.where((_iota((LANES, LANES), 0) // RWKV_N) == (_iota((LANES, LANES), 1) // RWKV_N), 1.0, 0.0)

    def segsum(x):
        return jnp.concatenate([_mm_sel(x[:, i * LANES:(i + 1) * LANES], seg, terms=2) for i in range(n_pairs)], axis=1)

    kk = k * kkw_ref[...]
    kk = kk * lax.rsqrt(jnp.maximum(segsum(kk * kk), 1e-24))
    k2 = k * (1.0 + (a - 1.0) * ka_ref[...])
    beta = kk * a
    for p in range(n_pairs):
        sl = slice(p * LANES, (p + 1) * LANES)
        lw_p[p] = logw[:, sl]
        r_p[p] = r[:, sl]
        kk_p[p] = kk[:, sl]
        be_p[p] = beta[:, sl]
        k2_p[p] = k2[:, sl]
        v_p[p] = v[:, sl]

    rr, cc = _iota((tb, tb), 0), _iota((tb, tb), 1)
    same = (rr // cs) == (cc // cs)
    tri_incl = same & (cc <= rr)
    tri_strict = same & (cc < rr)
    l_incl = jnp.where(tri_incl, 1.0, 0.0)
    l_all = jnp.where(same, 1.0, 0.0)
    eye = jnp.where(rr == cc, 1.0, 0.0)
    lane = _iota((tb, LANES), 1)
    half = lane < RWKV_N
    same_half = (rr // RWKV_N) == (cc // RWKV_N)
    lane_c = _iota((cs, LANES), 1)
    lane_s = _iota((RWKV_N, LANES), 1)

    def below_left(s):
        return ((rr // (2 * s)) == (cc // (2 * s))) & ((rr % (2 * s)) >= s) & ((cc % (2 * s)) < s)

    pairs = range(n_pairs)
    heads = [(p, hh) for p in pairs for hh in range(2)]
    l_both = jnp.concatenate([l_incl, l_all], axis=0)
    rt, bt, at, vp, ak = {}, {}, {}, {}, {}
    for p in pairs:
        lw = lw_p[p]
        sums = _sel_mm(l_both, lw)
        cum = sums[:tb]
        ge_all[p] = jnp.exp(sums[tb:])
        ig = jnp.exp(-cum)
        rt[p] = r_p[p] * jnp.exp(cum)
        bt[p] = kk_p[p] * jnp.exp(cum - lw)
        at[p] = -(be_p[p] * ig)
        vp[p] = v_p[p]
        ak[p] = jnp.concatenate([at[p], k2_p[p] * ig], axis=0)
    a_ab, a_ak, a_ra, a_rk = {}, {}, {}, {}
    for h in heads:
        p, hh = h
        hm = half if hh == 0 else jnp.logical_not(half)
        s_b = _mm3_nt(jnp.where(hm, bt[p], 0.0), ak[p])
        s_r = _mm_nt(jnp.where(hm, rt[p], 0.0), ak[p])
        a_ab[h] = jnp.where(tri_strict, s_b[:, :tb], 0.0)
        a_ak[h] = jnp.where(tri_strict, s_b[:, tb:], 0.0)
        a_ra[h] = jnp.where(tri_incl, s_r[:, :tb], 0.0)
        a_rk[h] = jnp.where(tri_incl, s_r[:, tb:], 0.0)
    m_inv = {h: eye + jnp.where(below_left(1), a_ab[h], 0.0) for h in heads}
    w1 = {h: _mm3(a_ak[h], vp[h[0]]) for h in heads}
    s = 2
    while s < cs:
        low = {h: _mm3(m_inv[h], jnp.where(below_left(s), a_ab[h], 0.0)) for h in heads}
        m_inv = {h: m_inv[h] + _mm3(low[h], m_inv[h]) for h in heads}
        s *= 2
    mw = {h: _mm3(m_inv[h], jnp.concatenate([w1[h], bt[h[0]]], axis=1)) for h in heads}
    p_h = {h: _mm(a_ra[h], mw[h][:, LANES:]) for h in heads}
    y0_h = {h: _mm(a_ra[h], mw[h][:, :LANES]) + _mm(a_rk[h], vp[h[0]]) for h in heads}
    zeros = jnp.zeros((tb, LANES), F32)
    for p in pairs:
        u0 = jnp.where(half, mw[(p, 0)][:, :LANES], mw[(p, 1)][:, :LANES])
        mb = jnp.where(half, mw[(p, 0)][:, LANES:], mw[(p, 1)][:, LANES:])
        pp_all[p] = rt[p] + jnp.where(half, p_h[(p, 0)], p_h[(p, 1)])
        y0_all[p] = jnp.where(half, y0_h[(p, 0)], y0_h[(p, 1)])
        u0t, vt, mbt = u0.T, vp[p].T, mb.T
        for c in range(n_chunks):
            tmask = (lane // cs) == c
            lhs = jnp.concatenate([
                jnp.concatenate([jnp.where(tmask, u0t, 0.0), jnp.where(tmask, vt, 0.0)], axis=1),
                jnp.concatenate([jnp.where(tmask, mbt, 0.0), zeros], axis=1)], axis=0)
            hg = _mm3(lhs, ak[p])
            h_all[p * n_chunks + c] = jnp.where(lane_s < RWKV_N, hg[:RWKV_N], hg[RWKV_N:tb])
            g_all[p * n_chunks + c] = jnp.where(same_half, hg[tb:], 0.0) + eye

    state = [s_ref[p] for p in range(n_pairs)]
    for c in range(n_chunks):
        rows = slice(c * cs, (c + 1) * cs)
        for p in range(n_pairs):
            sp = state[p]
            pc = pp_all[p, rows, :]
            lhs_y = jnp.concatenate([jnp.where(lane_c < RWKV_N, pc, 0.0), jnp.where(lane_c >= RWKV_N, pc, 0.0)], axis=0)
            yy = _mm_nt(lhs_y, jnp.concatenate([sp, sp], axis=0))
            y_p[p, rows, :] = jnp.where(lane_c < RWKV_N, yy[:cs], yy[cs:]) + y0_all[p, rows, :]
            state[p] = (_mm3(sp, g_all[p * n_chunks + c]) + h_all[p * n_chunks + c]) * ge_all[p, c * cs:c * cs + 1, :]
    for p in range(n_pairs):
        s_ref[p] = state[p]

    y = jnp.concatenate([y_p[p] for p in range(n_pairs)], axis=1)
    mu = segsum(y) * (1.0 / RWKV_N)
    yc = y - mu
    var = segsum(yc * yc) * (1.0 / RWKV_N)
    yn = yc * lax.rsqrt(var + RWKV_GN_EPS) * gnw_ref[...] + gnb_ref[...]
    bonus = segsum(r * k2 * rk_ref[...]) * v
    o_ref[0] = ((yn + bonus) * g).astype(o_ref.dtype)


def _rwkv(p, mus, w0, w2p, a0, a2p, g2, k_k, k_a, r_k, gn_w, gn_b, *, tb=128):
    bsz, t, _ = p.shape
    w = RWKV_HEADS * RWKV_N
    n_pairs = RWKV_HEADS // 2
    assert tb == LANES
    row = lambda a: a.reshape(1, -1)
    full = lambda shape: pl.BlockSpec(shape, lambda b, i: (0,) * len(shape))
    ptile = lambda: pltpu.VMEM((n_pairs, tb, LANES), F32)
    return pl.pallas_call(
        _rwkv_kernel,
        grid=(bsz, t // tb),
        in_specs=[
            pl.BlockSpec((1, tb, w), lambda b, i: (b, i, 3)),
            pl.BlockSpec((1, tb, w), lambda b, i: (b, i, 4)),
            pl.BlockSpec((1, tb, w), lambda b, i: (b, i, 5)),
            pl.BlockSpec((1, tb, LANES), lambda b, i: (b, i, 25)),
            pl.BlockSpec((1, tb, LANES), lambda b, i: (b, i, 26)),
            full((1, w)), full((1, w)), full((1, w)), full((1, LANES)), full((1, LANES)),
            full((1, w)), full((LANES, w)), full((1, w)), full((LANES, w)), full((LANES, w)),
            full((1, w)), full((1, w)), full((1, w)), full((1, w)), full((1, w)),
        ],
        out_specs=pl.BlockSpec((1, tb, w), lambda b, i: (b, i, 0)),
        out_shape=jax.ShapeDtypeStruct((bsz, t, w), BF16),
        scratch_shapes=[
            pltpu.VMEM((8, w), F32), pltpu.VMEM((8, w), F32), pltpu.VMEM((8, w), F32),
            pltpu.VMEM((8, LANES), F32), pltpu.VMEM((8, LANES), F32),
            pltpu.VMEM((n_pairs, RWKV_N, LANES), F32),
            ptile(), ptile(), ptile(), ptile(), ptile(), ptile(), ptile(), ptile(), ptile(), ptile(),
            pltpu.VMEM((n_pairs * (tb // RWKV_CHUNK), LANES, LANES), F32),
            pltpu.VMEM((n_pairs * (tb // RWKV_CHUNK), RWKV_N, LANES), F32),
        ],
        compiler_params=_cparams(("parallel", "arbitrary")),
        name="rwkv7",
    )(p, p, p, p, p, *[row(m) for m in mus], row(w0), w2p, row(a0), a2p, g2,
      row(k_k), row(k_a), row(r_k), row(gn_w), row(gn_b))


def _cmp_kernel(sk_ref, sv_ref, wak_ref, wbk_ref, wav_ref, wbv_ref, pek_ref, pev_ref, w1k_ref, w1v_ref,
                w2k_ref, w2v_ref, ok_ref, ov_ref):
    def one(seg_ref, wa_ref, wb_ref, pe_ref, w1_ref, w2_ref, o_ref):
        seg = seg_ref[0].astype(BF16)
        first = _dot(seg, wa_ref[...])
        second = _dot(seg, wb_ref[...])
        n = first.shape[0]
        pe_term = _mm3(pe_ref[...], w1_ref[...])[0:1]
        hidden = _gelu_tanh(first + pltpu.roll(second, n - 1, 0) + pe_term)
        o_ref[0] = _mm(hidden, w2_ref[...])

    one(sk_ref, wak_ref, wbk_ref, pek_ref, w1k_ref, w2k_ref, ok_ref)
    one(sv_ref, wav_ref, wbv_ref, pev_ref, w1v_ref, w2v_ref, ov_ref)


def _nsa_compress(kc_tok, vc_tok, pe_k, w1_k, w2_k, pe_v, w1_v, w2_v):
    bsz, t, gw = kc_tok.shape
    st, dh, hid, g = NSA_CMP_STRIDE, NSA_DH, NSA_CMP_HIDDEN, NSA_GROUPS
    nseg = t // st
    eye = jnp.eye(g, dtype=F32)

    def expand_w1(w1):
        w = w1.reshape(NSA_CMP_LEN, dh, hid)
        big = jnp.einsum('ldc,gh->lgdhc', w, eye).reshape(NSA_CMP_LEN * g * dh, g * hid)
        half = st * g * dh
        return big[:half].astype(BF16), big[half:].astype(BF16)

    def expand_w2(w2):
        return jnp.einsum('cd,gh->gchd', w2, eye).reshape(g * hid, g * dh).astype(BF16)

    def pe_rows(pe):
        return jnp.zeros((8, NSA_CMP_LEN * dh), F32).at[0].set(pe.reshape(-1))

    wak, wbk = expand_w1(w1_k)
    wav, wbv = expand_w1(w1_v)
    full = lambda shape: pl.BlockSpec(shape, lambda b: (0,) * len(shape))
    seg_spec = pl.BlockSpec((1, nseg, st * gw), lambda b: (b, 0, 0))
    out_spec = pl.BlockSpec((1, nseg, gw), lambda b: (b, 0, 0))
    return pl.pallas_call(
        _cmp_kernel,
        grid=(bsz,),
        in_specs=[seg_spec, seg_spec,
                  full(wak.shape), full(wbk.shape), full(wav.shape), full(wbv.shape),
                  full((8, NSA_CMP_LEN * dh)), full((8, NSA_CMP_LEN * dh)),
                  full((NSA_CMP_LEN * dh, g * hid)), full((NSA_CMP_LEN * dh, g * hid)),
                  full((g * hid, gw)), full((g * hid, gw))],
        out_specs=[out_spec, out_spec],
        out_shape=[jax.ShapeDtypeStruct((bsz, nseg, gw), F32)] * 2,
        compiler_params=_cparams(("parallel",)),
        name="nsa_compress",
    )(kc_tok.reshape(bsz, nseg, st * gw), vc_tok.reshape(bsz, nseg, st * gw),
      wak, wbk, wav, wbv, pe_rows(pe_k), pe_rows(pe_v),
      jnp.tile(w1_k, (1, g)), jnp.tile(w1_v, (1, g)), expand_w2(w2_k), expand_w2(w2_v))


def _nsa_attn_kernel(q_ref, kc_ref, vc_ref, ks_ref, vs_ref, kw_ref, vw_ref, gt_ref, sl_ref, o_ref, acc_scr):
    g = pl.program_id(1)
    qi = pl.program_id(2)
    qb, dh, rep = NSA_Q_BLOCK, NSA_DH, NSA_REP
    n_cmp_pad = kc_ref.shape[1]
    kt = LANES
    rows_all = rep * qb

    lane_q = _iota((1, LANES), 1)
    lane_b = _iota((qb, LANES), 1)
    q_rows = []
    for r in range(rep):
        s1, s2, s3 = (t.astype(F32) for t in _split3(sl_ref[0, r:r + 1, :]))
        slope_cols = jnp.where((lane_q >= dh) & (lane_q < dh + 2), s1,
                               jnp.where((lane_q >= dh + 2) & (lane_q < dh + 4), s2,
                                         jnp.where((lane_q >= dh + 4) & (lane_q < dh + 6), s3, 0.0)))
        two_heads = q_ref[0, :, (r // 2) * LANES:(r // 2 + 1) * LANES]
        if r % 2:
            two_heads = pltpu.roll(two_heads, dh, 1)
        q_rows.append(jnp.where(lane_b < dh, two_heads * (dh ** -0.5), slope_cols))
    q = jnp.concatenate(q_rows, axis=0).astype(BF16)
    t0 = qi * qb

    row_l = _iota((rows_all, kt), 0) % qb
    lane_k = _iota((rows_all, kt), 1)
    causal = row_l >= lane_k
    win_lo = row_l < lane_k

    def lane_tiles(s):
        return [s[:, j * kt:(j + 1) * kt] for j in range(s.shape[1] // kt)]

    def tile_max(tiles, start):
        m = start
        for tile in tiles:
            m = jnp.maximum(m, tile)
        return m

    def normalised(acc):
        return acc / acc[:, dh:dh + 1]

    gsel = jnp.where(_iota((LANES, LANES), 0) == g * (rep * 3) + _iota((LANES, LANES), 1), 1.0, 0.0)
    gates = _sigmoid(_mm_sel(gt_ref[0], gsel))

    n_idx = _iota((rows_all, n_cmp_pad), 1)
    row_c = _iota((rows_all, n_cmp_pad), 0) % qb
    valid_c = (t0 + row_c >= n_idx * NSA_CMP_STRIDE + NSA_CMP_LEN - 1) & (n_idx < n_cmp_pad - 1)
    s_cmp = jnp.where(valid_c, _dot_nt(q, kc_ref[0]), NEG_BIG)
    e_cmp = jnp.where(valid_c, jnp.exp(s_cmp - jnp.max(s_cmp, axis=-1, keepdims=True)), 0.0)
    den = jnp.sum(e_cmp, axis=-1, keepdims=True)
    p_cmp = e_cmp / jnp.where(den > 0.0, den, 1.0)
    o_cmp_all = _mm(p_cmp, vc_ref[0])
    o_cmp = [o_cmp_all[r * qb:(r + 1) * qb] for r in range(rep)]
    n_sel_blocks = ks_ref.shape[1] // NSA_SEL_LEN
    on, oj = _iota((n_cmp_pad, LANES), 0), _iota((n_cmp_pad, LANES), 1)
    overlap = jnp.where((on * NSA_CMP_STRIDE <= oj * NSA_SEL_LEN + NSA_SEL_LEN - 1)
                        & (on * NSA_CMP_STRIDE + NSA_CMP_LEN - 1 >= oj * NSA_SEL_LEN)
                        & (oj < n_sel_blocks) & (on < n_cmp_pad - 1), 1.0, 0.0)
    p_group = p_cmp[0:qb]
    for r in range(1, rep):
        p_group = p_group + p_cmp[r * qb:(r + 1) * qb]
    imp = _mm_sel(p_group, overlap)

    imp_t = imp.T[:n_sel_blocks]
    jj = _iota((n_sel_blocks, qb), 0)
    jf = jj.astype(F32)
    ahead = (t0 + _iota((n_sel_blocks, qb), 1)) // NSA_SEL_LEN - jj
    valid_b = ahead >= 0
    forced = (jj == 0) | (valid_b & (ahead < NSA_N_LOCAL))
    score = jnp.where(valid_b, imp_t + jnp.where(forced, NSA_FORCE, 0.0), -NSA_FORCE)
    sel = jnp.zeros((n_sel_blocks, qb), F32)
    for _ in range(NSA_N_SEL):
        best = jnp.max(score, axis=0, keepdims=True)
        first = jnp.min(jnp.where(score == best, jf, float(n_sel_blocks)), axis=0, keepdims=True)
        pick = jf == first
        sel = jnp.where(pick, 1.0, sel)
        score = jnp.where(pick, NEG_BIG, score)
    sel_q = jnp.concatenate([sel, jnp.zeros((LANES - n_sel_blocks, qb), F32)], axis=0).T.astype(BF16)

    n_win = NSA_WINDOW // kt + 1
    win_s, win_v = [], []
    for w in range(n_win):
        kb = qi - (n_win - 1) + w
        k0 = pl.multiple_of(jnp.maximum(kb, 0) * kt, kt)
        s = _dot_nt(q, kw_ref[0, pl.ds(k0, kt), :])
        if w == 0:
            s = jnp.where(win_lo, s, NEG_BIG)
        if w == n_win - 1:
            s = jnp.where(causal, s, NEG_BIG)
        else:
            s = s + jnp.where(kb >= 0, 0.0, NEG_BIG)
        win_s.append(s)
        win_v.append(vw_ref[0, pl.ds(k0, kt), :])
    m_win = jnp.broadcast_to(jnp.max(tile_max(win_s[1:], win_s[0]), axis=-1, keepdims=True), (rows_all, kt))
    e_win = jnp.concatenate([jnp.exp(s - m_win).astype(BF16) for s in win_s], axis=1)
    o_win = normalised(_dot(e_win, jnp.concatenate(win_v, axis=0)))

    gate_of = lambda r, branch: jnp.broadcast_to(gates[:, 3 * r + branch:3 * r + branch + 1], (qb, LANES))
    partial = [gate_of(r, 0) * o_cmp[r] + gate_of(r, 2) * o_win[r * qb:(r + 1) * qb] for r in range(rep)]
    gate_sel = jnp.concatenate([gate_of(r, 1) for r in range(rep)], axis=0)

    tpg = NSA_SWEEP_TILES
    kg = tpg * kt

    sel_bias = ((sel_q.astype(F32) - 1.0) * (2.0 ** 100)).astype(BF16)
    q_sel = jnp.concatenate([q, jnp.concatenate([sel_bias] * rep, axis=0)], axis=1)
    n_past = qi // tpg
    diag = qi - n_past * tpg

    def sweep(n_before):
        def run():
            m_run, acc = None, None
            for gi in range(n_before + 1):
                tiles = lane_tiles(_dot_nt(q_sel, ks_ref[0, gi * kg:(gi + 1) * kg, :]))
                if gi == n_before:
                    shifts = [jnp.where(j < diag, -kt, jnp.where(j == diag, 0, kt)) for j in range(tpg)]
                    tiles = [jnp.where(row_l >= lane_k + sh, tile, NEG_BIG) for sh, tile in zip(shifts, tiles)]
                m_grp = jnp.broadcast_to(jnp.max(tile_max(tiles[1:], tiles[0]), axis=-1, keepdims=True), (rows_all, kt))
                m_new = m_grp if m_run is None else jnp.maximum(m_run, m_grp)
                e = jnp.concatenate([jnp.exp(tile - m_new).astype(BF16) for tile in tiles], axis=1)
                pv = _dot(e, vs_ref[0, gi * kg:(gi + 1) * kg, :])
                acc = pv if acc is None else jnp.exp(m_run - m_new) * acc + pv
                m_run = m_new
            acc_scr[...] = gate_sel * normalised(acc)
        return run

    lax.switch(n_past, [sweep(n) for n in range(ks_ref.shape[1] // kg)])
    mixed = [partial[r] + acc_scr[r * qb:(r + 1) * qb, :] for r in range(rep)]
    o_ref[0] = jnp.concatenate([jnp.where(lane_b < dh, mixed[r], pltpu.roll(mixed[r + 1], dh, 1))
                                for r in range(0, rep, 2)], axis=1).astype(o_ref.dtype)


def _nsa_attention(p, kcmp, vcmp, ks, vs, kw, vw, slopes):
    bsz, t, _ = p.shape
    g, rep, qb, dh = NSA_GROUPS, NSA_REP, NSA_Q_BLOCK, NSA_DH
    whole = lambda a: pl.BlockSpec((1, a.shape[1], a.shape[2] // g), lambda b, gi, i: (b, 0, gi))
    gate_col = p.shape[-1] // LANES - 1
    return pl.pallas_call(
        _nsa_attn_kernel,
        grid=(bsz, g, t // qb),
        in_specs=[
            pl.BlockSpec((1, qb, rep * dh), lambda b, gi, i: (b, i, gi)),
            whole(kcmp), whole(vcmp), whole(ks), whole(vs), whole(kw), whole(vw),
            pl.BlockSpec((1, qb, LANES), lambda b, gi, i: (b, i, gate_col)),
            pl.BlockSpec((1, 8, LANES), lambda b, gi, i: (gi, 0, 0)),
        ],
        out_specs=pl.BlockSpec((1, qb, rep * dh), lambda b, gi, i: (b, i, gi)),
        out_shape=jax.ShapeDtypeStruct((bsz, t, g * rep * dh), BF16),
        scratch_shapes=[pltpu.VMEM((rep * qb, LANES), F32)],
        compiler_params=_cparams(("parallel", "parallel", "arbitrary")),
        name="nsa_attention",
    )(p, kcmp, vcmp, ks, vs, kw, vw, p, slopes)


def _even_mixer(x, norm_g, sc, sh, w_in, shift_mu, a_up, a_b, gla_g, w0, w2, a0, a2, g2, k_k, k_a, r_k, gn_w, gn_b):
    d = x.shape[-1]
    qk, vw, w = GLA_HEADS * GLA_DK, GLA_HEADS * GLA_DV, RWKV_HEADS * RWKV_N
    gla_cols = 2 * qk + 2 * vw + GLA_LOWRANK
    wg, wr = w_in[:, :gla_cols], w_in[:, gla_cols:]
    o_r, o_wl, o_k, o_v, o_al, o_gl = np.cumsum([0, w, RWKV_W_LORA, w, w, RWKV_A_LORA]).tolist()
    pad = lambda a, n: jnp.pad(a, ((0, 0), (0, n - a.shape[1])))
    w_perm = jnp.concatenate([
        wg[:, :2 * qk + 2 * vw],
        wr[:, o_r:o_r + w], wr[:, o_k:o_k + w], wr[:, o_v:o_v + w],
        pad(wg[:, 2 * qk + 2 * vw:], LANES),
        wr[:, o_wl:o_wl + RWKV_W_LORA], wr[:, o_al:o_al + RWKV_A_LORA],
        wr[:, o_gl:o_gl + RWKV_G_LORA]], axis=1).astype(BF16)
    p = _norm_proj(x, norm_g, sc, sh, w_perm)
    a_up_pad = jnp.zeros((LANES, qk), F32).at[:GLA_LOWRANK].set(a_up)
    o_gla = _gla(p, a_up_pad, a_b, gla_g)
    mu = shift_mu
    mus = [mu[o_r:o_r + w], mu[o_k:o_k + w], mu[o_v:o_v + w],
           jnp.concatenate([mu[o_wl:o_wl + RWKV_W_LORA], mu[o_al:o_al + RWKV_A_LORA]]), mu[o_gl:o_gl + RWKV_G_LORA]]
    w2p = jnp.zeros((LANES, w), F32).at[:RWKV_W_LORA].set(w2)
    a2p = jnp.zeros((LANES, w), F32).at[RWKV_W_LORA:RWKV_W_LORA + RWKV_A_LORA].set(a2)
    o_rw = _rwkv(p, mus, w0, w2p, a0, a2p, g2, k_k, k_a, r_k.reshape(-1), gn_w, gn_b)
    return [o_gla, o_rw]


def _nsa_mixer(x, norm_g, sc, sh, w_in, pe_k, w1_k, w2_k, pe_v, w1_v, w2_v):
    bsz, t, d = x.shape
    g, dh, heads = NSA_GROUPS, NSA_DH, NSA_HEADS
    n_cols = w_in.shape[1]
    n_pad = -(-n_cols // (3 * LANES)) * (3 * LANES)
    w_pad = jnp.pad(w_in, ((0, 0), (0, n_pad - n_cols))).astype(BF16)
    kv = g * dh
    off = heads * dh
    p, p_kv = _norm_proj(x, norm_g, sc, sh, w_pad, copy_cols=(off + 2 * kv, 4 * kv))
    seg = lambda i: p[..., off + i * kv: off + (i + 1) * kv]
    kcmp, vcmp = _nsa_compress(seg(0), seg(1), pe_k, w1_k, w2_k, pe_v, w1_v, w2_v)
    slopes = 2.0 ** (-8.0 * jnp.arange(1, heads + 1, dtype=F32) / heads)
    slopes = jnp.broadcast_to(jnp.pad(slopes.reshape(g, NSA_REP), ((0, 0), (0, 8 - NSA_REP)))[:, :, None], (g, 8, LANES))

    def per_group(a, *cols, width):
        rows = a.shape[1]
        const = [jnp.broadcast_to(c.astype(BF16)[None], (bsz, rows, c.shape[-1])) for c in cols]
        const.append(jnp.zeros((bsz, rows, width - dh - sum(c.shape[-1] for c in cols)), BF16))
        parts = []
        for gi in range(g):
            parts += [a[..., gi * dh:(gi + 1) * dh].astype(BF16)] + const
        return jnp.concatenate(parts, axis=-1)

    split_pos = lambda q: jnp.tile(jnp.stack([q // LANES * LANES, q % LANES], axis=-1), (1, 3))
    pos = jnp.arange(t)
    pos_cols = split_pos(pos)
    gap = jnp.zeros((t, LANES - dh - pos_cols.shape[1]), BF16)
    blk_cols = pos[:, None] // NSA_SEL_LEN == jnp.arange(LANES)[None, :]
    ones_col = jnp.ones((t, 1), BF16)
    cmp_end_cols = split_pos(jnp.arange(kcmp.shape[1]) * NSA_CMP_STRIDE + NSA_CMP_LEN - 1)
    kv_seg = lambda i: p_kv[..., i * kv:(i + 1) * kv]
    return [_nsa_attention(p, per_group(kcmp, cmp_end_cols, width=LANES), per_group(vcmp, width=LANES),
                           per_group(kv_seg(0), pos_cols, gap, blk_cols, width=2 * LANES),
                           per_group(kv_seg(1), ones_col, width=LANES),
                           per_group(kv_seg(2), pos_cols, width=LANES),
                           per_group(kv_seg(3), ones_col, width=LANES), slopes)]


def kernel(x, c, ada_w, ada_b, norm1_g, norm2_g, ffn_w_up, ffn_conv_w, ffn_conv_b, ffn_w_down, ev_w_in, ev_shift_mu, gla_a_up, gla_a_b, gla_norm_g, rw_w0, rw_w2, rw_a0, rw_a2, rw_g2, rw_k_k, rw_k_a, rw_r_k, rw_gn_w, rw_gn_b, ev_w_out, od_w_in, cmp_pe_k, cmp_w1_k, cmp_w2_k, cmp_pe_v, cmp_w1_v, cmp_w2_v, od_w_out, final_norm_g):
    bsz, t, d = x.shape
    depth = ada_w.shape[0]
    mod = _ada_mod(c, ada_w, ada_b)
    for layer in range(depth):
        sh1, sc1, g1, sh2, sc2, g2 = (mod[layer, :, i * d:(i + 1) * d].reshape(bsz, 1, d) for i in range(6))
        i = layer // 2
        if layer % 2 == 0:
            mix = _even_mixer(x, norm1_g[layer], sc1, sh1, ev_w_in[i], ev_shift_mu[i], gla_a_up[i], gla_a_b[i],
                              gla_norm_g[i], rw_w0[i], rw_w2[i], rw_a0[i], rw_a2[i], rw_g2[i], rw_k_k[i], rw_k_a[i],
                              rw_r_k[i], rw_gn_w[i], rw_gn_b[i])
            w_out = ev_w_out[i]
        else:
            mix = _nsa_mixer(x, norm1_g[layer], sc1, sh1, od_w_in[i], cmp_pe_k[i], cmp_w1_k[i], cmp_w2_k[i],
                             cmp_pe_v[i], cmp_w1_v[i], cmp_w2_v[i])
            w_out = od_w_out[i]
        x = _out_proj(mix, w_out.astype(BF16), x, g1)
        x = _conv_ffn(x, norm2_g[layer], sc2, sh2, g2, ffn_w_up[layer].astype(BF16), ffn_conv_w[layer],
                      ffn_conv_b[layer], ffn_w_down[layer].astype(BF16), final_norm_g,
                      final_norm=(layer == depth - 1))
    return x
```

```python
import functools

import numpy as np
import jax
import jax.numpy as jnp
from jax import lax
from jax.experimental import pallas as pl
from jax.experimental.pallas import tpu as pltpu

F32 = jnp.float32
BF16 = jnp.bfloat16

D_MODEL = 1024
NORM_EPS = 1e-6
GLA_HEADS, GLA_DK, GLA_DV, GLA_LOWRANK, GLA_GATE_NORM, GLA_CHUNK = 4, 64, 128, 16, 16.0, 16
RWKV_HEADS, RWKV_N, RWKV_GN_EPS = 8, 64, 64e-5
RWKV_W_LORA, RWKV_A_LORA, RWKV_G_LORA = 64, 64, 128
RWKV_CHUNK = 64
NSA_HEADS, NSA_GROUPS, NSA_DH = 16, 4, 64
NSA_REP = NSA_HEADS // NSA_GROUPS
NSA_CMP_LEN, NSA_CMP_STRIDE, NSA_CMP_HIDDEN = 32, 16, 64
NSA_SEL_LEN, NSA_N_SEL, NSA_N_LOCAL, NSA_WINDOW, NSA_Q_BLOCK, NSA_FORCE = 64, 8, 2, 512, 128, 100.0
NSA_SWEEP_TILES = 4
FFN_HIDDEN = 2816

LANES = 128
VMEM_LIMIT = 56 * 1024 * 1024
NEG_BIG = -1e30


def _cparams(sem):
    return pltpu.CompilerParams(dimension_semantics=sem, vmem_limit_bytes=VMEM_LIMIT)


def _dot(a, b):
    return jnp.dot(a, b, preferred_element_type=F32)


def _dot_nt(a, b):
    return lax.dot_general(a, b, (((1,), (1,)), ((), ())), preferred_element_type=F32)


def _mm(a, b):
    return _dot(a.astype(BF16), b.astype(BF16))


def _mm_nt(a, b):
    return _dot_nt(a.astype(BF16), b.astype(BF16))


def _split3(a):
    a1 = a.astype(BF16)
    r1 = a - a1.astype(F32)
    a2 = r1.astype(BF16)
    a3 = (r1 - a2.astype(F32)).astype(BF16)
    return a1, a2, a3


def _mm_sel(a, b01, terms=3):
    b = b01.astype(BF16)
    out = None
    for part in _split3(a)[:terms]:
        out = _dot(part, b) if out is None else out + _dot(part, b)
    return out


def _sel_mm(a01, b):
    b1, b2, b3 = _split3(b)
    a = a01.astype(BF16)
    return _dot(a, b1) + _dot(a, b2) + _dot(a, b3)


def _mm3(a, b):
    a1 = a.astype(BF16)
    a2 = (a - a1.astype(F32)).astype(BF16)
    b1 = b.astype(BF16)
    b2 = (b - b1.astype(F32)).astype(BF16)
    return _dot(a1, b1) + _dot(a1, b2) + _dot(a2, b1)


def _mm3_nt(a, b):
    a1 = a.astype(BF16)
    a2 = (a - a1.astype(F32)).astype(BF16)
    b1 = b.astype(BF16)
    b2 = (b - b1.astype(F32)).astype(BF16)
    return _dot_nt(a1, b1) + _dot_nt(a1, b2) + _dot_nt(a2, b1)


def _iota(shape, dim):
    return lax.broadcasted_iota(jnp.int32, shape, dim)


def _sigmoid(x):
    return 1.0 / (1.0 + jnp.exp(-x))


def _softplus(x):
    return jnp.maximum(x, 0.0) + jnp.log(1.0 + jnp.exp(-jnp.abs(x)))


def _gelu_tanh(x):
    return x * (0.5 * (1.0 + jnp.tanh(0.7978845608028654 * (x + 0.044715 * (x * x * x)))))


def _rms(x, eps):
    return x * lax.rsqrt(jnp.mean(x * x, axis=-1, keepdims=True) + eps)


def _mod_kernel(c_ref, w_ref, b_ref, o_ref):
    c = c_ref[...]
    cond = c * _sigmoid(c)
    o_ref[0] = _mm3(cond, w_ref[0]) + b_ref[0]


def _ada_mod(c, ada_w, ada_b):
    depth, d, n = ada_w.shape
    bsz = c.shape[0]
    rows = 8
    c8 = jnp.zeros((rows, d), F32).at[:bsz].set(c)
    tn = 1536
    out = pl.pallas_call(
        _mod_kernel,
        grid=(depth, n // tn),
        in_specs=[
            pl.BlockSpec((rows, d), lambda l, j: (0, 0)),
            pl.BlockSpec((1, d, tn), lambda l, j: (l, 0, j)),
            pl.BlockSpec((1, 1, tn), lambda l, j: (l, 0, j)),
        ],
        out_specs=pl.BlockSpec((1, rows, tn), lambda l, j: (l, 0, j)),
        out_shape=jax.ShapeDtypeStruct((depth, rows, n), F32),
        compiler_params=_cparams(("parallel", "parallel")),
        name="ada_mod",
    )(c8, ada_w, ada_b.reshape(depth, 1, n))
    return out[:, :bsz]


def _norm_proj_kernel(x_ref, g_ref, sc_ref, sh_ref, w_ref, o_ref, *kv_refs, nsa_kv_start):
    hn = _rms(x_ref[0], NORM_EPS) * g_ref[...]
    hn = (hn * (1.0 + sc_ref[0]) + sh_ref[0]).astype(BF16)
    out = _dot(hn, w_ref[...])
    o_ref[0] = out
    if nsa_kv_start is None:
        return
    tm, dh, kv = out.shape[0], NSA_DH, NSA_GROUPS * NSA_DH
    lane = _iota((tm, LANES), 1)
    pos = pl.program_id(1) * tm + _iota((tm, LANES), 0)
    pos_hi, pos_lo = (pos // LANES * LANES).astype(F32), (pos % LANES).astype(F32)
    in_pos = (lane >= dh) & (lane < dh + 6)
    pos_cols = jnp.where(in_pos & ((lane - dh) % 2 == 0), pos_hi, jnp.where(in_pos, pos_lo, 0.0))
    ones_col = jnp.where(lane == dh, 1.0, 0.0)
    block_onehot = jnp.where(lane == pos // NSA_SEL_LEN, 1.0, 0.0)

    def group_tile(seg, gi):
        c0 = nsa_kv_start + seg * kv + (gi // 2) * LANES
        tile = out[:, c0:c0 + LANES]
        return pltpu.roll(tile, dh, 1) if gi % 2 else tile

    def layout(seg, const, extra=None):
        tiles = []
        for gi in range(NSA_GROUPS):
            tiles.append(jnp.where(lane < dh, group_tile(seg, gi), const))
            if extra is not None:
                tiles.append(extra)
        return jnp.concatenate(tiles, axis=1).astype(BF16)

    ks_ref, vs_ref, kw_ref, vw_ref = kv_refs
    ks_ref[0] = layout(0, pos_cols, block_onehot)
    vs_ref[0] = layout(1, ones_col)
    kw_ref[0] = layout(2, pos_cols)
    vw_ref[0] = layout(3, ones_col)


def _norm_proj(x, g, sc, sh, w, *, tm=512, nsa_kv_start=None):
    bsz, t, d = x.shape
    n = w.shape[1]
    out_specs = [pl.BlockSpec((1, tm, n), lambda b, i: (b, i, 0))]
    out_shape = [jax.ShapeDtypeStruct((bsz, t, n), F32)]
    if nsa_kv_start is not None:
        for width in (2 * LANES, LANES, LANES, LANES):
            out_specs.append(pl.BlockSpec((1, tm, NSA_GROUPS * width), lambda b, i: (b, i, 0)))
            out_shape.append(jax.ShapeDtypeStruct((bsz, t, NSA_GROUPS * width), BF16))
    outs = pl.pallas_call(
        functools.partial(_norm_proj_kernel, nsa_kv_start=nsa_kv_start),
        grid=(bsz, t // tm),
        in_specs=[
            pl.BlockSpec((1, tm, d), lambda b, i: (b, i, 0)),
            pl.BlockSpec((1, d), lambda b, i: (0, 0)),
            pl.BlockSpec((1, 1, d), lambda b, i: (b, 0, 0)),
            pl.BlockSpec((1, 1, d), lambda b, i: (b, 0, 0)),
            pl.BlockSpec((d, n), lambda b, i: (0, 0)),
        ],
        out_specs=out_specs,
        out_shape=out_shape,
        compiler_params=_cparams(("parallel", "parallel")),
        name="norm_proj",
    )(x, g.reshape(1, d), sc, sh, w)
    return outs[0] if nsa_kv_start is None else outs


def _out_proj_kernel(*refs):
    *mw_refs, x_ref, gate_ref, o_ref = refs
    n = len(mw_refs) // 2
    proj = _mm(mw_refs[0][0], mw_refs[n][...])
    for m_ref, w_ref in zip(mw_refs[1:n], mw_refs[n + 1:]):
        proj = proj + _mm(m_ref[0], w_ref[...])
    o_ref[0] = x_ref[0] + gate_ref[0] * proj


def _out_proj(mixes, w, x, gate, *, tm=512):
    bsz, t, d = x.shape
    widths = [m.shape[-1] for m in mixes]
    offs = np.cumsum([0] + widths).tolist()
    ws = [w[o:o + k] for o, k in zip(offs, widths)]
    return pl.pallas_call(
        _out_proj_kernel,
        grid=(bsz, t // tm),
        in_specs=[pl.BlockSpec((1, tm, k), lambda b, i: (b, i, 0)) for k in widths]
        + [pl.BlockSpec((k, d), lambda b, i: (0, 0)) for k in widths]
        + [pl.BlockSpec((1, tm, d), lambda b, i: (b, i, 0)), pl.BlockSpec((1, 1, d), lambda b, i: (b, 0, 0))],
        out_specs=pl.BlockSpec((1, tm, d), lambda b, i: (b, i, 0)),
        out_shape=jax.ShapeDtypeStruct((bsz, t, d), F32),
        compiler_params=_cparams(("parallel", "parallel")),
        name="out_proj",
    )(*mixes, *ws, x, gate)


def _ffn_kernel(x_ref, g_ref, sc_ref, sh_ref, gate_ref, wu_ref, wv_ref, cw_ref, cb_ref, wd_ref, fg_ref,
                o_ref, hn_ref, acc_ref, halo_ref, *, final_norm):
    ti = pl.program_id(1)
    fj = pl.program_id(2)
    tm, fk = acc_ref.shape[0], wu_ref.shape[1]

    @pl.when(fj == 0)
    def _():
        hn = _rms(x_ref[0], NORM_EPS) * g_ref[...]
        hn_ref[...] = (hn * (1.0 + sc_ref[0]) + sh_ref[0]).astype(BF16)
        acc_ref[...] = jnp.zeros_like(acc_ref)

    @pl.when(ti == 0)
    def _():
        halo_ref[fj] = jnp.zeros((8, fk), F32)

    hn = hn_ref[...]
    u = _dot(hn, wu_ref[...])
    v = _dot(hn, wv_ref[...])
    prev = halo_ref[fj]
    row = _iota((tm, fk), 0)
    u1 = jnp.where(row == 0, prev[7:8], pltpu.roll(u, 1, 0))
    u2 = jnp.where(row == 0, prev[6:7], jnp.where(row == 1, prev[7:8], pltpu.roll(u, 2, 0)))
    halo_ref[fj] = u[tm - 8:tm]
    cw = cw_ref[...]
    uc = cw[0:1] * u2 + cw[1:2] * u1 + cw[2:3] * u + cb_ref[...]
    h = _gelu_tanh(uc) * v
    acc_ref[...] += _dot(h.astype(BF16), wd_ref[...])

    @pl.when(fj == pl.num_programs(2) - 1)
    def _():
        y = x_ref[0] + gate_ref[0] * acc_ref[...]
        if final_norm:
            y = _rms(y, NORM_EPS) * fg_ref[...]
        o_ref[0] = y


def _conv_ffn(x, g, sc, sh, gate, w_up, conv_w, conv_b, w_down, final_g, *, final_norm, tm=512, fk=2816):
    bsz, t, d = x.shape
    f = w_down.shape[0]
    nf = f // fk
    kern = functools.partial(_ffn_kernel, final_norm=final_norm)
    return pl.pallas_call(
        kern,
        grid=(bsz, t // tm, nf),
        in_specs=[
            pl.BlockSpec((1, tm, d), lambda b, i, j: (b, i, 0)),
            pl.BlockSpec((1, d), lambda b, i, j: (0, 0)),
            pl.BlockSpec((1, 1, d), lambda b, i, j: (b, 0, 0)),
            pl.BlockSpec((1, 1, d), lambda b, i, j: (b, 0, 0)),
            pl.BlockSpec((1, 1, d), lambda b, i, j: (b, 0, 0)),
            pl.BlockSpec((d, fk), lambda b, i, j: (0, j)),
            pl.BlockSpec((d, fk), lambda b, i, j: (0, j + nf)),
            pl.BlockSpec((3, fk), lambda b, i, j: (0, j)),
            pl.BlockSpec((1, fk), lambda b, i, j: (0, j)),
            pl.BlockSpec((fk, d), lambda b, i, j: (j, 0)),
            pl.BlockSpec((1, d), lambda b, i, j: (0, 0)),
        ],
        out_specs=pl.BlockSpec((1, tm, d), lambda b, i, j: (b, i, 0)),
        out_shape=jax.ShapeDtypeStruct((bsz, t, d), F32),
        scratch_shapes=[pltpu.VMEM((tm, d), BF16), pltpu.VMEM((tm, d), F32), pltpu.VMEM((nf, 8, fk), F32)],
        compiler_params=_cparams(("parallel", "arbitrary", "arbitrary")),
        name="conv_ffn",
    )(x, g.reshape(1, d), sc, sh, gate, w_up, w_up, conv_w, conv_b.reshape(1, f), w_down, final_g.reshape(1, d))


def _gla_kernel(q_ref, k_ref, v_ref, og_ref, lr_ref, aup_ref, ab_ref, gg_ref, o_ref,
                st_ref, b_scr, bend_scr, q_scr, k_scr, kd_scr, qg_scr, vt_scr, o_scr):
    tb = q_ref.shape[1]
    cs = GLA_CHUNK
    n_chunks = tb // cs

    @pl.when(pl.program_id(1) == 0)
    def _():
        st_ref[...] = jnp.zeros_like(st_ref)

    z = _mm3(lr_ref[0], aup_ref[...]) + ab_ref[...]
    la = -_softplus(-z) * (1.0 / GLA_GATE_NORM)
    rr, cc = _iota((tb, tb), 0), _iota((tb, tb), 1)
    same = (rr // cs) == (cc // cs)
    b = _sel_mm(jnp.where(same & (cc <= rr), 1.0, 0.0), la)
    bend = _sel_mm(jnp.where(same, 1.0, 0.0), la)
    q = q_ref[0] * (GLA_DK ** -0.5)
    k = k_ref[0]
    b_scr[...] = b
    bend_scr[...] = bend
    q_scr[...] = q
    k_scr[...] = k
    kd_scr[...] = (k * jnp.exp(bend - b)).astype(BF16)
    qg_scr[...] = q * jnp.exp(b)
    v_all = v_ref[0]
    for h in range(GLA_HEADS):
        vt_scr[h] = v_all[:, h * GLA_DV:(h + 1) * GLA_DV].T.astype(BF16)

    lane_c = _iota((cs, LANES), 1)
    row_c = _iota((cs, LANES), 0)
    lane_t = _iota((tb, LANES), 1)
    head_rows = _iota((LANES, LANES), 0) // GLA_DK

    def chunk(c, carry):
        r0 = c * cs
        tmask = (lane_t // cs) == c
        for p in range(GLA_HEADS // 2):
            lanes = slice(p * LANES, (p + 1) * LANES)
            bc = b_scr[pl.ds(r0, cs), lanes]
            qc = q_scr[pl.ds(r0, cs), lanes]
            kc = k_scr[pl.ds(r0, cs), lanes]
            blocks = []
            for s in range(cs):
                m = row_c >= s
                rel = jnp.where(m, bc - bc[s:s + 1], 0.0)
                blocks.append(jnp.where(m, qc * kc[s:s + 1] * jnp.exp(rel), 0.0))
            a_st = jnp.concatenate(blocks, axis=0)
            qgc = qg_scr[pl.ds(r0, cs), lanes]
            lhs = jnp.concatenate([jnp.where(lane_c < GLA_DK, qgc, 0.0),
                                   jnp.where(lane_c >= GLA_DK, qgc, 0.0)], axis=0)
            st = st_ref[p]
            o_inter = _mm_nt(lhs, st)
            for hh in range(2):
                h = 2 * p + hh
                zsum = _mm_sel(a_st, jnp.where(head_rows == hh, 1.0, 0.0), terms=1)
                vc = v_ref[0, pl.ds(r0, cs), h * GLA_DV:(h + 1) * GLA_DV]
                o_h = o_inter[hh * cs:(hh + 1) * cs]
                for s in range(cs):
                    o_h = o_h + zsum[s * cs:(s + 1) * cs] * vc[s:s + 1]
                o_scr[pl.ds(r0, cs), h * GLA_DV:(h + 1) * GLA_DV] = o_h
            lhs_u = jnp.concatenate([jnp.where(tmask, vt_scr[2 * p], 0.0).astype(BF16),
                                     jnp.where(tmask, vt_scr[2 * p + 1], 0.0).astype(BF16)], axis=0)
            upd = _dot(lhs_u, kd_scr[:, lanes])
            upd = jnp.where(lane_t < GLA_DK, upd[:GLA_DV], upd[GLA_DV:])
            decay = jnp.exp(bend_scr[pl.ds(r0, 1), lanes])
            st_ref[p] = st * decay + upd
        return carry

    for c in range(n_chunks):
        chunk(c, 0)

    og = og_ref[0]
    for h in range(GLA_HEADS):
        sl = slice(h * GLA_DV, (h + 1) * GLA_DV)
        gate = og[:, sl]
        o_ref[0, :, sl] = (_rms(o_scr[:, sl], NORM_EPS) * gg_ref[...] * (gate * _sigmoid(gate))).astype(o_ref.dtype)


def _gla(p, a_up_pad, a_b, gla_g, *, tb=128):
    bsz, t, _ = p.shape
    qk = GLA_HEADS * GLA_DK
    vw = GLA_HEADS * GLA_DV
    assert tb == LANES and GLA_DV == LANES
    return pl.pallas_call(
        _gla_kernel,
        grid=(bsz, t // tb),
        in_specs=[
            pl.BlockSpec((1, tb, qk), lambda b, i: (b, i, 0)),
            pl.BlockSpec((1, tb, qk), lambda b, i: (b, i, 1)),
            pl.BlockSpec((1, tb, vw), lambda b, i: (b, i, 1)),
            pl.BlockSpec((1, tb, vw), lambda b, i: (b, i, 2)),
            pl.BlockSpec((1, tb, LANES), lambda b, i: (b, i, 24)),
            pl.BlockSpec((LANES, qk), lambda b, i: (0, 0)),
            pl.BlockSpec((1, qk), lambda b, i: (0, 0)),
            pl.BlockSpec((1, GLA_DV), lambda b, i: (0, 0)),
        ],
        out_specs=pl.BlockSpec((1, tb, vw), lambda b, i: (b, i, 0)),
        out_shape=jax.ShapeDtypeStruct((bsz, t, vw), BF16),
        scratch_shapes=[
            pltpu.VMEM((GLA_HEADS // 2, GLA_DV, LANES), F32),
            pltpu.VMEM((tb, qk), F32), pltpu.VMEM((tb, qk), F32),
            pltpu.VMEM((tb, qk), F32), pltpu.VMEM((tb, qk), F32),
            pltpu.VMEM((tb, qk), BF16), pltpu.VMEM((tb, qk), F32),
            pltpu.VMEM((GLA_HEADS, GLA_DV, tb), BF16),
            pltpu.VMEM((tb, vw), F32),
        ],
        compiler_params=_cparams(("parallel", "arbitrary")),
        name="gla",
    )(p, p, p, p, p, a_up_pad, a_b.reshape(1, qk), gla_g.reshape(1, GLA_DV))


def _rwkv_kernel(r_ref, k_ref, v_ref, wa_ref, gl_ref, mur_ref, muk_ref, muv_ref, muwa_ref, mugl_ref,
                 w0_ref, w2_ref, a0_ref, a2_ref, g2_ref, kkw_ref, ka_ref, rk_ref, gnw_ref, gnb_ref,
                 o_ref,
                 lr_scr, lk_scr, lv_scr, lwa_scr, lgl_scr, s_ref,
                 lw_p, r_p, kk_p, be_p, k2_p, v_p, y_p, pp_all, y0_all, ge_all, g_all, h_all):
    tb = r_ref.shape[1]
    cs = RWKV_CHUNK
    n_chunks = tb // cs
    n_pairs = RWKV_HEADS // 2
    ti = pl.program_id(1)
    lasts = (lr_scr, lk_scr, lv_scr, lwa_scr, lgl_scr)

    @pl.when(ti == 0)
    def _():
        s_ref[...] = jnp.zeros_like(s_ref)
        for ref in lasts:
            ref[...] = jnp.zeros_like(ref)

    def shifted(x_ref, last_ref, mu_ref):
        x = x_ref[0]
        row = _iota(x.shape, 0)
        prev = jnp.where(row == 0, last_ref[7:8], pltpu.roll(x, 1, 0))
        last_ref[...] = x[tb - 8:tb]
        return x + (prev - x) * mu_ref[...]

    r = shifted(r_ref, lr_scr, mur_ref)
    k = shifted(k_ref, lk_scr, muk_ref)
    v = shifted(v_ref, lv_scr, muv_ref)
    wa = shifted(wa_ref, lwa_scr, muwa_ref)
    gl = shifted(gl_ref, lgl_scr, mugl_ref)

    logw = -jnp.exp(-_softplus(-(w0_ref[...] + _mm3(jnp.tanh(wa), w2_ref[...]))) - 0.5)
    a = _sigmoid(a0_ref[...] + _mm3(wa, a2_ref[...]))
    g = _mm(_sigmoid(gl), g2_ref[...])
    seg = jnp.where((_iota((LANES, LANES), 0) // RWKV_N) == (_iota((LANES, LANES), 1) // RWKV_N), 1.0, 0.0)

    def segsum(x):
        return jnp.concatenate([_mm_sel(x[:, i * LANES:(i + 1) * LANES], seg, terms=2) for i in range(n_pairs)], axis=1)

    kk = k * kkw_ref[...]
    kk = kk * lax.rsqrt(jnp.maximum(segsum(kk * kk), 1e-24))
    k2 = k * (1.0 + (a - 1.0) * ka_ref[...])
    beta = kk * a
    for p in range(n_pairs):
        sl = slice(p * LANES, (p + 1) * LANES)
        lw_p[p] = logw[:, sl]
        r_p[p] = r[:, sl]
        kk_p[p] = kk[:, sl]
        be_p[p] = beta[:, sl]
        k2_p[p] = k2[:, sl]
        v_p[p] = v[:, sl]

    rr, cc = _iota((tb, tb), 0), _iota((tb, tb), 1)
    same = (rr // cs) == (cc // cs)
    tri_incl = same & (cc <= rr)
    tri_strict = same & (cc < rr)
    l_incl = jnp.where(tri_incl, 1.0, 0.0)
    l_all = jnp.where(same, 1.0, 0.0)
    eye = jnp.where(rr == cc, 1.0, 0.0)
    lane = _iota((tb, LANES), 1)
    half = lane < RWKV_N
    same_half = (rr // RWKV_N) == (cc // RWKV_N)
    lane_c = _iota((cs, LANES), 1)
    lane_s = _iota((RWKV_N, LANES), 1)

    def below_left(s):
        return ((rr // (2 * s)) == (cc // (2 * s))) & ((rr % (2 * s)) >= s) & ((cc % (2 * s)) < s)

    pairs = range(n_pairs)
    heads = [(p, hh) for p in pairs for hh in range(2)]
    l_both = jnp.concatenate([l_incl, l_all], axis=0)
    rt, bt, at, vp, ak = {}, {}, {}, {}, {}
    for p in pairs:
        lw = lw_p[p]
        sums = _sel_mm(l_both, lw)
        cum = sums[:tb]
        ge_all[p] = jnp.exp(sums[tb:])
        ig = jnp.exp(-cum)
        rt[p] = r_p[p] * jnp.exp(cum)
        bt[p] = kk_p[p] * jnp.exp(cum - lw)
        at[p] = -(be_p[p] * ig)
        vp[p] = v_p[p]
        ak[p] = jnp.concatenate([at[p], k2_p[p] * ig], axis=0)
    a_ab, a_ak, a_ra, a_rk = {}, {}, {}, {}
    for h in heads:
        p, hh = h
        hm = half if hh == 0 else jnp.logical_not(half)
        s_b = _mm3_nt(jnp.where(hm, bt[p], 0.0), ak[p])
        s_r = _mm_nt(jnp.where(hm, rt[p], 0.0), ak[p])
        a_ab[h] = jnp.where(tri_strict, s_b[:, :tb], 0.0)
        a_ak[h] = jnp.where(tri_strict, s_b[:, tb:], 0.0)
        a_ra[h] = jnp.where(tri_incl, s_r[:, :tb], 0.0)
        a_rk[h] = jnp.where(tri_incl, s_r[:, tb:], 0.0)
    m_inv = {h: eye + jnp.where(below_left(1), a_ab[h], 0.0) for h in heads}
    w1 = {h: _mm3(a_ak[h], vp[h[0]]) for h in heads}
    s = 2
    while s < cs:
        low = {h: _mm3(m_inv[h], jnp.where(below_left(s), a_ab[h], 0.0)) for h in heads}
        m_inv = {h: m_inv[h] + _mm3(low[h], m_inv[h]) for h in heads}
        s *= 2
    mw = {h: _mm3(m_inv[h], jnp.concatenate([w1[h], bt[h[0]]], axis=1)) for h in heads}
    p_h = {h: _mm(a_ra[h], mw[h][:, LANES:]) for h in heads}
    y0_h = {h: _mm(a_ra[h], mw[h][:, :LANES]) + _mm(a_rk[h], vp[h[0]]) for h in heads}
    zeros = jnp.zeros((tb, LANES), F32)
    for p in pairs:
        u0 = jnp.where(half, mw[(p, 0)][:, :LANES], mw[(p, 1)][:, :LANES])
        mb = jnp.where(half, mw[(p, 0)][:, LANES:], mw[(p, 1)][:, LANES:])
        pp_all[p] = rt[p] + jnp.where(half, p_h[(p, 0)], p_h[(p, 1)])
        y0_all[p] = jnp.where(half, y0_h[(p, 0)], y0_h[(p, 1)])
        u0t, vt, mbt = u0.T, vp[p].T, mb.T
        for c in range(n_chunks):
            tmask = (lane // cs) == c
            lhs = jnp.concatenate([
                jnp.concatenate([jnp.where(tmask, u0t, 0.0), jnp.where(tmask, vt, 0.0)], axis=1),
                jnp.concatenate([jnp.where(tmask, mbt, 0.0), zeros], axis=1)], axis=0)
            hg = _mm3(lhs, ak[p])
            h_all[p * n_chunks + c] = jnp.where(lane_s < RWKV_N, hg[:RWKV_N], hg[RWKV_N:tb])
            g_all[p * n_chunks + c] = jnp.where(same_half, hg[tb:], 0.0) + eye

    state = [s_ref[p] for p in range(n_pairs)]
    for c in range(n_chunks):
        rows = slice(c * cs, (c + 1) * cs)
        for p in range(n_pairs):
            sp = state[p]
            pc = pp_all[p, rows, :]
            lhs_y = jnp.concatenate([jnp.where(lane_c < RWKV_N, pc, 0.0), jnp.where(lane_c >= RWKV_N, pc, 0.0)], axis=0)
            yy = _mm_nt(lhs_y, jnp.concatenate([sp, sp], axis=0))
            y_p[p, rows, :] = jnp.where(lane_c < RWKV_N, yy[:cs], yy[cs:]) + y0_all[p, rows, :]
            state[p] = (_mm3(sp, g_all[p * n_chunks + c]) + h_all[p * n_chunks + c]) * ge_all[p, c * cs:c * cs + 1, :]
    for p in range(n_pairs):
        s_ref[p] = state[p]

    y = jnp.concatenate([y_p[p] for p in range(n_pairs)], axis=1)
    mu = segsum(y) * (1.0 / RWKV_N)
    yc = y - mu
    var = segsum(yc * yc) * (1.0 / RWKV_N)
    yn = yc * lax.rsqrt(var + RWKV_GN_EPS) * gnw_ref[...] + gnb_ref[...]
    bonus = segsum(r * k2 * rk_ref[...]) * v
    o_ref[0] = ((yn + bonus) * g).astype(o_ref.dtype)


def _rwkv(p, mus, w0, w2p, a0, a2p, g2, k_k, k_a, r_k, gn_w, gn_b, *, tb=128):
    bsz, t, _ = p.shape
    w = RWKV_HEADS * RWKV_N
    n_pairs = RWKV_HEADS // 2
    assert tb == LANES
    row = lambda a: a.reshape(1, -1)
    full = lambda shape: pl.BlockSpec(shape, lambda b, i: (0,) * len(shape))
    ptile = lambda: pltpu.VMEM((n_pairs, tb, LANES), F32)
    return pl.pallas_call(
        _rwkv_kernel,
        grid=(bsz, t // tb),
        in_specs=[
            pl.BlockSpec((1, tb, w), lambda b, i: (b, i, 3)),
            pl.BlockSpec((1, tb, w), lambda b, i: (b, i, 4)),
            pl.BlockSpec((1, tb, w), lambda b, i: (b, i, 5)),
            pl.BlockSpec((1, tb, LANES), lambda b, i: (b, i, 25)),
            pl.BlockSpec((1, tb, LANES), lambda b, i: (b, i, 26)),
            full((1, w)), full((1, w)), full((1, w)), full((1, LANES)), full((1, LANES)),
            full((1, w)), full((LANES, w)), full((1, w)), full((LANES, w)), full((LANES, w)),
            full((1, w)), full((1, w)), full((1, w)), full((1, w)), full((1, w)),
        ],
        out_specs=pl.BlockSpec((1, tb, w), lambda b, i: (b, i, 0)),
        out_shape=jax.ShapeDtypeStruct((bsz, t, w), BF16),
        scratch_shapes=[
            pltpu.VMEM((8, w), F32), pltpu.VMEM((8, w), F32), pltpu.VMEM((8, w), F32),
            pltpu.VMEM((8, LANES), F32), pltpu.VMEM((8, LANES), F32),
            pltpu.VMEM((n_pairs, RWKV_N, LANES), F32),
            ptile(), ptile(), ptile(), ptile(), ptile(), ptile(), ptile(), ptile(), ptile(), ptile(),
            pltpu.VMEM((n_pairs * (tb // RWKV_CHUNK), LANES, LANES), F32),
            pltpu.VMEM((n_pairs * (tb // RWKV_CHUNK), RWKV_N, LANES), F32),
        ],
        compiler_params=_cparams(("parallel", "arbitrary")),
        name="rwkv7",
    )(p, p, p, p, p, *[row(m) for m in mus], row(w0), w2p, row(a0), a2p, g2,
      row(k_k), row(k_a), row(r_k), row(gn_w), row(gn_b))


def _cmp_kernel(sk_ref, sv_ref, wak_ref, wbk_ref, wav_ref, wbv_ref, pek_ref, pev_ref, w1k_ref, w1v_ref,
                w2k_ref, w2v_ref, ok_ref, ov_ref):
    def one(seg_ref, wa_ref, wb_ref, pe_ref, w1_ref, w2_ref, o_ref):
        seg = seg_ref[0].astype(BF16)
        first = _dot(seg, wa_ref[...])
        second = _dot(seg, wb_ref[...])
        n = first.shape[0]
        pe_term = _mm3(pe_ref[...], w1_ref[...])[0:1]
        hidden = _gelu_tanh(first + pltpu.roll(second, n - 1, 0) + pe_term)
        o_ref[0] = _mm(hidden, w2_ref[...])

    one(sk_ref, wak_ref, wbk_ref, pek_ref, w1k_ref, w2k_ref, ok_ref)
    one(sv_ref, wav_ref, wbv_ref, pev_ref, w1v_ref, w2v_ref, ov_ref)


def _nsa_compress(kc_tok, vc_tok, pe_k, w1_k, w2_k, pe_v, w1_v, w2_v):
    bsz, t, gw = kc_tok.shape
    st, dh, hid, g = NSA_CMP_STRIDE, NSA_DH, NSA_CMP_HIDDEN, NSA_GROUPS
    nseg = t // st
    eye = jnp.eye(g, dtype=F32)

    def expand_w1(w1):
        w = w1.reshape(NSA_CMP_LEN, dh, hid)
        big = jnp.einsum('ldc,gh->lgdhc', w, eye).reshape(NSA_CMP_LEN * g * dh, g * hid)
        half = st * g * dh
        return big[:half].astype(BF16), big[half:].astype(BF16)

    def expand_w2(w2):
        return jnp.einsum('cd,gh->gchd', w2, eye).reshape(g * hid, g * dh).astype(BF16)

    def pe_rows(pe):
        return jnp.zeros((8, NSA_CMP_LEN * dh), F32).at[0].set(pe.reshape(-1))

    wak, wbk = expand_w1(w1_k)
    wav, wbv = expand_w1(w1_v)
    full = lambda shape: pl.BlockSpec(shape, lambda b: (0,) * len(shape))
    seg_spec = pl.BlockSpec((1, nseg, st * gw), lambda b: (b, 0, 0))
    out_spec = pl.BlockSpec((1, nseg, gw), lambda b: (b, 0, 0))
    return pl.pallas_call(
        _cmp_kernel,
        grid=(bsz,),
        in_specs=[seg_spec, seg_spec,
                  full(wak.shape), full(wbk.shape), full(wav.shape), full(wbv.shape),
                  full((8, NSA_CMP_LEN * dh)), full((8, NSA_CMP_LEN * dh)),
                  full((NSA_CMP_LEN * dh, g * hid)), full((NSA_CMP_LEN * dh, g * hid)),
                  full((g * hid, gw)), full((g * hid, gw))],
        out_specs=[out_spec, out_spec],
        out_shape=[jax.ShapeDtypeStruct((bsz, nseg, gw), F32)] * 2,
        compiler_params=_cparams(("parallel",)),
        name="nsa_compress",
    )(kc_tok.reshape(bsz, nseg, st * gw), vc_tok.reshape(bsz, nseg, st * gw),
      wak, wbk, wav, wbv, pe_rows(pe_k), pe_rows(pe_v),
      jnp.tile(w1_k, (1, g)), jnp.tile(w1_v, (1, g)), expand_w2(w2_k), expand_w2(w2_v))


def _nsa_attn_kernel(q_ref, kc_ref, vc_ref, ks_ref, vs_ref, kw_ref, vw_ref, gt_ref, sl_ref, o_ref, acc_scr):
    g = pl.program_id(1)
    qi = pl.program_id(2)
    qb, dh, rep = NSA_Q_BLOCK, NSA_DH, NSA_REP
    n_cmp_pad = kc_ref.shape[1]
    kt = LANES
    rows_all = rep * qb

    lane_q = _iota((1, LANES), 1)
    lane_b = _iota((qb, LANES), 1)
    q_rows = []
    for r in range(rep):
        s1, s2, s3 = (t.astype(F32) for t in _split3(sl_ref[0, r:r + 1, :]))
        slope_cols = jnp.where((lane_q >= dh) & (lane_q < dh + 2), s1,
                               jnp.where((lane_q >= dh + 2) & (lane_q < dh + 4), s2,
                                         jnp.where((lane_q >= dh + 4) & (lane_q < dh + 6), s3, 0.0)))
        two_heads = q_ref[0, :, (r // 2) * LANES:(r // 2 + 1) * LANES]
        if r % 2:
            two_heads = pltpu.roll(two_heads, dh, 1)
        q_rows.append(jnp.where(lane_b < dh, two_heads * (dh ** -0.5), slope_cols))
    q = jnp.concatenate(q_rows, axis=0).astype(BF16)
    t0 = qi * qb

    row_l = _iota((rows_all, kt), 0) % qb
    lane_k = _iota((rows_all, kt), 1)
    causal = row_l >= lane_k
    win_lo = row_l < lane_k

    def lane_tiles(s):
        return [s[:, j * kt:(j + 1) * kt] for j in range(s.shape[1] // kt)]

    def tile_max(tiles, start):
        m = start
        for tile in tiles:
            m = jnp.maximum(m, tile)
        return m

    def normalised(acc):
        return acc / acc[:, dh:dh + 1]

    gsel = jnp.where(_iota((LANES, LANES), 0) == g * (rep * 3) + _iota((LANES, LANES), 1), 1.0, 0.0)
    gates = _sigmoid(_mm_sel(gt_ref[0], gsel))

    n_idx = _iota((rows_all, n_cmp_pad), 1)
    row_c = _iota((rows_all, n_cmp_pad), 0) % qb
    valid_c = (t0 + row_c >= n_idx * NSA_CMP_STRIDE + NSA_CMP_LEN - 1) & (n_idx < n_cmp_pad - 1)
    s_cmp = jnp.where(valid_c, _dot_nt(q, kc_ref[0]), NEG_BIG)
    e_cmp = jnp.where(valid_c, jnp.exp(s_cmp - jnp.max(s_cmp, axis=-1, keepdims=True)), 0.0)
    den = jnp.sum(e_cmp, axis=-1, keepdims=True)
    p_cmp = e_cmp / jnp.where(den > 0.0, den, 1.0)
    o_cmp_all = _mm(p_cmp, vc_ref[0])
    o_cmp = [o_cmp_all[r * qb:(r + 1) * qb] for r in range(rep)]
    n_sel_blocks = ks_ref.shape[1] // NSA_SEL_LEN
    on, oj = _iota((n_cmp_pad, LANES), 0), _iota((n_cmp_pad, LANES), 1)
    overlap = jnp.where((on * NSA_CMP_STRIDE <= oj * NSA_SEL_LEN + NSA_SEL_LEN - 1)
                        & (on * NSA_CMP_STRIDE + NSA_CMP_LEN - 1 >= oj * NSA_SEL_LEN)
                        & (oj < n_sel_blocks) & (on < n_cmp_pad - 1), 1.0, 0.0)
    p_group = p_cmp[0:qb]
    for r in range(1, rep):
        p_group = p_group + p_cmp[r * qb:(r + 1) * qb]
    imp = _mm_sel(p_group, overlap)

    imp_t = imp.T[:n_sel_blocks]
    jj = _iota((n_sel_blocks, qb), 0)
    jf = jj.astype(F32)
    ahead = (t0 + _iota((n_sel_blocks, qb), 1)) // NSA_SEL_LEN - jj
    valid_b = ahead >= 0
    forced = (jj == 0) | (valid_b & (ahead < NSA_N_LOCAL))
    score = jnp.where(valid_b, imp_t + jnp.where(forced, NSA_FORCE, 0.0), -NSA_FORCE)
    sel = jnp.zeros((n_sel_blocks, qb), F32)
    for _ in range(NSA_N_SEL):
        best = jnp.max(score, axis=0, keepdims=True)
        first = jnp.min(jnp.where(score == best, jf, float(n_sel_blocks)), axis=0, keepdims=True)
        pick = jf == first
        sel = jnp.where(pick, 1.0, sel)
        score = jnp.where(pick, NEG_BIG, score)
    sel_q = jnp.concatenate([sel, jnp.zeros((LANES - n_sel_blocks, qb), F32)], axis=0).T.astype(BF16)

    n_win = NSA_WINDOW // kt + 1
    win_s, win_v = [], []
    for w in range(n_win):
        kb = qi - (n_win - 1) + w
        k0 = pl.multiple_of(jnp.maximum(kb, 0) * kt, kt)
        s = _dot_nt(q, kw_ref[0, pl.ds(k0, kt), :])
        if w == 0:
            s = jnp.where(win_lo, s, NEG_BIG)
        if w == n_win - 1:
            s = jnp.where(causal, s, NEG_BIG)
        else:
            s = s + jnp.where(kb >= 0, 0.0, NEG_BIG)
        win_s.append(s)
        win_v.append(vw_ref[0, pl.ds(k0, kt), :])
    m_win = jnp.broadcast_to(jnp.max(tile_max(win_s[1:], win_s[0]), axis=-1, keepdims=True), (rows_all, kt))
    e_win = jnp.concatenate([jnp.exp(s - m_win).astype(BF16) for s in win_s], axis=1)
    o_win = normalised(_dot(e_win, jnp.concatenate(win_v, axis=0)))

    gate_of = lambda r, branch: jnp.broadcast_to(gates[:, 3 * r + branch:3 * r + branch + 1], (qb, LANES))
    partial = [gate_of(r, 0) * o_cmp[r] + gate_of(r, 2) * o_win[r * qb:(r + 1) * qb] for r in range(rep)]
    gate_sel = jnp.concatenate([gate_of(r, 1) for r in range(rep)], axis=0)

    tpg = NSA_SWEEP_TILES
    kg = tpg * kt

    sel_bias = ((sel_q.astype(F32) - 1.0) * (2.0 ** 100)).astype(BF16)
    q_sel = jnp.concatenate([q, jnp.concatenate([sel_bias] * rep, axis=0)], axis=1)
    n_past = qi // tpg
    diag = qi - n_past * tpg

    def sweep(n_before):
        def run():
            m_run, acc = None, None
            for gi in range(n_before + 1):
                tiles = lane_tiles(_dot_nt(q_sel, ks_ref[0, gi * kg:(gi + 1) * kg, :]))
                if gi == n_before:
                    shifts = [jnp.where(j < diag, -kt, jnp.where(j == diag, 0, kt)) for j in range(tpg)]
                    tiles = [jnp.where(row_l >= lane_k + sh, tile, NEG_BIG) for sh, tile in zip(shifts, tiles)]
                m_grp = jnp.broadcast_to(jnp.max(tile_max(tiles[1:], tiles[0]), axis=-1, keepdims=True), (rows_all, kt))
                m_new = m_grp if m_run is None else jnp.maximum(m_run, m_grp)
                e = jnp.concatenate([jnp.exp(tile - m_new).astype(BF16) for tile in tiles], axis=1)
                pv = _dot(e, vs_ref[0, gi * kg:(gi + 1) * kg, :])
                acc = pv if acc is None else jnp.exp(m_run - m_new) * acc + pv
                m_run = m_new
            acc_scr[...] = gate_sel * normalised(acc)
        return run

    lax.switch(n_past, [sweep(n) for n in range(ks_ref.shape[1] // kg)])
    mixed = [partial[r] + acc_scr[r * qb:(r + 1) * qb, :] for r in range(rep)]
    o_ref[0] = jnp.concatenate([jnp.where(lane_b < dh, mixed[r], pltpu.roll(mixed[r + 1], dh, 1))
                                for r in range(0, rep, 2)], axis=1).astype(o_ref.dtype)


def _nsa_attention(p, kcmp, vcmp, ks, vs, kw, vw, slopes):
    bsz, t, _ = p.shape
    g, rep, qb, dh = NSA_GROUPS, NSA_REP, NSA_Q_BLOCK, NSA_DH
    whole = lambda a: pl.BlockSpec((1, a.shape[1], a.shape[2] // g), lambda b, gi, i: (b, 0, gi))
    gate_col = p.shape[-1] // LANES - 1
    return pl.pallas_call(
        _nsa_attn_kernel,
        grid=(bsz, g, t // qb),
        in_specs=[
            pl.BlockSpec((1, qb, rep * dh), lambda b, gi, i: (b, i, gi)),
            whole(kcmp), whole(vcmp), whole(ks), whole(vs), whole(kw), whole(vw),
            pl.BlockSpec((1, qb, LANES), lambda b, gi, i: (b, i, gate_col)),
            pl.BlockSpec((1, 8, LANES), lambda b, gi, i: (gi, 0, 0)),
        ],
        out_specs=pl.BlockSpec((1, qb, rep * dh), lambda b, gi, i: (b, i, gi)),
        out_shape=jax.ShapeDtypeStruct((bsz, t, g * rep * dh), BF16),
        scratch_shapes=[pltpu.VMEM((rep * qb, LANES), F32)],
        compiler_params=_cparams(("parallel", "parallel", "arbitrary")),
        name="nsa_attention",
    )(p, kcmp, vcmp, ks, vs, kw, vw, p, slopes)


def _even_mixer(x, norm_g, sc, sh, w_in, shift_mu, a_up, a_b, gla_g, w0, w2, a0, a2, g2, k_k, k_a, r_k, gn_w, gn_b):
    d = x.shape[-1]
    qk, vw, w = GLA_HEADS * GLA_DK, GLA_HEADS * GLA_DV, RWKV_HEADS * RWKV_N
    gla_cols = 2 * qk + 2 * vw + GLA_LOWRANK
    wg, wr = w_in[:, :gla_cols], w_in[:, gla_cols:]
    o_r, o_wl, o_k, o_v, o_al, o_gl = np.cumsum([0, w, RWKV_W_LORA, w, w, RWKV_A_LORA]).tolist()
    pad = lambda a, n: jnp.pad(a, ((0, 0), (0, n - a.shape[1])))
    w_perm = jnp.concatenate([
        wg[:, :2 * qk + 2 * vw],
        wr[:, o_r:o_r + w], wr[:, o_k:o_k + w], wr[:, o_v:o_v + w],
        pad(wg[:, 2 * qk + 2 * vw:], LANES),
        wr[:, o_wl:o_wl + RWKV_W_LORA], wr[:, o_al:o_al + RWKV_A_LORA],
        wr[:, o_gl:o_gl + RWKV_G_LORA]], axis=1).astype(BF16)
    p = _norm_proj(x, norm_g, sc, sh, w_perm)
    a_up_pad = jnp.zeros((LANES, qk), F32).at[:GLA_LOWRANK].set(a_up)
    o_gla = _gla(p, a_up_pad, a_b, gla_g)
    mu = shift_mu
    mus = [mu[o_r:o_r + w], mu[o_k:o_k + w], mu[o_v:o_v + w],
           jnp.concatenate([mu[o_wl:o_wl + RWKV_W_LORA], mu[o_al:o_al + RWKV_A_LORA]]), mu[o_gl:o_gl + RWKV_G_LORA]]
    w2p = jnp.zeros((LANES, w), F32).at[:RWKV_W_LORA].set(w2)
    a2p = jnp.zeros((LANES, w), F32).at[RWKV_W_LORA:RWKV_W_LORA + RWKV_A_LORA].set(a2)
    o_rw = _rwkv(p, mus, w0, w2p, a0, a2p, g2, k_k, k_a, r_k.reshape(-1), gn_w, gn_b)
    return [o_gla, o_rw]


def _nsa_mixer(x, norm_g, sc, sh, w_in, pe_k, w1_k, w2_k, pe_v, w1_v, w2_v):
    bsz, t, d = x.shape
    g, dh, heads = NSA_GROUPS, NSA_DH, NSA_HEADS
    n_cols = w_in.shape[1]
    n_pad = -(-n_cols // (3 * LANES)) * (3 * LANES)
    w_pad = jnp.pad(w_in, ((0, 0), (0, n_pad - n_cols))).astype(BF16)
    kv = g * dh
    off = heads * dh
    p, ks, vs, kw, vw = _norm_proj(x, norm_g, sc, sh, w_pad, nsa_kv_start=off + 2 * kv)
    seg = lambda i: p[..., off + i * kv: off + (i + 1) * kv]
    kcmp, vcmp = _nsa_compress(seg(0), seg(1), pe_k, w1_k, w2_k, pe_v, w1_v, w2_v)
    slopes = 2.0 ** (-8.0 * jnp.arange(1, heads + 1, dtype=F32) / heads)
    slopes = jnp.broadcast_to(jnp.pad(slopes.reshape(g, NSA_REP), ((0, 0), (0, 8 - NSA_REP)))[:, :, None], (g, 8, LANES))

    def per_group(a, *cols):
        rows = a.shape[1]
        const = [jnp.broadcast_to(c.astype(BF16)[None], (bsz, rows, c.shape[-1])) for c in cols]
        const.append(jnp.zeros((bsz, rows, LANES - dh - sum(c.shape[-1] for c in cols)), BF16))
        parts = []
        for gi in range(g):
            parts += [a[..., gi * dh:(gi + 1) * dh].astype(BF16)] + const
        return jnp.concatenate(parts, axis=-1)

    cmp_end = jnp.arange(kcmp.shape[1]) * NSA_CMP_STRIDE + NSA_CMP_LEN - 1
    cmp_end_cols = jnp.tile(jnp.stack([cmp_end // LANES * LANES, cmp_end % LANES], axis=-1), (1, 3))
    return [_nsa_attention(p, per_group(kcmp, cmp_end_cols), per_group(vcmp), ks, vs, kw, vw, slopes)]


def kernel(x, c, ada_w, ada_b, norm1_g, norm2_g, ffn_w_up, ffn_conv_w, ffn_conv_b, ffn_w_down, ev_w_in, ev_shift_mu, gla_a_up, gla_a_b, gla_norm_g, rw_w0, rw_w2, rw_a0, rw_a2, rw_g2, rw_k_k, rw_k_a, rw_r_k, rw_gn_w, rw_gn_b, ev_w_out, od_w_in, cmp_pe_k, cmp_w1_k, cmp_w2_k, cmp_pe_v, cmp_w1_v, cmp_w2_v, od_w_out, final_norm_g):
    bsz, t, d = x.shape
    depth = ada_w.shape[0]
    mod = _ada_mod(c, ada_w, ada_b)
    for layer in range(depth):
        sh1, sc1, g1, sh2, sc2, g2 = (mod[layer, :, i * d:(i + 1) * d].reshape(bsz, 1, d) for i in range(6))
        i = layer // 2
        if layer % 2 == 0:
            mix = _even_mixer(x, norm1_g[layer], sc1, sh1, ev_w_in[i], ev_shift_mu[i], gla_a_up[i], gla_a_b[i],
                              gla_norm_g[i], rw_w0[i], rw_w2[i], rw_a0[i], rw_a2[i], rw_g2[i], rw_k_k[i], rw_k_a[i],
                              rw_r_k[i], rw_gn_w[i], rw_gn_b[i])
            w_out = ev_w_out[i]
        else:
            mix = _nsa_mixer(x, norm1_g[layer], sc1, sh1, od_w_in[i], cmp_pe_k[i], cmp_w1_k[i], cmp_w2_k[i],
                             cmp_pe_v[i], cmp_w1_v[i], cmp_w2_v[i])
            w_out = od_w_out[i]
        x = _out_proj(mix, w_out.astype(BF16), x, g1)
        x = _conv_ffn(x, norm2_g[layer], sc2, sh2, g2, ffn_w_up[layer].astype(BF16), ffn_conv_w[layer],
                      ffn_conv_b[layer], ffn_w_down[layer].astype(BF16), final_norm_g,
                      final_norm=(layer == depth - 1))
    return x
```

```python
import functools

import numpy as np
import jax
import jax.numpy as jnp
from jax import lax
from jax.experimental import pallas as pl
from jax.experimental.pallas import tpu as pltpu

F32 = jnp.float32
BF16 = jnp.bfloat16

D_MODEL = 1024
NORM_EPS = 1e-6
GLA_HEADS, GLA_DK, GLA_DV, GLA_LOWRANK, GLA_GATE_NORM, GLA_CHUNK = 4, 64, 128, 16, 16.0, 16
RWKV_HEADS, RWKV_N, RWKV_GN_EPS = 8, 64, 64e-5
RWKV_W_LORA, RWKV_A_LORA, RWKV_G_LORA = 64, 64, 128
RWKV_CHUNK = 64
NSA_HEADS, NSA_GROUPS, NSA_DH = 16, 4, 64
NSA_REP = NSA_HEADS // NSA_GROUPS
NSA_CMP_LEN, NSA_CMP_STRIDE, NSA_CMP_HIDDEN = 32, 16, 64
NSA_SEL_LEN, NSA_N_SEL, NSA_N_LOCAL, NSA_WINDOW, NSA_Q_BLOCK, NSA_FORCE = 64, 8, 2, 512, 128, 100.0
NSA_SWEEP_TILES = 4
NSA_BLOCKS_PER_STEP = 2
FFN_HIDDEN = 2816

LANES = 128
VMEM_LIMIT = 56 * 1024 * 1024
NEG_BIG = -1e30


def _cparams(sem):
    return pltpu.CompilerParams(dimension_semantics=sem, vmem_limit_bytes=VMEM_LIMIT)


def _dot(a, b):
    return jnp.dot(a, b, preferred_element_type=F32)


def _dot_nt(a, b):
    return lax.dot_general(a, b, (((1,), (1,)), ((), ())), preferred_element_type=F32)


def _mm(a, b):
    return _dot(a.astype(BF16), b.astype(BF16))


def _mm_nt(a, b):
    return _dot_nt(a.astype(BF16), b.astype(BF16))


def _split3(a):
    a1 = a.astype(BF16)
    r1 = a - a1.astype(F32)
    a2 = r1.astype(BF16)
    a3 = (r1 - a2.astype(F32)).astype(BF16)
    return a1, a2, a3


def _mm_sel(a, b01, terms=3):
    b = b01.astype(BF16)
    out = None
    for part in _split3(a)[:terms]:
        out = _dot(part, b) if out is None else out + _dot(part, b)
    return out


def _sel_mm(a01, b):
    b1, b2, b3 = _split3(b)
    a = a01.astype(BF16)
    return _dot(a, b1) + _dot(a, b2) + _dot(a, b3)


def _mm3(a, b):
    a1 = a.astype(BF16)
    a2 = (a - a1.astype(F32)).astype(BF16)
    b1 = b.astype(BF16)
    b2 = (b - b1.astype(F32)).astype(BF16)
    return _dot(a1, b1) + _dot(a1, b2) + _dot(a2, b1)


def _mm3_nt(a, b):
    a1 = a.astype(BF16)
    a2 = (a - a1.astype(F32)).astype(BF16)
    b1 = b.astype(BF16)
    b2 = (b - b1.astype(F32)).astype(BF16)
    return _dot_nt(a1, b1) + _dot_nt(a1, b2) + _dot_nt(a2, b1)


def _iota(shape, dim):
    return lax.broadcasted_iota(jnp.int32, shape, dim)


def _sigmoid(x):
    return 1.0 / (1.0 + jnp.exp(-x))


def _softplus(x):
    return jnp.maximum(x, 0.0) + jnp.log(1.0 + jnp.exp(-jnp.abs(x)))


def _gelu_tanh(x):
    return x * (0.5 * (1.0 + jnp.tanh(0.7978845608028654 * (x + 0.044715 * (x * x * x)))))


def _rms(x, eps):
    return x * lax.rsqrt(jnp.mean(x * x, axis=-1, keepdims=True) + eps)


def _mod_kernel(c_ref, w_ref, b_ref, o_ref):
    c = c_ref[...]
    cond = c * _sigmoid(c)
    o_ref[0] = _mm3(cond, w_ref[0]) + b_ref[0]


def _ada_mod(c, ada_w, ada_b):
    depth, d, n = ada_w.shape
    bsz = c.shape[0]
    rows = 8
    c8 = jnp.zeros((rows, d), F32).at[:bsz].set(c)
    tn = 1536
    out = pl.pallas_call(
        _mod_kernel,
        grid=(depth, n // tn),
        in_specs=[
            pl.BlockSpec((rows, d), lambda l, j: (0, 0)),
            pl.BlockSpec((1, d, tn), lambda l, j: (l, 0, j)),
            pl.BlockSpec((1, 1, tn), lambda l, j: (l, 0, j)),
        ],
        out_specs=pl.BlockSpec((1, rows, tn), lambda l, j: (l, 0, j)),
        out_shape=jax.ShapeDtypeStruct((depth, rows, n), F32),
        compiler_params=_cparams(("parallel", "parallel")),
        name="ada_mod",
    )(c8, ada_w, ada_b.reshape(depth, 1, n))
    return out[:, :bsz]


def _norm_proj_kernel(x_ref, g_ref, sc_ref, sh_ref, w_ref, o_ref, *kv_refs, nsa_kv_start):
    hn = _rms(x_ref[0], NORM_EPS) * g_ref[...]
    hn = (hn * (1.0 + sc_ref[0]) + sh_ref[0]).astype(BF16)
    out = _dot(hn, w_ref[...])
    o_ref[0] = out
    if nsa_kv_start is None:
        return
    tm, dh, kv = out.shape[0], NSA_DH, NSA_GROUPS * NSA_DH
    lane = _iota((tm, LANES), 1)
    pos = pl.program_id(1) * tm + _iota((tm, LANES), 0)
    pos_hi, pos_lo = (pos // LANES * LANES).astype(F32), (pos % LANES).astype(F32)
    in_pos = (lane >= dh) & (lane < dh + 6)
    pos_cols = jnp.where(in_pos & ((lane - dh) % 2 == 0), pos_hi, jnp.where(in_pos, pos_lo, 0.0))
    ones_col = jnp.where(lane == dh, 1.0, 0.0)
    block_onehot = jnp.where(lane == pos // NSA_SEL_LEN, 1.0, 0.0)

    def group_tile(seg, gi):
        c0 = nsa_kv_start + seg * kv + (gi // 2) * LANES
        tile = out[:, c0:c0 + LANES]
        return pltpu.roll(tile, dh, 1) if gi % 2 else tile

    def layout(seg, const, extra=None):
        tiles = []
        for gi in range(NSA_GROUPS):
            tiles.append(jnp.where(lane < dh, group_tile(seg, gi), const))
            if extra is not None:
                tiles.append(extra)
        return jnp.concatenate(tiles, axis=1).astype(BF16)

    ks_ref, vs_ref, kw_ref, vw_ref = kv_refs
    ks_ref[0] = layout(0, pos_cols, block_onehot)
    vs_ref[0] = layout(1, ones_col)
    kw_ref[0] = layout(2, pos_cols)
    vw_ref[0] = layout(3, ones_col)


def _norm_proj(x, g, sc, sh, w, *, tm=512, nsa_kv_start=None):
    bsz, t, d = x.shape
    n = w.shape[1]
    out_specs = [pl.BlockSpec((1, tm, n), lambda b, i: (b, i, 0))]
    out_shape = [jax.ShapeDtypeStruct((bsz, t, n), F32)]
    if nsa_kv_start is not None:
        for width in (2 * LANES, LANES, LANES, LANES):
            out_specs.append(pl.BlockSpec((1, tm, NSA_GROUPS * width), lambda b, i: (b, i, 0)))
            out_shape.append(jax.ShapeDtypeStruct((bsz, t, NSA_GROUPS * width), BF16))
    outs = pl.pallas_call(
        functools.partial(_norm_proj_kernel, nsa_kv_start=nsa_kv_start),
        grid=(bsz, t // tm),
        in_specs=[
            pl.BlockSpec((1, tm, d), lambda b, i: (b, i, 0)),
            pl.BlockSpec((1, d), lambda b, i: (0, 0)),
            pl.BlockSpec((1, 1, d), lambda b, i: (b, 0, 0)),
            pl.BlockSpec((1, 1, d), lambda b, i: (b, 0, 0)),
            pl.BlockSpec((d, n), lambda b, i: (0, 0)),
        ],
        out_specs=out_specs,
        out_shape=out_shape,
        compiler_params=_cparams(("parallel", "parallel")),
        name="norm_proj",
    )(x, g.reshape(1, d), sc, sh, w)
    return outs[0] if nsa_kv_start is None else outs


def _out_proj_kernel(*refs):
    *mw_refs, x_ref, gate_ref, o_ref = refs
    n = len(mw_refs) // 2
    proj = _mm(mw_refs[0][0], mw_refs[n][...])
    for m_ref, w_ref in zip(mw_refs[1:n], mw_refs[n + 1:]):
        proj = proj + _mm(m_ref[0], w_ref[...])
    o_ref[0] = x_ref[0] + gate_ref[0] * proj


def _out_proj(mixes, w, x, gate, *, tm=512):
    bsz, t, d = x.shape
    widths = [m.shape[-1] for m in mixes]
    offs = np.cumsum([0] + widths).tolist()
    ws = [w[o:o + k] for o, k in zip(offs, widths)]
    return pl.pallas_call(
        _out_proj_kernel,
        grid=(bsz, t // tm),
        in_specs=[pl.BlockSpec((1, tm, k), lambda b, i: (b, i, 0)) for k in widths]
        + [pl.BlockSpec((k, d), lambda b, i: (0, 0)) for k in widths]
        + [pl.BlockSpec((1, tm, d), lambda b, i: (b, i, 0)), pl.BlockSpec((1, 1, d), lambda b, i: (b, 0, 0))],
        out_specs=pl.BlockSpec((1, tm, d), lambda b, i: (b, i, 0)),
        out_shape=jax.ShapeDtypeStruct((bsz, t, d), F32),
        compiler_params=_cparams(("parallel", "parallel")),
        name="out_proj",
    )(*mixes, *ws, x, gate)


def _ffn_kernel(x_ref, g_ref, sc_ref, sh_ref, gate_ref, wu_ref, wv_ref, cw_ref, cb_ref, wd_ref, fg_ref,
                o_ref, hn_ref, acc_ref, halo_ref, *, final_norm):
    ti = pl.program_id(1)
    fj = pl.program_id(2)
    tm, fk = acc_ref.shape[0], wu_ref.shape[1]

    @pl.when(fj == 0)
    def _():
        hn = _rms(x_ref[0], NORM_EPS) * g_ref[...]
        hn_ref[...] = (hn * (1.0 + sc_ref[0]) + sh_ref[0]).astype(BF16)
        acc_ref[...] = jnp.zeros_like(acc_ref)

    @pl.when(ti == 0)
    def _():
        halo_ref[fj] = jnp.zeros((8, fk), F32)

    hn = hn_ref[...]
    u = _dot(hn, wu_ref[...])
    v = _dot(hn, wv_ref[...])
    prev = halo_ref[fj]
    row = _iota((tm, fk), 0)
    u1 = jnp.where(row == 0, prev[7:8], pltpu.roll(u, 1, 0))
    u2 = jnp.where(row == 0, prev[6:7], jnp.where(row == 1, prev[7:8], pltpu.roll(u, 2, 0)))
    halo_ref[fj] = u[tm - 8:tm]
    cw = cw_ref[...]
    uc = cw[0:1] * u2 + cw[1:2] * u1 + cw[2:3] * u + cb_ref[...]
    h = _gelu_tanh(uc) * v
    acc_ref[...] += _dot(h.astype(BF16), wd_ref[...])

    @pl.when(fj == pl.num_programs(2) - 1)
    def _():
        y = x_ref[0] + gate_ref[0] * acc_ref[...]
        if final_norm:
            y = _rms(y, NORM_EPS) * fg_ref[...]
        o_ref[0] = y


def _conv_ffn(x, g, sc, sh, gate, w_up, conv_w, conv_b, w_down, final_g, *, final_norm, tm=512, fk=2816):
    bsz, t, d = x.shape
    f = w_down.shape[0]
    nf = f // fk
    kern = functools.partial(_ffn_kernel, final_norm=final_norm)
    return pl.pallas_call(
        kern,
        grid=(bsz, t // tm, nf),
        in_specs=[
            pl.BlockSpec((1, tm, d), lambda b, i, j: (b, i, 0)),
            pl.BlockSpec((1, d), lambda b, i, j: (0, 0)),
            pl.BlockSpec((1, 1, d), lambda b, i, j: (b, 0, 0)),
            pl.BlockSpec((1, 1, d), lambda b, i, j: (b, 0, 0)),
            pl.BlockSpec((1, 1, d), lambda b, i, j: (b, 0, 0)),
            pl.BlockSpec((d, fk), lambda b, i, j: (0, j)),
            pl.BlockSpec((d, fk), lambda b, i, j: (0, j + nf)),
            pl.BlockSpec((3, fk), lambda b, i, j: (0, j)),
            pl.BlockSpec((1, fk), lambda b, i, j: (0, j)),
            pl.BlockSpec((fk, d), lambda b, i, j: (j, 0)),
            pl.BlockSpec((1, d), lambda b, i, j: (0, 0)),
        ],
        out_specs=pl.BlockSpec((1, tm, d), lambda b, i, j: (b, i, 0)),
        out_shape=jax.ShapeDtypeStruct((bsz, t, d), F32),
        scratch_shapes=[pltpu.VMEM((tm, d), BF16), pltpu.VMEM((tm, d), F32), pltpu.VMEM((nf, 8, fk), F32)],
        compiler_params=_cparams(("parallel", "arbitrary", "arbitrary")),
        name="conv_ffn",
    )(x, g.reshape(1, d), sc, sh, gate, w_up, w_up, conv_w, conv_b.reshape(1, f), w_down, final_g.reshape(1, d))


def _gla_kernel(q_ref, k_ref, v_ref, og_ref, lr_ref, aup_ref, ab_ref, gg_ref, o_ref,
                st_ref, b_scr, bend_scr, q_scr, k_scr, kd_scr, qg_scr, vt_scr, o_scr):
    tb = q_ref.shape[1]
    cs = GLA_CHUNK
    n_chunks = tb // cs

    @pl.when(pl.program_id(1) == 0)
    def _():
        st_ref[...] = jnp.zeros_like(st_ref)

    z = _mm3(lr_ref[0], aup_ref[...]) + ab_ref[...]
    la = -_softplus(-z) * (1.0 / GLA_GATE_NORM)
    rr, cc = _iota((tb, tb), 0), _iota((tb, tb), 1)
    same = (rr // cs) == (cc // cs)
    b = _sel_mm(jnp.where(same & (cc <= rr), 1.0, 0.0), la)
    bend = _sel_mm(jnp.where(same, 1.0, 0.0), la)
    q = q_ref[0] * (GLA_DK ** -0.5)
    k = k_ref[0]
    b_scr[...] = b
    bend_scr[...] = bend
    q_scr[...] = q
    k_scr[...] = k
    kd_scr[...] = (k * jnp.exp(bend - b)).astype(BF16)
    qg_scr[...] = q * jnp.exp(b)
    v_all = v_ref[0]
    for h in range(GLA_HEADS):
        vt_scr[h] = v_all[:, h * GLA_DV:(h + 1) * GLA_DV].T.astype(BF16)

    lane_c = _iota((cs, LANES), 1)
    row_c = _iota((cs, LANES), 0)
    lane_t = _iota((tb, LANES), 1)
    head_rows = _iota((LANES, LANES), 0) // GLA_DK

    def chunk(c, carry):
        r0 = c * cs
        tmask = (lane_t // cs) == c
        for p in range(GLA_HEADS // 2):
            lanes = slice(p * LANES, (p + 1) * LANES)
            bc = b_scr[pl.ds(r0, cs), lanes]
            qc = q_scr[pl.ds(r0, cs), lanes]
            kc = k_scr[pl.ds(r0, cs), lanes]
            blocks = []
            for s in range(cs):
                m = row_c >= s
                rel = jnp.where(m, bc - bc[s:s + 1], 0.0)
                blocks.append(jnp.where(m, qc * kc[s:s + 1] * jnp.exp(rel), 0.0))
            a_st = jnp.concatenate(blocks, axis=0)
            qgc = qg_scr[pl.ds(r0, cs), lanes]
            lhs = jnp.concatenate([jnp.where(lane_c < GLA_DK, qgc, 0.0),
                                   jnp.where(lane_c >= GLA_DK, qgc, 0.0)], axis=0)
            st = st_ref[p]
            o_inter = _mm_nt(lhs, st)
            for hh in range(2):
                h = 2 * p + hh
                zsum = _mm_sel(a_st, jnp.where(head_rows == hh, 1.0, 0.0), terms=1)
                vc = v_ref[0, pl.ds(r0, cs), h * GLA_DV:(h + 1) * GLA_DV]
                o_h = o_inter[hh * cs:(hh + 1) * cs]
                for s in range(cs):
                    o_h = o_h + zsum[s * cs:(s + 1) * cs] * vc[s:s + 1]
                o_scr[pl.ds(r0, cs), h * GLA_DV:(h + 1) * GLA_DV] = o_h
            lhs_u = jnp.concatenate([jnp.where(tmask, vt_scr[2 * p], 0.0).astype(BF16),
                                     jnp.where(tmask, vt_scr[2 * p + 1], 0.0).astype(BF16)], axis=0)
            upd = _dot(lhs_u, kd_scr[:, lanes])
            upd = jnp.where(lane_t < GLA_DK, upd[:GLA_DV], upd[GLA_DV:])
            decay = jnp.exp(bend_scr[pl.ds(r0, 1), lanes])
            st_ref[p] = st * decay + upd
        return carry

    for c in range(n_chunks):
        chunk(c, 0)

    og = og_ref[0]
    for h in range(GLA_HEADS):
        sl = slice(h * GLA_DV, (h + 1) * GLA_DV)
        gate = og[:, sl]
        o_ref[0, :, sl] = (_rms(o_scr[:, sl], NORM_EPS) * gg_ref[...] * (gate * _sigmoid(gate))).astype(o_ref.dtype)


def _gla(p, a_up_pad, a_b, gla_g, *, tb=128):
    bsz, t, _ = p.shape
    qk = GLA_HEADS * GLA_DK
    vw = GLA_HEADS * GLA_DV
    assert tb == LANES and GLA_DV == LANES
    return pl.pallas_call(
        _gla_kernel,
        grid=(bsz, t // tb),
        in_specs=[
            pl.BlockSpec((1, tb, qk), lambda b, i: (b, i, 0)),
            pl.BlockSpec((1, tb, qk), lambda b, i: (b, i, 1)),
            pl.BlockSpec((1, tb, vw), lambda b, i: (b, i, 1)),
            pl.BlockSpec((1, tb, vw), lambda b, i: (b, i, 2)),
            pl.BlockSpec((1, tb, LANES), lambda b, i: (b, i, 24)),
            pl.BlockSpec((LANES, qk), lambda b, i: (0, 0)),
            pl.BlockSpec((1, qk), lambda b, i: (0, 0)),
            pl.BlockSpec((1, GLA_DV), lambda b, i: (0, 0)),
        ],
        out_specs=pl.BlockSpec((1, tb, vw), lambda b, i: (b, i, 0)),
        out_shape=jax.ShapeDtypeStruct((bsz, t, vw), BF16),
        scratch_shapes=[
            pltpu.VMEM((GLA_HEADS // 2, GLA_DV, LANES), F32),
            pltpu.VMEM((tb, qk), F32), pltpu.VMEM((tb, qk), F32),
            pltpu.VMEM((tb, qk), F32), pltpu.VMEM((tb, qk), F32),
            pltpu.VMEM((tb, qk), BF16), pltpu.VMEM((tb, qk), F32),
            pltpu.VMEM((GLA_HEADS, GLA_DV, tb), BF16),
            pltpu.VMEM((tb, vw), F32),
        ],
        compiler_params=_cparams(("parallel", "arbitrary")),
        name="gla",
    )(p, p, p, p, p, a_up_pad, a_b.reshape(1, qk), gla_g.reshape(1, GLA_DV))


def _rwkv_kernel(r_ref, k_ref, v_ref, wa_ref, gl_ref, mur_ref, muk_ref, muv_ref, muwa_ref, mugl_ref,
                 w0_ref, w2_ref, a0_ref, a2_ref, g2_ref, kkw_ref, ka_ref, rk_ref, gnw_ref, gnb_ref,
                 o_ref,
                 lr_scr, lk_scr, lv_scr, lwa_scr, lgl_scr, s_ref,
                 lw_p, r_p, kk_p, be_p, k2_p, v_p, y_p, pp_all, y0_all, ge_all, g_all, h_all):
    tb = r_ref.shape[1]
    cs = RWKV_CHUNK
    n_chunks = tb // cs
    n_pairs = RWKV_HEADS // 2
    ti = pl.program_id(1)
    lasts = (lr_scr, lk_scr, lv_scr, lwa_scr, lgl_scr)

    @pl.when(ti == 0)
    def _():
        s_ref[...] = jnp.zeros_like(s_ref)
        for ref in lasts:
            ref[...] = jnp.zeros_like(ref)

    def shifted(x_ref, last_ref, mu_ref):
        x = x_ref[0]
        row = _iota(x.shape, 0)
        prev = jnp.where(row == 0, last_ref[7:8], pltpu.roll(x, 1, 0))
        last_ref[...] = x[tb - 8:tb]
        return x + (prev - x) * mu_ref[...]

    r = shifted(r_ref, lr_scr, mur_ref)
    k = shifted(k_ref, lk_scr, muk_ref)
    v = shifted(v_ref, lv_scr, muv_ref)
    wa = shifted(wa_ref, lwa_scr, muwa_ref)
    gl = shifted(gl_ref, lgl_scr, mugl_ref)

    logw = -jnp.exp(-_softplus(-(w0_ref[...] + _mm3(jnp.tanh(wa), w2_ref[...]))) - 0.5)
    a = _sigmoid(a0_ref[...] + _mm3(wa, a2_ref[...]))
    g = _mm(_sigmoid(gl), g2_ref[...])
    seg = jnp.where((_iota((LANES, LANES), 0) // RWKV_N) == (_iota((LANES, LANES), 1) // RWKV_N), 1.0, 0.0)

    def segsum(x):
        return jnp.concatenate([_mm_sel(x[:, i * LANES:(i + 1) * LANES], seg, terms=2) for i in range(n_pairs)], axis=1)

    kk = k * kkw_ref[...]
    kk = kk * lax.rsqrt(jnp.maximum(segsum(kk * kk), 1e-24))
    k2 = k * (1.0 + (a - 1.0) * ka_ref[...])
    beta = kk * a
    for p in range(n_pairs):
        sl = slice(p * LANES, (p + 1) * LANES)
        lw_p[p] = logw[:, sl]
        r_p[p] = r[:, sl]
        kk_p[p] = kk[:, sl]
        be_p[p] = beta[:, sl]
        k2_p[p] = k2[:, sl]
        v_p[p] = v[:, sl]

    rr, cc = _iota((tb, tb), 0), _iota((tb, tb), 1)
    same = (rr // cs) == (cc // cs)
    tri_incl = same & (cc <= rr)
    tri_strict = same & (cc < rr)
    l_incl = jnp.where(tri_incl, 1.0, 0.0)
    l_all = jnp.where(same, 1.0, 0.0)
    eye = jnp.where(rr == cc, 1.0, 0.0)
    lane = _iota((tb, LANES), 1)
    half = lane < RWKV_N
    same_half = (rr // RWKV_N) == (cc // RWKV_N)
    lane_c = _iota((cs, LANES), 1)
    lane_s = _iota((RWKV_N, LANES), 1)

    def below_left(s):
        return ((rr // (2 * s)) == (cc // (2 * s))) & ((rr % (2 * s)) >= s) & ((cc % (2 * s)) < s)

    pairs = range(n_pairs)
    heads = [(p, hh) for p in pairs for hh in range(2)]
    l_both = jnp.concatenate([l_incl, l_all], axis=0)
    rt, bt, at, vp, ak = {}, {}, {}, {}, {}
    for p in pairs:
        lw = lw_p[p]
        sums = _sel_mm(l_both, lw)
        cum = sums[:tb]
        ge_all[p] = jnp.exp(sums[tb:])
        ig = jnp.exp(-cum)
        rt[p] = r_p[p] * jnp.exp(cum)
        bt[p] = kk_p[p] * jnp.exp(cum - lw)
        at[p] = -(be_p[p] * ig)
        vp[p] = v_p[p]
        ak[p] = jnp.concatenate([at[p], k2_p[p] * ig], axis=0)
    a_ab, a_ak, a_ra, a_rk = {}, {}, {}, {}
    for h in heads:
        p, hh = h
        hm = half if hh == 0 else jnp.logical_not(half)
        s_b = _mm3_nt(jnp.where(hm, bt[p], 0.0), ak[p])
        s_r = _mm_nt(jnp.where(hm, rt[p], 0.0), ak[p])
        a_ab[h] = jnp.where(tri_strict, s_b[:, :tb], 0.0)
        a_ak[h] = jnp.where(tri_strict, s_b[:, tb:], 0.0)
        a_ra[h] = jnp.where(tri_incl, s_r[:, :tb], 0.0)
        a_rk[h] = jnp.where(tri_incl, s_r[:, tb:], 0.0)
    m_inv = {h: eye + jnp.where(below_left(1), a_ab[h], 0.0) for h in heads}
    w1 = {h: _mm3(a_ak[h], vp[h[0]]) for h in heads}
    s = 2
    while s < cs:
        low = {h: _mm3(m_inv[h], jnp.where(below_left(s), a_ab[h], 0.0)) for h in heads}
        m_inv = {h: m_inv[h] + _mm3(low[h], m_inv[h]) for h in heads}
        s *= 2
    mw = {h: _mm3(m_inv[h], jnp.concatenate([w1[h], bt[h[0]]], axis=1)) for h in heads}
    p_h = {h: _mm(a_ra[h], mw[h][:, LANES:]) for h in heads}
    y0_h = {h: _mm(a_ra[h], mw[h][:, :LANES]) + _mm(a_rk[h], vp[h[0]]) for h in heads}
    zeros = jnp.zeros((tb, LANES), F32)
    for p in pairs:
        u0 = jnp.where(half, mw[(p, 0)][:, :LANES], mw[(p, 1)][:, :LANES])
        mb = jnp.where(half, mw[(p, 0)][:, LANES:], mw[(p, 1)][:, LANES:])
        pp_all[p] = rt[p] + jnp.where(half, p_h[(p, 0)], p_h[(p, 1)])
        y0_all[p] = jnp.where(half, y0_h[(p, 0)], y0_h[(p, 1)])
        u0t, vt, mbt = u0.T, vp[p].T, mb.T
        for c in range(n_chunks):
            tmask = (lane // cs) == c
            lhs = jnp.concatenate([
                jnp.concatenate([jnp.where(tmask, u0t, 0.0), jnp.where(tmask, vt, 0.0)], axis=1),
                jnp.concatenate([jnp.where(tmask, mbt, 0.0), zeros], axis=1)], axis=0)
            hg = _mm3(lhs, ak[p])
            h_all[p * n_chunks + c] = jnp.where(lane_s < RWKV_N, hg[:RWKV_N], hg[RWKV_N:tb])
            g_all[p * n_chunks + c] = jnp.where(same_half, hg[tb:], 0.0) + eye

    state = [s_ref[p] for p in range(n_pairs)]
    for c in range(n_chunks):
        rows = slice(c * cs, (c + 1) * cs)
        for p in range(n_pairs):
            sp = state[p]
            pc = pp_all[p, rows, :]
            lhs_y = jnp.concatenate([jnp.where(lane_c < RWKV_N, pc, 0.0), jnp.where(lane_c >= RWKV_N, pc, 0.0)], axis=0)
            yy = _mm_nt(lhs_y, jnp.concatenate([sp, sp], axis=0))
            y_p[p, rows, :] = jnp.where(lane_c < RWKV_N, yy[:cs], yy[cs:]) + y0_all[p, rows, :]
            state[p] = (_mm3(sp, g_all[p * n_chunks + c]) + h_all[p * n_chunks + c]) * ge_all[p, c * cs:c * cs + 1, :]
    for p in range(n_pairs):
        s_ref[p] = state[p]

    y = jnp.concatenate([y_p[p] for p in range(n_pairs)], axis=1)
    mu = segsum(y) * (1.0 / RWKV_N)
    yc = y - mu
    var = segsum(yc * yc) * (1.0 / RWKV_N)
    yn = yc * lax.rsqrt(var + RWKV_GN_EPS) * gnw_ref[...] + gnb_ref[...]
    bonus = segsum(r * k2 * rk_ref[...]) * v
    o_ref[0] = ((yn + bonus) * g).astype(o_ref.dtype)


def _rwkv(p, mus, w0, w2p, a0, a2p, g2, k_k, k_a, r_k, gn_w, gn_b, *, tb=128):
    bsz, t, _ = p.shape
    w = RWKV_HEADS * RWKV_N
    n_pairs = RWKV_HEADS // 2
    assert tb == LANES
    row = lambda a: a.reshape(1, -1)
    full = lambda shape: pl.BlockSpec(shape, lambda b, i: (0,) * len(shape))
    ptile = lambda: pltpu.VMEM((n_pairs, tb, LANES), F32)
    return pl.pallas_call(
        _rwkv_kernel,
        grid=(bsz, t // tb),
        in_specs=[
            pl.BlockSpec((1, tb, w), lambda b, i: (b, i, 3)),
            pl.BlockSpec((1, tb, w), lambda b, i: (b, i, 4)),
            pl.BlockSpec((1, tb, w), lambda b, i: (b, i, 5)),
            pl.BlockSpec((1, tb, LANES), lambda b, i: (b, i, 25)),
            pl.BlockSpec((1, tb, LANES), lambda b, i: (b, i, 26)),
            full((1, w)), full((1, w)), full((1, w)), full((1, LANES)), full((1, LANES)),
            full((1, w)), full((LANES, w)), full((1, w)), full((LANES, w)), full((LANES, w)),
            full((1, w)), full((1, w)), full((1, w)), full((1, w)), full((1, w)),
        ],
        out_specs=pl.BlockSpec((1, tb, w), lambda b, i: (b, i, 0)),
        out_shape=jax.ShapeDtypeStruct((bsz, t, w), BF16),
        scratch_shapes=[
            pltpu.VMEM((8, w), F32), pltpu.VMEM((8, w), F32), pltpu.VMEM((8, w), F32),
            pltpu.VMEM((8, LANES), F32), pltpu.VMEM((8, LANES), F32),
            pltpu.VMEM((n_pairs, RWKV_N, LANES), F32),
            ptile(), ptile(), ptile(), ptile(), ptile(), ptile(), ptile(), ptile(), ptile(), ptile(),
            pltpu.VMEM((n_pairs * (tb // RWKV_CHUNK), LANES, LANES), F32),
            pltpu.VMEM((n_pairs * (tb // RWKV_CHUNK), RWKV_N, LANES), F32),
        ],
        compiler_params=_cparams(("parallel", "arbitrary")),
        name="rwkv7",
    )(p, p, p, p, p, *[row(m) for m in mus], row(w0), w2p, row(a0), a2p, g2,
      row(k_k), row(k_a), row(r_k), row(gn_w), row(gn_b))


def _cmp_kernel(sk_ref, sv_ref, wak_ref, wbk_ref, wav_ref, wbv_ref, pek_ref, pev_ref, w1k_ref, w1v_ref,
                w2k_ref, w2v_ref, ok_ref, ov_ref):
    def one(seg_ref, wa_ref, wb_ref, pe_ref, w1_ref, w2_ref, o_ref):
        seg = seg_ref[0].astype(BF16)
        first = _dot(seg, wa_ref[...])
        second = _dot(seg, wb_ref[...])
        n = first.shape[0]
        pe_term = _mm3(pe_ref[...], w1_ref[...])[0:1]
        hidden = _gelu_tanh(first + pltpu.roll(second, n - 1, 0) + pe_term)
        o_ref[0] = _mm(hidden, w2_ref[...])

    one(sk_ref, wak_ref, wbk_ref, pek_ref, w1k_ref, w2k_ref, ok_ref)
    one(sv_ref, wav_ref, wbv_ref, pev_ref, w1v_ref, w2v_ref, ov_ref)


def _nsa_compress(kc_tok, vc_tok, pe_k, w1_k, w2_k, pe_v, w1_v, w2_v):
    bsz, t, gw = kc_tok.shape
    st, dh, hid, g = NSA_CMP_STRIDE, NSA_DH, NSA_CMP_HIDDEN, NSA_GROUPS
    nseg = t // st
    eye = jnp.eye(g, dtype=F32)

    def expand_w1(w1):
        w = w1.reshape(NSA_CMP_LEN, dh, hid)
        big = jnp.einsum('ldc,gh->lgdhc', w, eye).reshape(NSA_CMP_LEN * g * dh, g * hid)
        half = st * g * dh
        return big[:half].astype(BF16), big[half:].astype(BF16)

    def expand_w2(w2):
        return jnp.einsum('cd,gh->gchd', w2, eye).reshape(g * hid, g * dh).astype(BF16)

    def pe_rows(pe):
        return jnp.zeros((8, NSA_CMP_LEN * dh), F32).at[0].set(pe.reshape(-1))

    wak, wbk = expand_w1(w1_k)
    wav, wbv = expand_w1(w1_v)
    full = lambda shape: pl.BlockSpec(shape, lambda b: (0,) * len(shape))
    seg_spec = pl.BlockSpec((1, nseg, st * gw), lambda b: (b, 0, 0))
    out_spec = pl.BlockSpec((1, nseg, gw), lambda b: (b, 0, 0))
    return pl.pallas_call(
        _cmp_kernel,
        grid=(bsz,),
        in_specs=[seg_spec, seg_spec,
                  full(wak.shape), full(wbk.shape), full(wav.shape), full(wbv.shape),
                  full((8, NSA_CMP_LEN * dh)), full((8, NSA_CMP_LEN * dh)),
                  full((NSA_CMP_LEN * dh, g * hid)), full((NSA_CMP_LEN * dh, g * hid)),
                  full((g * hid, gw)), full((g * hid, gw))],
        out_specs=[out_spec, out_spec],
        out_shape=[jax.ShapeDtypeStruct((bsz, nseg, gw), F32)] * 2,
        compiler_params=_cparams(("parallel",)),
        name="nsa_compress",
    )(kc_tok.reshape(bsz, nseg, st * gw), vc_tok.reshape(bsz, nseg, st * gw),
      wak, wbk, wav, wbv, pe_rows(pe_k), pe_rows(pe_v),
      jnp.tile(w1_k, (1, g)), jnp.tile(w1_v, (1, g)), expand_w2(w2_k), expand_w2(w2_v))


def _nsa_attn_kernel(q_ref, kc_ref, vc_ref, ks_ref, vs_ref, kw_ref, vw_ref, gt_ref, sl_ref, o_ref, acc_scr):
    g = pl.program_id(1)
    qi = pl.program_id(2)
    qb, dh, rep = NSA_Q_BLOCK, NSA_DH, NSA_REP
    n_cmp_pad = kc_ref.shape[1]
    kt = LANES
    nb = q_ref.shape[1] // qb
    rows_all = nb * rep * qb

    lane_q = _iota((1, LANES), 1)
    lane_b = _iota((qb, LANES), 1)
    slope_cols = []
    for r in range(rep):
        s1, s2, s3 = (t.astype(F32) for t in _split3(sl_ref[0, r:r + 1, :]))
        slope_cols.append(jnp.where((lane_q >= dh) & (lane_q < dh + 2), s1,
                                    jnp.where((lane_q >= dh + 2) & (lane_q < dh + 4), s2,
                                              jnp.where((lane_q >= dh + 4) & (lane_q < dh + 6), s3, 0.0))))
    q_rows = []
    for blk in range(nb):
        for r in range(rep):
            two_heads = q_ref[0, blk * qb:(blk + 1) * qb, (r // 2) * LANES:(r // 2 + 1) * LANES]
            if r % 2:
                two_heads = pltpu.roll(two_heads, dh, 1)
            q_rows.append(jnp.where(lane_b < dh, two_heads * (dh ** -0.5), slope_cols[r]))
    q = jnp.concatenate(q_rows, axis=0).astype(BF16)
    t0 = qi * (nb * qb)

    def row_pos(shape):
        rows = _iota(shape, 0)
        return rows // (rep * qb) * qb + rows % qb

    ahead_of_lane = row_pos((rows_all, kt)) - _iota((rows_all, kt), 1)

    def lane_tiles(s):
        return [s[:, j * kt:(j + 1) * kt] for j in range(s.shape[1] // kt)]

    def tile_max(tiles, start):
        m = start
        for tile in tiles:
            m = jnp.maximum(m, tile)
        return m

    def normalised(acc):
        return acc / acc[:, dh:dh + 1]

    gsel = jnp.where(_iota((LANES, LANES), 0) == g * (rep * 3) + _iota((LANES, LANES), 1), 1.0, 0.0)
    gates = _sigmoid(_mm_sel(gt_ref[0], gsel))

    n_idx = _iota((rows_all, n_cmp_pad), 1)
    valid_c = ((t0 + row_pos((rows_all, n_cmp_pad)) >= n_idx * NSA_CMP_STRIDE + NSA_CMP_LEN - 1)
               & (n_idx < n_cmp_pad - 1))
    s_cmp = jnp.where(valid_c, _dot_nt(q, kc_ref[0]), NEG_BIG)
    e_cmp = jnp.where(valid_c, jnp.exp(s_cmp - jnp.max(s_cmp, axis=-1, keepdims=True)), 0.0)
    den = jnp.sum(e_cmp, axis=-1, keepdims=True)
    p_cmp = e_cmp / jnp.where(den > 0.0, den, 1.0)
    o_cmp = _mm(p_cmp, vc_ref[0])
    n_sel_blocks = ks_ref.shape[1] // NSA_SEL_LEN
    on, oj = _iota((n_cmp_pad, LANES), 0), _iota((n_cmp_pad, LANES), 1)
    overlap = jnp.where((on * NSA_CMP_STRIDE <= oj * NSA_SEL_LEN + NSA_SEL_LEN - 1)
                        & (on * NSA_CMP_STRIDE + NSA_CMP_LEN - 1 >= oj * NSA_SEL_LEN)
                        & (oj < n_sel_blocks) & (on < n_cmp_pad - 1), 1.0, 0.0)
    p_groups = []
    for blk in range(nb):
        rows = [slice((blk * rep + r) * qb, (blk * rep + r + 1) * qb) for r in range(rep)]
        p_sum = p_cmp[rows[0]]
        for rs in rows[1:]:
            p_sum = p_sum + p_cmp[rs]
        p_groups.append(p_sum)
    imp = _mm_sel(jnp.concatenate(p_groups, axis=0), overlap)

    nq = nb * qb
    imp_t = imp.T[:n_sel_blocks]
    jj = _iota((n_sel_blocks, nq), 0)
    jf = jj.astype(F32)
    ahead = (t0 + _iota((n_sel_blocks, nq), 1)) // NSA_SEL_LEN - jj
    valid_b = ahead >= 0
    forced = (jj == 0) | (valid_b & (ahead < NSA_N_LOCAL))
    score = jnp.where(valid_b, imp_t + jnp.where(forced, NSA_FORCE, 0.0), -NSA_FORCE)
    sel = jnp.zeros((n_sel_blocks, nq), F32)
    for _ in range(NSA_N_SEL):
        best = jnp.max(score, axis=0, keepdims=True)
        first = jnp.min(jnp.where(score == best, jf, float(n_sel_blocks)), axis=0, keepdims=True)
        pick = jf == first
        sel = jnp.where(pick, 1.0, sel)
        score = jnp.where(pick, NEG_BIG, score)
    sel_q = jnp.concatenate([sel, jnp.zeros((LANES - n_sel_blocks, nq), F32)], axis=0).T.astype(BF16)

    w_tiles = NSA_WINDOW // kt
    win_s, win_v = [], []
    for rel in range(-w_tiles, nb):
        kb = qi * nb + rel
        k0 = pl.multiple_of(jnp.maximum(kb, 0) * kt, kt)
        s = _dot_nt(q, kw_ref[0, pl.ds(k0, kt), :])
        if not all(-w_tiles < rel - blk < 0 for blk in range(nb)):
            dist = ahead_of_lane - rel * kt
            s = jnp.where((dist >= 0) & (dist < NSA_WINDOW), s, NEG_BIG)
        if rel < 0:
            s = s + jnp.where(kb >= 0, 0.0, NEG_BIG)
        win_s.append(s)
        win_v.append(vw_ref[0, pl.ds(k0, kt), :])
    m_win = jnp.broadcast_to(jnp.max(tile_max(win_s[1:], win_s[0]), axis=-1, keepdims=True), (rows_all, kt))
    e_win = jnp.concatenate([jnp.exp(s - m_win).astype(BF16) for s in win_s], axis=1)
    o_win = normalised(_dot(e_win, jnp.concatenate(win_v, axis=0)))

    def gate_of(rb, branch):
        blk, r = divmod(rb, rep)
        col = 3 * r + branch
        return jnp.broadcast_to(gates[blk * qb:(blk + 1) * qb, col:col + 1], (qb, LANES))

    row_blocks = [slice(rb * qb, (rb + 1) * qb) for rb in range(nb * rep)]
    partial = [gate_of(rb, 0) * o_cmp[rs] + gate_of(rb, 2) * o_win[rs] for rb, rs in enumerate(row_blocks)]
    gate_sel = jnp.concatenate([gate_of(rb, 1) for rb in range(nb * rep)], axis=0)

    tpg = NSA_SWEEP_TILES
    kg = tpg * kt
    assert tpg % nb == 0

    sel_bias = ((sel_q.astype(F32) - 1.0) * (2.0 ** 100)).astype(BF16)
    q_sel = jnp.concatenate([q, jnp.concatenate([sel_bias[blk * qb:(blk + 1) * qb] for blk in range(nb)
                                                 for _ in range(rep)], axis=0)], axis=1)
    n_past = (qi * nb) // tpg
    first_off = (n_past * tpg - qi * nb) * kt

    def sweep(n_before):
        def run():
            m_run, acc = None, None
            for gi in range(n_before + 1):
                tiles = lane_tiles(_dot_nt(q_sel, ks_ref[0, gi * kg:(gi + 1) * kg, :]))
                if gi == n_before:
                    tiles = [jnp.where(ahead_of_lane >= first_off + j * kt, tile, NEG_BIG) for j, tile in enumerate(tiles)]
                m_grp = jnp.broadcast_to(jnp.max(tile_max(tiles[1:], tiles[0]), axis=-1, keepdims=True), (rows_all, kt))
                m_new = m_grp if m_run is None else jnp.maximum(m_run, m_grp)
                e = jnp.concatenate([jnp.exp(tile - m_new).astype(BF16) for tile in tiles], axis=1)
                pv = _dot(e, vs_ref[0, gi * kg:(gi + 1) * kg, :])
                acc = pv if acc is None else jnp.exp(m_run - m_new) * acc + pv
                m_run = m_new
            acc_scr[...] = gate_sel * normalised(acc)
        return run

    lax.switch(n_past, [sweep(n) for n in range(ks_ref.shape[1] // kg)])
    mixed = [partial[rb] + acc_scr[rs, :] for rb, rs in enumerate(row_blocks)]
    o_ref[0] = jnp.concatenate(
        [jnp.concatenate([jnp.where(lane_b < dh, mixed[blk * rep + r], pltpu.roll(mixed[blk * rep + r + 1], dh, 1))
                          for r in range(0, rep, 2)], axis=1) for blk in range(nb)], axis=0).astype(o_ref.dtype)


def _nsa_attention(p, kcmp, vcmp, ks, vs, kw, vw, slopes):
    bsz, t, _ = p.shape
    g, rep, dh = NSA_GROUPS, NSA_REP, NSA_DH
    qs = NSA_BLOCKS_PER_STEP * NSA_Q_BLOCK
    whole = lambda a: pl.BlockSpec((1, a.shape[1], a.shape[2] // g), lambda b, gi, i: (b, 0, gi))
    gate_col = p.shape[-1] // LANES - 1
    return pl.pallas_call(
        _nsa_attn_kernel,
        grid=(bsz, g, t // qs),
        in_specs=[
            pl.BlockSpec((1, qs, rep * dh), lambda b, gi, i: (b, i, gi)),
            whole(kcmp), whole(vcmp), whole(ks), whole(vs), whole(kw), whole(vw),
            pl.BlockSpec((1, qs, LANES), lambda b, gi, i: (b, i, gate_col)),
            pl.BlockSpec((1, 8, LANES), lambda b, gi, i: (gi, 0, 0)),
        ],
        out_specs=pl.BlockSpec((1, qs, rep * dh), lambda b, gi, i: (b, i, gi)),
        out_shape=jax.ShapeDtypeStruct((bsz, t, g * rep * dh), BF16),
        scratch_shapes=[pltpu.VMEM((rep * qs, LANES), F32)],
        compiler_params=_cparams(("parallel", "parallel", "arbitrary")),
        name="nsa_attention",
    )(p, kcmp, vcmp, ks, vs, kw, vw, p, slopes)


def _even_mixer(x, norm_g, sc, sh, w_in, shift_mu, a_up, a_b, gla_g, w0, w2, a0, a2, g2, k_k, k_a, r_k, gn_w, gn_b):
    d = x.shape[-1]
    qk, vw, w = GLA_HEADS * GLA_DK, GLA_HEADS * GLA_DV, RWKV_HEADS * RWKV_N
    gla_cols = 2 * qk + 2 * vw + GLA_LOWRANK
    wg, wr = w_in[:, :gla_cols], w_in[:, gla_cols:]
    o_r, o_wl, o_k, o_v, o_al, o_gl = np.cumsum([0, w, RWKV_W_LORA, w, w, RWKV_A_LORA]).tolist()
    pad = lambda a, n: jnp.pad(a, ((0, 0), (0, n - a.shape[1])))
    w_perm = jnp.concatenate([
        wg[:, :2 * qk + 2 * vw],
        wr[:, o_r:o_r + w], wr[:, o_k:o_k + w], wr[:, o_v:o_v + w],
        pad(wg[:, 2 * qk + 2 * vw:], LANES),
        wr[:, o_wl:o_wl + RWKV_W_LORA], wr[:, o_al:o_al + RWKV_A_LORA],
        wr[:, o_gl:o_gl + RWKV_G_LORA]], axis=1).astype(BF16)
    p = _norm_proj(x, norm_g, sc, sh, w_perm)
    a_up_pad = jnp.zeros((LANES, qk), F32).at[:GLA_LOWRANK].set(a_up)
    o_gla = _gla(p, a_up_pad, a_b, gla_g)
    mu = shift_mu
    mus = [mu[o_r:o_r + w], mu[o_k:o_k + w], mu[o_v:o_v + w],
           jnp.concatenate([mu[o_wl:o_wl + RWKV_W_LORA], mu[o_al:o_al + RWKV_A_LORA]]), mu[o_gl:o_gl + RWKV_G_LORA]]
    w2p = jnp.zeros((LANES, w), F32).at[:RWKV_W_LORA].set(w2)
    a2p = jnp.zeros((LANES, w), F32).at[RWKV_W_LORA:RWKV_W_LORA + RWKV_A_LORA].set(a2)
    o_rw = _rwkv(p, mus, w0, w2p, a0, a2p, g2, k_k, k_a, r_k.reshape(-1), gn_w, gn_b)
    return [o_gla, o_rw]


def _nsa_mixer(x, norm_g, sc, sh, w_in, pe_k, w1_k, w2_k, pe_v, w1_v, w2_v):
    bsz, t, d = x.shape
    g, dh, heads = NSA_GROUPS, NSA_DH, NSA_HEADS
    n_cols = w_in.shape[1]
    n_pad = -(-n_cols // (3 * LANES)) * (3 * LANES)
    w_pad = jnp.pad(w_in, ((0, 0), (0, n_pad - n_cols))).astype(BF16)
    kv = g * dh
    off = heads * dh
    p, ks, vs, kw, vw = _norm_proj(x, norm_g, sc, sh, w_pad, nsa_kv_start=off + 2 * kv)
    seg = lambda i: p[..., off + i * kv: off + (i + 1) * kv]
    kcmp, vcmp = _nsa_compress(seg(0), seg(1), pe_k, w1_k, w2_k, pe_v, w1_v, w2_v)
    slopes = 2.0 ** (-8.0 * jnp.arange(1, heads + 1, dtype=F32) / heads)
    slopes = jnp.broadcast_to(jnp.pad(slopes.reshape(g, NSA_REP), ((0, 0), (0, 8 - NSA_REP)))[:, :, None], (g, 8, LANES))

    def per_group(a, *cols):
        rows = a.shape[1]
        const = [jnp.broadcast_to(c.astype(BF16)[None], (bsz, rows, c.shape[-1])) for c in cols]
        const.append(jnp.zeros((bsz, rows, LANES - dh - sum(c.shape[-1] for c in cols)), BF16))
        parts = []
        for gi in range(g):
            parts += [a[..., gi * dh:(gi + 1) * dh].astype(BF16)] + const
        return jnp.concatenate(parts, axis=-1)

    cmp_end = jnp.arange(kcmp.shape[1]) * NSA_CMP_STRIDE + NSA_CMP_LEN - 1
    cmp_end_cols = jnp.tile(jnp.stack([cmp_end // LANES * LANES, cmp_end % LANES], axis=-1), (1, 3))
    return [_nsa_attention(p, per_group(kcmp, cmp_end_cols), per_group(vcmp), ks, vs, kw, vw, slopes)]


def kernel(x, c, ada_w, ada_b, norm1_g, norm2_g, ffn_w_up, ffn_conv_w, ffn_conv_b, ffn_w_down, ev_w_in, ev_shift_mu, gla_a_up, gla_a_b, gla_norm_g, rw_w0, rw_w2, rw_a0, rw_a2, rw_g2, rw_k_k, rw_k_a, rw_r_k, rw_gn_w, rw_gn_b, ev_w_out, od_w_in, cmp_pe_k, cmp_w1_k, cmp_w2_k, cmp_pe_v, cmp_w1_v, cmp_w2_v, od_w_out, final_norm_g):
    bsz, t, d = x.shape
    depth = ada_w.shape[0]
    mod = _ada_mod(c, ada_w, ada_b)
    for layer in range(depth):
        sh1, sc1, g1, sh2, sc2, g2 = (mod[layer, :, i * d:(i + 1) * d].reshape(bsz, 1, d) for i in range(6))
        i = layer // 2
        if layer % 2 == 0:
            mix = _even_mixer(x, norm1_g[layer], sc1, sh1, ev_w_in[i], ev_shift_mu[i], gla_a_up[i], gla_a_b[i],
                              gla_norm_g[i], rw_w0[i], rw_w2[i], rw_a0[i], rw_a2[i], rw_g2[i], rw_k_k[i], rw_k_a[i],
                              rw_r_k[i], rw_gn_w[i], rw_gn_b[i])
            w_out = ev_w_out[i]
        else:
            mix = _nsa_mixer(x, norm1_g[layer], sc1, sh1, od_w_in[i], cmp_pe_k[i], cmp_w1_k[i], cmp_w2_k[i],
                             cmp_pe_v[i], cmp_w1_v[i], cmp_w2_v[i])
            w_out = od_w_out[i]
        x = _out_proj(mix, w_out.astype(BF16), x, g1)
        x = _conv_ffn(x, norm2_g[layer], sc2, sh2, g2, ffn_w_up[layer].astype(BF16), ffn_conv_w[layer],
                      ffn_conv_b[layer], ffn_w_down[layer].astype(BF16), final_norm_g,
                      final_norm=(layer == depth - 1))
    return x
```

```python
import functools

import numpy as np
import jax
import jax.numpy as jnp
from jax import lax
from jax.experimental import pallas as pl
from jax.experimental.pallas import tpu as pltpu

F32 = jnp.float32
BF16 = jnp.bfloat16

D_MODEL = 1024
NORM_EPS = 1e-6
GLA_HEADS, GLA_DK, GLA_DV, GLA_LOWRANK, GLA_GATE_NORM, GLA_CHUNK = 4, 64, 128, 16, 16.0, 16
RWKV_HEADS, RWKV_N, RWKV_GN_EPS = 8, 64, 64e-5
RWKV_W_LORA, RWKV_A_LORA, RWKV_G_LORA = 64, 64, 128
RWKV_CHUNK = 64
NSA_HEADS, NSA_GROUPS, NSA_DH = 16, 4, 64
NSA_REP = NSA_HEADS // NSA_GROUPS
NSA_CMP_LEN, NSA_CMP_STRIDE, NSA_CMP_HIDDEN = 32, 16, 64
NSA_SEL_LEN, NSA_N_SEL, NSA_N_LOCAL, NSA_WINDOW, NSA_Q_BLOCK, NSA_FORCE = 64, 8, 2, 512, 128, 100.0
NSA_SWEEP_TILES = 4
NSA_BLOCKS_PER_STEP = 2
FFN_HIDDEN = 2816

LANES = 128
VMEM_LIMIT = 56 * 1024 * 1024
NEG_BIG = -1e30


def _cparams(sem):
    return pltpu.CompilerParams(dimension_semantics=sem, vmem_limit_bytes=VMEM_LIMIT)


def _dot(a, b):
    return jnp.dot(a, b, preferred_element_type=F32)


def _dot_nt(a, b):
    return lax.dot_general(a, b, (((1,), (1,)), ((), ())), preferred_element_type=F32)


def _mm(a, b):
    return _dot(a.astype(BF16), b.astype(BF16))


def _mm_nt(a, b):
    return _dot_nt(a.astype(BF16), b.astype(BF16))


def _split3(a):
    a1 = a.astype(BF16)
    r1 = a - a1.astype(F32)
    a2 = r1.astype(BF16)
    a3 = (r1 - a2.astype(F32)).astype(BF16)
    return a1, a2, a3


def _mm_sel(a, b01, terms=3):
    b = b01.astype(BF16)
    out = None
    for part in _split3(a)[:terms]:
        out = _dot(part, b) if out is None else out + _dot(part, b)
    return out


def _sel_mm(a01, b):
    b1, b2, b3 = _split3(b)
    a = a01.astype(BF16)
    return _dot(a, b1) + _dot(a, b2) + _dot(a, b3)


def _mm3(a, b):
    a1 = a.astype(BF16)
    a2 = (a - a1.astype(F32)).astype(BF16)
    b1 = b.astype(BF16)
    b2 = (b - b1.astype(F32)).astype(BF16)
    return _dot(a1, b1) + _dot(a1, b2) + _dot(a2, b1)


def _mm3_nt(a, b):
    a1 = a.astype(BF16)
    a2 = (a - a1.astype(F32)).astype(BF16)
    b1 = b.astype(BF16)
    b2 = (b - b1.astype(F32)).astype(BF16)
    return _dot_nt(a1, b1) + _dot_nt(a1, b2) + _dot_nt(a2, b1)


def _iota(shape, dim):
    return lax.broadcasted_iota(jnp.int32, shape, dim)


def _sigmoid(x):
    return 1.0 / (1.0 + jnp.exp(-x))


def _softplus(x):
    return jnp.maximum(x, 0.0) + jnp.log(1.0 + jnp.exp(-jnp.abs(x)))


def _gelu_tanh(x):
    return x * (0.5 * (1.0 + jnp.tanh(0.7978845608028654 * (x + 0.044715 * (x * x * x)))))


def _rms(x, eps):
    return x * lax.rsqrt(jnp.mean(x * x, axis=-1, keepdims=True) + eps)


def _mod_kernel(c_ref, w_ref, b_ref, o_ref):
    c = c_ref[...]
    cond = c * _sigmoid(c)
    o_ref[0] = _mm3(cond, w_ref[0]) + b_ref[0]


def _ada_mod(c, ada_w, ada_b):
    depth, d, n = ada_w.shape
    bsz = c.shape[0]
    rows = 8
    c8 = jnp.zeros((rows, d), F32).at[:bsz].set(c)
    tn = 1536
    out = pl.pallas_call(
        _mod_kernel,
        grid=(depth, n // tn),
        in_specs=[
            pl.BlockSpec((rows, d), lambda l, j: (0, 0)),
            pl.BlockSpec((1, d, tn), lambda l, j: (l, 0, j)),
            pl.BlockSpec((1, 1, tn), lambda l, j: (l, 0, j)),
        ],
        out_specs=pl.BlockSpec((1, rows, tn), lambda l, j: (l, 0, j)),
        out_shape=jax.ShapeDtypeStruct((depth, rows, n), F32),
        compiler_params=_cparams(("parallel", "parallel")),
        name="ada_mod",
    )(c8, ada_w, ada_b.reshape(depth, 1, n))
    return out[:, :bsz]


def _norm_proj_kernel(x_ref, g_ref, sc_ref, sh_ref, w_ref, o_ref, *kv_refs, nsa_kv_start):
    hn = _rms(x_ref[0], NORM_EPS) * g_ref[...]
    hn = (hn * (1.0 + sc_ref[0]) + sh_ref[0]).astype(BF16)
    out = _dot(hn, w_ref[...])
    o_ref[0] = out
    if nsa_kv_start is None:
        return
    tm, dh, kv = out.shape[0], NSA_DH, NSA_GROUPS * NSA_DH
    lane = _iota((tm, LANES), 1)
    pos = pl.program_id(1) * tm + _iota((tm, LANES), 0)
    pos_hi, pos_lo = (pos // LANES * LANES).astype(F32), (pos % LANES).astype(F32)
    in_pos = (lane >= dh) & (lane < dh + 6)
    pos_cols = jnp.where(in_pos & ((lane - dh) % 2 == 0), pos_hi, jnp.where(in_pos, pos_lo, 0.0))
    ones_col = jnp.where(lane == dh, 1.0, 0.0)
    block_onehot = jnp.where(lane == pos // NSA_SEL_LEN, 1.0, 0.0)

    def group_tile(seg, gi):
        c0 = nsa_kv_start + seg * kv + (gi // 2) * LANES
        tile = out[:, c0:c0 + LANES]
        return pltpu.roll(tile, dh, 1) if gi % 2 else tile

    def layout(seg, const, extra=None):
        tiles = []
        for gi in range(NSA_GROUPS):
            tiles.append(jnp.where(lane < dh, group_tile(seg, gi), const))
            if extra is not None:
                tiles.append(extra)
        return jnp.concatenate(tiles, axis=1).astype(BF16)

    ks_ref, vs_ref, kw_ref, vw_ref = kv_refs
    ks_ref[0] = layout(0, pos_cols, block_onehot)
    vs_ref[0] = layout(1, ones_col)
    kw_ref[0] = layout(2, pos_cols)
    vw_ref[0] = layout(3, ones_col)


def _norm_proj(x, g, sc, sh, w, *, tm=512, nsa_kv_start=None):
    bsz, t, d = x.shape
    n = w.shape[1]
    out_specs = [pl.BlockSpec((1, tm, n), lambda b, i: (b, i, 0))]
    out_shape = [jax.ShapeDtypeStruct((bsz, t, n), F32)]
    if nsa_kv_start is not None:
        for width in (2 * LANES, LANES, LANES, LANES):
            out_specs.append(pl.BlockSpec((1, tm, NSA_GROUPS * width), lambda b, i: (b, i, 0)))
            out_shape.append(jax.ShapeDtypeStruct((bsz, t, NSA_GROUPS * width), BF16))
    outs = pl.pallas_call(
        functools.partial(_norm_proj_kernel, nsa_kv_start=nsa_kv_start),
        grid=(bsz, t // tm),
        in_specs=[
            pl.BlockSpec((1, tm, d), lambda b, i: (b, i, 0)),
            pl.BlockSpec((1, d), lambda b, i: (0, 0)),
            pl.BlockSpec((1, 1, d), lambda b, i: (b, 0, 0)),
            pl.BlockSpec((1, 1, d), lambda b, i: (b, 0, 0)),
            pl.BlockSpec((d, n), lambda b, i: (0, 0)),
        ],
        out_specs=out_specs,
        out_shape=out_shape,
        compiler_params=_cparams(("parallel", "parallel")),
        name="norm_proj",
    )(x, g.reshape(1, d), sc, sh, w)
    return outs[0] if nsa_kv_start is None else outs


def _ffn_kernel(*refs, final_norm, n_mix):
    mix_refs, wo_refs = refs[:n_mix], refs[n_mix:2 * n_mix]
    (x_ref, g1_ref, g_ref, sc_ref, sh_ref, gate_ref, wu_ref, wv_ref, cw_ref, cb_ref, wd_ref, fg_ref,
     o_ref, hn_ref, acc_ref, halo_ref, x1_ref) = refs[2 * n_mix:]
    ti = pl.program_id(1)
    fj = pl.program_id(2)
    tm, fk = acc_ref.shape[0], wu_ref.shape[1]

    @pl.when(fj == 0)
    def _():
        proj = _mm(mix_refs[0][0], wo_refs[0][...])
        for m_ref, w_ref in zip(mix_refs[1:], wo_refs[1:]):
            proj = proj + _mm(m_ref[0], w_ref[...])
        x1 = x_ref[0] + g1_ref[0] * proj
        x1_ref[...] = x1
        hn = _rms(x1, NORM_EPS) * g_ref[...]
        hn_ref[...] = (hn * (1.0 + sc_ref[0]) + sh_ref[0]).astype(BF16)
        acc_ref[...] = jnp.zeros_like(acc_ref)

    @pl.when(ti == 0)
    def _():
        halo_ref[fj] = jnp.zeros((8, fk), F32)

    hn = hn_ref[...]
    u = _dot(hn, wu_ref[...])
    v = _dot(hn, wv_ref[...])
    prev = halo_ref[fj]
    row = _iota((tm, fk), 0)
    u1 = jnp.where(row == 0, prev[7:8], pltpu.roll(u, 1, 0))
    u2 = jnp.where(row == 0, prev[6:7], jnp.where(row == 1, prev[7:8], pltpu.roll(u, 2, 0)))
    halo_ref[fj] = u[tm - 8:tm]
    cw = cw_ref[...]
    uc = cw[0:1] * u2 + cw[1:2] * u1 + cw[2:3] * u + cb_ref[...]
    h = _gelu_tanh(uc) * v
    acc_ref[...] += _dot(h.astype(BF16), wd_ref[...])

    @pl.when(fj == pl.num_programs(2) - 1)
    def _():
        y = x1_ref[...] + gate_ref[0] * acc_ref[...]
        if final_norm:
            y = _rms(y, NORM_EPS) * fg_ref[...]
        o_ref[0] = y


def _conv_ffn(mixes, w_out, x, g1, g, sc, sh, gate, w_up, conv_w, conv_b, w_down, final_g, *, final_norm, tm=512, fk=2816):
    bsz, t, d = x.shape
    f = w_down.shape[0]
    nf = f // fk
    widths = [m.shape[-1] for m in mixes]
    offs = np.cumsum([0] + widths).tolist()
    w_outs = [w_out[o:o + k] for o, k in zip(offs, widths)]
    kern = functools.partial(_ffn_kernel, final_norm=final_norm, n_mix=len(mixes))
    return pl.pallas_call(
        kern,
        grid=(bsz, t // tm, nf),
        in_specs=[pl.BlockSpec((1, tm, k), lambda b, i, j: (b, i, 0)) for k in widths]
        + [pl.BlockSpec((k, d), lambda b, i, j: (0, 0)) for k in widths]
        + [
            pl.BlockSpec((1, tm, d), lambda b, i, j: (b, i, 0)),
            pl.BlockSpec((1, 1, d), lambda b, i, j: (b, 0, 0)),
            pl.BlockSpec((1, d), lambda b, i, j: (0, 0)),
            pl.BlockSpec((1, 1, d), lambda b, i, j: (b, 0, 0)),
            pl.BlockSpec((1, 1, d), lambda b, i, j: (b, 0, 0)),
            pl.BlockSpec((1, 1, d), lambda b, i, j: (b, 0, 0)),
            pl.BlockSpec((d, fk), lambda b, i, j: (0, j)),
            pl.BlockSpec((d, fk), lambda b, i, j: (0, j + nf)),
            pl.BlockSpec((3, fk), lambda b, i, j: (0, j)),
            pl.BlockSpec((1, fk), lambda b, i, j: (0, j)),
            pl.BlockSpec((fk, d), lambda b, i, j: (j, 0)),
            pl.BlockSpec((1, d), lambda b, i, j: (0, 0)),
        ],
        out_specs=pl.BlockSpec((1, tm, d), lambda b, i, j: (b, i, 0)),
        out_shape=jax.ShapeDtypeStruct((bsz, t, d), F32),
        scratch_shapes=[pltpu.VMEM((tm, d), BF16), pltpu.VMEM((tm, d), F32), pltpu.VMEM((nf, 8, fk), F32),
                        pltpu.VMEM((tm, d), F32)],
        compiler_params=_cparams(("parallel", "arbitrary", "arbitrary")),
        name="conv_ffn",
    )(*mixes, *w_outs, x, g1, g.reshape(1, d), sc, sh, gate, w_up, w_up, conv_w, conv_b.reshape(1, f), w_down,
      final_g.reshape(1, d))


def _gla_kernel(q_ref, k_ref, v_ref, og_ref, lr_ref, aup_ref, ab_ref, gg_ref, o_ref,
                st_ref, b_scr, bend_scr, q_scr, k_scr, kd_scr, qg_scr, vt_scr, o_scr):
    tb = q_ref.shape[1]
    cs = GLA_CHUNK
    n_chunks = tb // cs

    @pl.when(pl.program_id(1) == 0)
    def _():
        st_ref[...] = jnp.zeros_like(st_ref)

    z = _mm3(lr_ref[0], aup_ref[...]) + ab_ref[...]
    la = -_softplus(-z) * (1.0 / GLA_GATE_NORM)
    rr, cc = _iota((tb, tb), 0), _iota((tb, tb), 1)
    same = (rr // cs) == (cc // cs)
    b = _sel_mm(jnp.where(same & (cc <= rr), 1.0, 0.0), la)
    bend = _sel_mm(jnp.where(same, 1.0, 0.0), la)
    q = q_ref[0] * (GLA_DK ** -0.5)
    k = k_ref[0]
    b_scr[...] = b
    bend_scr[...] = bend
    q_scr[...] = q
    k_scr[...] = k
    kd_scr[...] = (k * jnp.exp(bend - b)).astype(BF16)
    qg_scr[...] = q * jnp.exp(b)
    v_all = v_ref[0]
    for h in range(GLA_HEADS):
        vt_scr[h] = v_all[:, h * GLA_DV:(h + 1) * GLA_DV].T.astype(BF16)

    lane_c = _iota((cs, LANES), 1)
    row_c = _iota((cs, LANES), 0)
    lane_t = _iota((tb, LANES), 1)
    head_rows = _iota((LANES, LANES), 0) // GLA_DK

    def chunk(c, carry):
        r0 = c * cs
        tmask = (lane_t // cs) == c
        for p in range(GLA_HEADS // 2):
            lanes = slice(p * LANES, (p + 1) * LANES)
            bc = b_scr[pl.ds(r0, cs), lanes]
            qc = q_scr[pl.ds(r0, cs), lanes]
            kc = k_scr[pl.ds(r0, cs), lanes]
            blocks = []
            for s in range(cs):
                m = row_c >= s
                rel = jnp.where(m, bc - bc[s:s + 1], 0.0)
                blocks.append(jnp.where(m, qc * kc[s:s + 1] * jnp.exp(rel), 0.0))
            a_st = jnp.concatenate(blocks, axis=0)
            qgc = qg_scr[pl.ds(r0, cs), lanes]
            lhs = jnp.concatenate([jnp.where(lane_c < GLA_DK, qgc, 0.0),
                                   jnp.where(lane_c >= GLA_DK, qgc, 0.0)], axis=0)
            st = st_ref[p]
            o_inter = _mm_nt(lhs, st)
            for hh in range(2):
                h = 2 * p + hh
                zsum = _mm_sel(a_st, jnp.where(head_rows == hh, 1.0, 0.0), terms=1)
                vc = v_ref[0, pl.ds(r0, cs), h * GLA_DV:(h + 1) * GLA_DV]
                o_h = o_inter[hh * cs:(hh + 1) * cs]
                for s in range(cs):
                    o_h = o_h + zsum[s * cs:(s + 1) * cs] * vc[s:s + 1]
                o_scr[pl.ds(r0, cs), h * GLA_DV:(h + 1) * GLA_DV] = o_h
            lhs_u = jnp.concatenate([jnp.where(tmask, vt_scr[2 * p], 0.0).astype(BF16),
                                     jnp.where(tmask, vt_scr[2 * p + 1], 0.0).astype(BF16)], axis=0)
            upd = _dot(lhs_u, kd_scr[:, lanes])
            upd = jnp.where(lane_t < GLA_DK, upd[:GLA_DV], upd[GLA_DV:])
            decay = jnp.exp(bend_scr[pl.ds(r0, 1), lanes])
            st_ref[p] = st * decay + upd
        return carry

    for c in range(n_chunks):
        chunk(c, 0)

    og = og_ref[0]
    for h in range(GLA_HEADS):
        sl = slice(h * GLA_DV, (h + 1) * GLA_DV)
        gate = og[:, sl]
        o_ref[0, :, sl] = (_rms(o_scr[:, sl], NORM_EPS) * gg_ref[...] * (gate * _sigmoid(gate))).astype(o_ref.dtype)


def _gla(p, a_up_pad, a_b, gla_g, *, tb=128):
    bsz, t, _ = p.shape
    qk = GLA_HEADS * GLA_DK
    vw = GLA_HEADS * GLA_DV
    assert tb == LANES and GLA_DV == LANES
    return pl.pallas_call(
        _gla_kernel,
        grid=(bsz, t // tb),
        in_specs=[
            pl.BlockSpec((1, tb, qk), lambda b, i: (b, i, 0)),
            pl.BlockSpec((1, tb, qk), lambda b, i: (b, i, 1)),
            pl.BlockSpec((1, tb, vw), lambda b, i: (b, i, 1)),
            pl.BlockSpec((1, tb, vw), lambda b, i: (b, i, 2)),
            pl.BlockSpec((1, tb, LANES), lambda b, i: (b, i, 24)),
            pl.BlockSpec((LANES, qk), lambda b, i: (0, 0)),
            pl.BlockSpec((1, qk), lambda b, i: (0, 0)),
            pl.BlockSpec((1, GLA_DV), lambda b, i: (0, 0)),
        ],
        out_specs=pl.BlockSpec((1, tb, vw), lambda b, i: (b, i, 0)),
        out_shape=jax.ShapeDtypeStruct((bsz, t, vw), BF16),
        scratch_shapes=[
            pltpu.VMEM((GLA_HEADS // 2, GLA_DV, LANES), F32),
            pltpu.VMEM((tb, qk), F32), pltpu.VMEM((tb, qk), F32),
            pltpu.VMEM((tb, qk), F32), pltpu.VMEM((tb, qk), F32),
            pltpu.VMEM((tb, qk), BF16), pltpu.VMEM((tb, qk), F32),
            pltpu.VMEM((GLA_HEADS, GLA_DV, tb), BF16),
            pltpu.VMEM((tb, vw), F32),
        ],
        compiler_params=_cparams(("parallel", "arbitrary")),
        name="gla",
    )(p, p, p, p, p, a_up_pad, a_b.reshape(1, qk), gla_g.reshape(1, GLA_DV))


def _rwkv_kernel(r_ref, k_ref, v_ref, wa_ref, gl_ref, mur_ref, muk_ref, muv_ref, muwa_ref, mugl_ref,
                 w0_ref, w2_ref, a0_ref, a2_ref, g2_ref, kkw_ref, ka_ref, rk_ref, gnw_ref, gnb_ref,
                 o_ref,
                 lr_scr, lk_scr, lv_scr, lwa_scr, lgl_scr, s_ref,
                 lw_p, r_p, kk_p, be_p, k2_p, v_p, y_p, pp_all, y0_all, ge_all, g_all, h_all):
    tb = r_ref.shape[1]
    cs = RWKV_CHUNK
    n_chunks = tb // cs
    n_pairs = RWKV_HEADS // 2
    ti = pl.program_id(1)
    lasts = (lr_scr, lk_scr, lv_scr, lwa_scr, lgl_scr)

    @pl.when(ti == 0)
    def _():
        s_ref[...] = jnp.zeros_like(s_ref)
        for ref in lasts:
            ref[...] = jnp.zeros_like(ref)

    def shifted(x_ref, last_ref, mu_ref):
        x = x_ref[0]
        row = _iota(x.shape, 0)
        prev = jnp.where(row == 0, last_ref[7:8], pltpu.roll(x, 1, 0))
        last_ref[...] = x[tb - 8:tb]
        return x + (prev - x) * mu_ref[...]

    r = shifted(r_ref, lr_scr, mur_ref)
    k = shifted(k_ref, lk_scr, muk_ref)
    v = shifted(v_ref, lv_scr, muv_ref)
    wa = shifted(wa_ref, lwa_scr, muwa_ref)
    gl = shifted(gl_ref, lgl_scr, mugl_ref)

    logw = -jnp.exp(-_softplus(-(w0_ref[...] + _mm3(jnp.tanh(wa), w2_ref[...]))) - 0.5)
    a = _sigmoid(a0_ref[...] + _mm3(wa, a2_ref[...]))
    g = _mm(_sigmoid(gl), g2_ref[...])
    seg = jnp.where((_iota((LANES, LANES), 0) // RWKV_N) == (_iota((LANES, LANES), 1) // RWKV_N), 1.0, 0.0)

    def segsum(x):
        return jnp.concatenate([_mm_sel(x[:, i * LANES:(i + 1) * LANES], seg, terms=2) for i in range(n_pairs)], axis=1)

    kk = k * kkw_ref[...]
    kk = kk * lax.rsqrt(jnp.maximum(segsum(kk * kk), 1e-24))
    k2 = k * (1.0 + (a - 1.0) * ka_ref[...])
    beta = kk * a
    for p in range(n_pairs):
        sl = slice(p * LANES, (p + 1) * LANES)
        lw_p[p] = logw[:, sl]
        r_p[p] = r[:, sl]
        kk_p[p] = kk[:, sl]
        be_p[p] = beta[:, sl]
        k2_p[p] = k2[:, sl]
        v_p[p] = v[:, sl]

    rr, cc = _iota((tb, tb), 0), _iota((tb, tb), 1)
    same = (rr // cs) == (cc // cs)
    tri_incl = same & (cc <= rr)
    tri_strict = same & (cc < rr)
    l_incl = jnp.where(tri_incl, 1.0, 0.0)
    l_all = jnp.where(same, 1.0, 0.0)
    eye = jnp.where(rr == cc, 1.0, 0.0)
    lane = _iota((tb, LANES), 1)
    half = lane < RWKV_N
    same_half = (rr // RWKV_N) == (cc // RWKV_N)
    lane_c = _iota((cs, LANES), 1)
    lane_s = _iota((RWKV_N, LANES), 1)

    def below_left(s):
        return ((rr // (2 * s)) == (cc // (2 * s))) & ((rr % (2 * s)) >= s) & ((cc % (2 * s)) < s)

    pairs = range(n_pairs)
    heads = [(p, hh) for p in pairs for hh in range(2)]
    l_both = jnp.concatenate([l_incl, l_all], axis=0)
    rt, bt, at, vp, ak = {}, {}, {}, {}, {}
    for p in pairs:
        lw = lw_p[p]
        sums = _sel_mm(l_both, lw)
        cum = sums[:tb]
        ge_all[p] = jnp.exp(sums[tb:])
        ig = jnp.exp(-cum)
        rt[p] = r_p[p] * jnp.exp(cum)
        bt[p] = kk_p[p] * jnp.exp(cum - lw)
        at[p] = -(be_p[p] * ig)
        vp[p] = v_p[p]
        ak[p] = jnp.concatenate([at[p], k2_p[p] * ig], axis=0)
    a_ab, a_ak, a_ra, a_rk = {}, {}, {}, {}
    for h in heads:
        p, hh = h
        hm = half if hh == 0 else jnp.logical_not(half)
        s_b = _mm3_nt(jnp.where(hm, bt[p], 0.0), ak[p])
        s_r = _mm_nt(jnp.where(hm, rt[p], 0.0), ak[p])
        a_ab[h] = jnp.where(tri_strict, s_b[:, :tb], 0.0)
        a_ak[h] = jnp.where(tri_strict, s_b[:, tb:], 0.0)
        a_ra[h] = jnp.where(tri_incl, s_r[:, :tb], 0.0)
        a_rk[h] = jnp.where(tri_incl, s_r[:, tb:], 0.0)
    m_inv = {h: eye + jnp.where(below_left(1), a_ab[h], 0.0) for h in heads}
    w1 = {h: _mm3(a_ak[h], vp[h[0]]) for h in heads}
    s = 2
    while s < cs:
        low = {h: _mm3(m_inv[h], jnp.where(below_left(s), a_ab[h], 0.0)) for h in heads}
        m_inv = {h: m_inv[h] + _mm3(low[h], m_inv[h]) for h in heads}
        s *= 2
    mw = {h: _mm3(m_inv[h], jnp.concatenate([w1[h], bt[h[0]]], axis=1)) for h in heads}
    p_h = {h: _mm(a_ra[h], mw[h][:, LANES:]) for h in heads}
    y0_h = {h: _mm(a_ra[h], mw[h][:, :LANES]) + _mm(a_rk[h], vp[h[0]]) for h in heads}
    zeros = jnp.zeros((tb, LANES), F32)
    for p in pairs:
        u0 = jnp.where(half, mw[(p, 0)][:, :LANES], mw[(p, 1)][:, :LANES])
        mb = jnp.where(half, mw[(p, 0)][:, LANES:], mw[(p, 1)][:, LANES:])
        pp_all[p] = rt[p] + jnp.where(half, p_h[(p, 0)], p_h[(p, 1)])
        y0_all[p] = jnp.where(half, y0_h[(p, 0)], y0_h[(p, 1)])
        u0t, vt, mbt = u0.T, vp[p].T, mb.T
        for c in range(n_chunks):
            tmask = (lane // cs) == c
            lhs = jnp.concatenate([
                jnp.concatenate([jnp.where(tmask, u0t, 0.0), jnp.where(tmask, vt, 0.0)], axis=1),
                jnp.concatenate([jnp.where(tmask, mbt, 0.0), zeros], axis=1)], axis=0)
            hg = _mm3(lhs, ak[p])
            h_all[p * n_chunks + c] = jnp.where(lane_s < RWKV_N, hg[:RWKV_N], hg[RWKV_N:tb])
            g_all[p * n_chunks + c] = jnp.where(same_half, hg[tb:], 0.0) + eye

    state = [s_ref[p] for p in range(n_pairs)]
    for c in range(n_chunks):
        rows = slice(c * cs, (c + 1) * cs)
        for p in range(n_pairs):
            sp = state[p]
            pc = pp_all[p, rows, :]
            lhs_y = jnp.concatenate([jnp.where(lane_c < RWKV_N, pc, 0.0), jnp.where(lane_c >= RWKV_N, pc, 0.0)], axis=0)
            yy = _mm_nt(lhs_y, jnp.concatenate([sp, sp], axis=0))
            y_p[p, rows, :] = jnp.where(lane_c < RWKV_N, yy[:cs], yy[cs:]) + y0_all[p, rows, :]
            state[p] = (_mm3(sp, g_all[p * n_chunks + c]) + h_all[p * n_chunks + c]) * ge_all[p, c * cs:c * cs + 1, :]
    for p in range(n_pairs):
        s_ref[p] = state[p]

    y = jnp.concatenate([y_p[p] for p in range(n_pairs)], axis=1)
    mu = segsum(y) * (1.0 / RWKV_N)
    yc = y - mu
    var = segsum(yc * yc) * (1.0 / RWKV_N)
    yn = yc * lax.rsqrt(var + RWKV_GN_EPS) * gnw_ref[...] + gnb_ref[...]
    bonus = segsum(r * k2 * rk_ref[...]) * v
    o_ref[0] = ((yn + bonus) * g).astype(o_ref.dtype)


def _rwkv(p, mus, w0, w2p, a0, a2p, g2, k_k, k_a, r_k, gn_w, gn_b, *, tb=128):
    bsz, t, _ = p.shape
    w = RWKV_HEADS * RWKV_N
    n_pairs = RWKV_HEADS // 2
    assert tb == LANES
    row = lambda a: a.reshape(1, -1)
    full = lambda shape: pl.BlockSpec(shape, lambda b, i: (0,) * len(shape))
    ptile = lambda: pltpu.VMEM((n_pairs, tb, LANES), F32)
    return pl.pallas_call(
        _rwkv_kernel,
        grid=(bsz, t // tb),
        in_specs=[
            pl.BlockSpec((1, tb, w), lambda b, i: (b, i, 3)),
            pl.BlockSpec((1, tb, w), lambda b, i: (b, i, 4)),
            pl.BlockSpec((1, tb, w), lambda b, i: (b, i, 5)),
            pl.BlockSpec((1, tb, LANES), lambda b, i: (b, i, 25)),
            pl.BlockSpec((1, tb, LANES), lambda b, i: (b, i, 26)),
            full((1, w)), full((1, w)), full((1, w)), full((1, LANES)), full((1, LANES)),
            full((1, w)), full((LANES, w)), full((1, w)), full((LANES, w)), full((LANES, w)),
            full((1, w)), full((1, w)), full((1, w)), full((1, w)), full((1, w)),
        ],
        out_specs=pl.BlockSpec((1, tb, w), lambda b, i: (b, i, 0)),
        out_shape=jax.ShapeDtypeStruct((bsz, t, w), BF16),
        scratch_shapes=[
            pltpu.VMEM((8, w), F32), pltpu.VMEM((8, w), F32), pltpu.VMEM((8, w), F32),
            pltpu.VMEM((8, LANES), F32), pltpu.VMEM((8, LANES), F32),
            pltpu.VMEM((n_pairs, RWKV_N, LANES), F32),
            ptile(), ptile(), ptile(), ptile(), ptile(), ptile(), ptile(), ptile(), ptile(), ptile(),
            pltpu.VMEM((n_pairs * (tb // RWKV_CHUNK), LANES, LANES), F32),
            pltpu.VMEM((n_pairs * (tb // RWKV_CHUNK), RWKV_N, LANES), F32),
        ],
        compiler_params=_cparams(("parallel", "arbitrary")),
        name="rwkv7",
    )(p, p, p, p, p, *[row(m) for m in mus], row(w0), w2p, row(a0), a2p, g2,
      row(k_k), row(k_a), row(r_k), row(gn_w), row(gn_b))


def _cmp_kernel(sk_ref, sv_ref, wak_ref, wbk_ref, wav_ref, wbv_ref, pek_ref, pev_ref, w1k_ref, w1v_ref,
                w2k_ref, w2v_ref, ok_ref, ov_ref):
    def one(seg_ref, wa_ref, wb_ref, pe_ref, w1_ref, w2_ref, o_ref):
        seg = seg_ref[0].astype(BF16)
        first = _dot(seg, wa_ref[...])
        second = _dot(seg, wb_ref[...])
        n = first.shape[0]
        pe_term = _mm3(pe_ref[...], w1_ref[...])[0:1]
        hidden = _gelu_tanh(first + pltpu.roll(second, n - 1, 0) + pe_term)
        o_ref[0] = _mm(hidden, w2_ref[...])

    one(sk_ref, wak_ref, wbk_ref, pek_ref, w1k_ref, w2k_ref, ok_ref)
    one(sv_ref, wav_ref, wbv_ref, pev_ref, w1v_ref, w2v_ref, ov_ref)


def _nsa_compress(kc_tok, vc_tok, pe_k, w1_k, w2_k, pe_v, w1_v, w2_v):
    bsz, t, gw = kc_tok.shape
    st, dh, hid, g = NSA_CMP_STRIDE, NSA_DH, NSA_CMP_HIDDEN, NSA_GROUPS
    nseg = t // st
    eye = jnp.eye(g, dtype=F32)

    def expand_w1(w1):
        w = w1.reshape(NSA_CMP_LEN, dh, hid)
        big = jnp.einsum('ldc,gh->lgdhc', w, eye).reshape(NSA_CMP_LEN * g * dh, g * hid)
        half = st * g * dh
        return big[:half].astype(BF16), big[half:].astype(BF16)

    def expand_w2(w2):
        return jnp.einsum('cd,gh->gchd', w2, eye).reshape(g * hid, g * dh).astype(BF16)

    def pe_rows(pe):
        return jnp.zeros((8, NSA_CMP_LEN * dh), F32).at[0].set(pe.reshape(-1))

    wak, wbk = expand_w1(w1_k)
    wav, wbv = expand_w1(w1_v)
    full = lambda shape: pl.BlockSpec(shape, lambda b: (0,) * len(shape))
    seg_spec = pl.BlockSpec((1, nseg, st * gw), lambda b: (b, 0, 0))
    out_spec = pl.BlockSpec((1, nseg, gw), lambda b: (b, 0, 0))
    return pl.pallas_call(
        _cmp_kernel,
        grid=(bsz,),
        in_specs=[seg_spec, seg_spec,
                  full(wak.shape), full(wbk.shape), full(wav.shape), full(wbv.shape),
                  full((8, NSA_CMP_LEN * dh)), full((8, NSA_CMP_LEN * dh)),
                  full((NSA_CMP_LEN * dh, g * hid)), full((NSA_CMP_LEN * dh, g * hid)),
                  full((g * hid, gw)), full((g * hid, gw))],
        out_specs=[out_spec, out_spec],
        out_shape=[jax.ShapeDtypeStruct((bsz, nseg, gw), F32)] * 2,
        compiler_params=_cparams(("parallel",)),
        name="nsa_compress",
    )(kc_tok.reshape(bsz, nseg, st * gw), vc_tok.reshape(bsz, nseg, st * gw),
      wak, wbk, wav, wbv, pe_rows(pe_k), pe_rows(pe_v),
      jnp.tile(w1_k, (1, g)), jnp.tile(w1_v, (1, g)), expand_w2(w2_k), expand_w2(w2_v))


def _nsa_attn_kernel(q_ref, kc_ref, vc_ref, ks_ref, vs_ref, kw_ref, vw_ref, gt_ref, sl_ref, o_ref, acc_scr):
    g = pl.program_id(1)
    qi = pl.program_id(2)
    qb, dh, rep = NSA_Q_BLOCK, NSA_DH, NSA_REP
    n_cmp_pad = kc_ref.shape[1]
    kt = LANES
    nb = q_ref.shape[1] // qb
    rows_all = nb * rep * qb

    lane_q = _iota((1, LANES), 1)
    lane_b = _iota((qb, LANES), 1)
    slope_cols = []
    for r in range(rep):
        s1, s2, s3 = (t.astype(F32) for t in _split3(sl_ref[0, r:r + 1, :]))
        slope_cols.append(jnp.where((lane_q >= dh) & (lane_q < dh + 2), s1,
                                    jnp.where((lane_q >= dh + 2) & (lane_q < dh + 4), s2,
                                              jnp.where((lane_q >= dh + 4) & (lane_q < dh + 6), s3, 0.0))))
    q_rows = []
    for blk in range(nb):
        for r in range(rep):
            two_heads = q_ref[0, blk * qb:(blk + 1) * qb, (r // 2) * LANES:(r // 2 + 1) * LANES]
            if r % 2:
                two_heads = pltpu.roll(two_heads, dh, 1)
            q_rows.append(jnp.where(lane_b < dh, two_heads * (dh ** -0.5), slope_cols[r]))
    q = jnp.concatenate(q_rows, axis=0).astype(BF16)
    t0 = qi * (nb * qb)

    def row_pos(shape):
        rows = _iota(shape, 0)
        return rows // (rep * qb) * qb + rows % qb

    ahead_of_lane = row_pos((rows_all, kt)) - _iota((rows_all, kt), 1)

    def lane_tiles(s):
        return [s[:, j * kt:(j + 1) * kt] for j in range(s.shape[1] // kt)]

    def tile_max(tiles, start):
        m = start
        for tile in tiles:
            m = jnp.maximum(m, tile)
        return m

    def normalised(acc):
        return acc / acc[:, dh:dh + 1]

    gsel = jnp.where(_iota((LANES, LANES), 0) == g * (rep * 3) + _iota((LANES, LANES), 1), 1.0, 0.0)
    gates = _sigmoid(_mm_sel(gt_ref[0], gsel))

    n_idx = _iota((rows_all, n_cmp_pad), 1)
    valid_c = ((t0 + row_pos((rows_all, n_cmp_pad)) >= n_idx * NSA_CMP_STRIDE + NSA_CMP_LEN - 1)
               & (n_idx < n_cmp_pad - 1))
    s_cmp = jnp.where(valid_c, _dot_nt(q, kc_ref[0]), NEG_BIG)
    e_cmp = jnp.where(valid_c, jnp.exp(s_cmp - jnp.max(s_cmp, axis=-1, keepdims=True)), 0.0)
    den = jnp.sum(e_cmp, axis=-1, keepdims=True)
    p_cmp = e_cmp / jnp.where(den > 0.0, den, 1.0)
    o_cmp = _mm(p_cmp, vc_ref[0])
    n_sel_blocks = ks_ref.shape[1] // NSA_SEL_LEN
    on, oj = _iota((n_cmp_pad, LANES), 0), _iota((n_cmp_pad, LANES), 1)
    overlap = jnp.where((on * NSA_CMP_STRIDE <= oj * NSA_SEL_LEN + NSA_SEL_LEN - 1)
                        & (on * NSA_CMP_STRIDE + NSA_CMP_LEN - 1 >= oj * NSA_SEL_LEN)
                        & (oj < n_sel_blocks) & (on < n_cmp_pad - 1), 1.0, 0.0)
    p_groups = []
    for blk in range(nb):
        rows = [slice((blk * rep + r) * qb, (blk * rep + r + 1) * qb) for r in range(rep)]
        p_sum = p_cmp[rows[0]]
        for rs in rows[1:]:
            p_sum = p_sum + p_cmp[rs]
        p_groups.append(p_sum)
    imp = _mm_sel(jnp.concatenate(p_groups, axis=0), overlap)

    nq = nb * qb
    imp_t = imp.T[:n_sel_blocks]
    jj = _iota((n_sel_blocks, nq), 0)
    jf = jj.astype(F32)
    ahead = (t0 + _iota((n_sel_blocks, nq), 1)) // NSA_SEL_LEN - jj
    valid_b = ahead >= 0
    forced = (jj == 0) | (valid_b & (ahead < NSA_N_LOCAL))
    score = jnp.where(valid_b, imp_t + jnp.where(forced, NSA_FORCE, 0.0), -NSA_FORCE)
    sel = jnp.zeros((n_sel_blocks, nq), F32)
    for _ in range(NSA_N_SEL):
        best = jnp.max(score, axis=0, keepdims=True)
        first = jnp.min(jnp.where(score == best, jf, float(n_sel_blocks)), axis=0, keepdims=True)
        pick = jf == first
        sel = jnp.where(pick, 1.0, sel)
        score = jnp.where(pick, NEG_BIG, score)
    sel_q = jnp.concatenate([sel, jnp.zeros((LANES - n_sel_blocks, nq), F32)], axis=0).T.astype(BF16)

    w_tiles = NSA_WINDOW // kt
    win_s, win_v = [], []
    for rel in range(-w_tiles, nb):
        kb = qi * nb + rel
        k0 = pl.multiple_of(jnp.maximum(kb, 0) * kt, kt)
        s = _dot_nt(q, kw_ref[0, pl.ds(k0, kt), :])
        if not all(-w_tiles < rel - blk < 0 for blk in range(nb)):
            dist = ahead_of_lane - rel * kt
            s = jnp.where((dist >= 0) & (dist < NSA_WINDOW), s, NEG_BIG)
        if rel < 0:
            s = s + jnp.where(kb >= 0, 0.0, NEG_BIG)
        win_s.append(s)
        win_v.append(vw_ref[0, pl.ds(k0, kt), :])
    m_win = jnp.broadcast_to(jnp.max(tile_max(win_s[1:], win_s[0]), axis=-1, keepdims=True), (rows_all, kt))
    e_win = jnp.concatenate([jnp.exp(s - m_win).astype(BF16) for s in win_s], axis=1)
    o_win = normalised(_dot(e_win, jnp.concatenate(win_v, axis=0)))

    def gate_of(rb, branch):
        blk, r = divmod(rb, rep)
        col = 3 * r + branch
        return jnp.broadcast_to(gates[blk * qb:(blk + 1) * qb, col:col + 1], (qb, LANES))

    row_blocks = [slice(rb * qb, (rb + 1) * qb) for rb in range(nb * rep)]
    partial = [gate_of(rb, 0) * o_cmp[rs] + gate_of(rb, 2) * o_win[rs] for rb, rs in enumerate(row_blocks)]
    gate_sel = jnp.concatenate([gate_of(rb, 1) for rb in range(nb * rep)], axis=0)

    tpg = NSA_SWEEP_TILES
    kg = tpg * kt
    assert tpg % nb == 0

    sel_bias = ((sel_q.astype(F32) - 1.0) * (2.0 ** 100)).astype(BF16)
    q_sel = jnp.concatenate([q, jnp.concatenate([sel_bias[blk * qb:(blk + 1) * qb] for blk in range(nb)
                                                 for _ in range(rep)], axis=0)], axis=1)
    n_past = (qi * nb) // tpg
    first_off = (n_past * tpg - qi * nb) * kt

    def sweep(n_before):
        def run():
            m_run, acc = None, None
            for gi in range(n_before + 1):
                tiles = lane_tiles(_dot_nt(q_sel, ks_ref[0, gi * kg:(gi + 1) * kg, :]))
                if gi == n_before:
                    tiles = [jnp.where(ahead_of_lane >= first_off + j * kt, tile, NEG_BIG) for j, tile in enumerate(tiles)]
                m_grp = jnp.broadcast_to(jnp.max(tile_max(tiles[1:], tiles[0]), axis=-1, keepdims=True), (rows_all, kt))
                m_new = m_grp if m_run is None else jnp.maximum(m_run, m_grp)
                e = jnp.concatenate([jnp.exp(tile - m_new).astype(BF16) for tile in tiles], axis=1)
                pv = _dot(e, vs_ref[0, gi * kg:(gi + 1) * kg, :])
                acc = pv if acc is None else jnp.exp(m_run - m_new) * acc + pv
                m_run = m_new
            acc_scr[...] = gate_sel * normalised(acc)
        return run

    lax.switch(n_past, [sweep(n) for n in range(ks_ref.shape[1] // kg)])
    mixed = [partial[rb] + acc_scr[rs, :] for rb, rs in enumerate(row_blocks)]
    o_ref[0] = jnp.concatenate(
        [jnp.concatenate([jnp.where(lane_b < dh, mixed[blk * rep + r], pltpu.roll(mixed[blk * rep + r + 1], dh, 1))
                          for r in range(0, rep, 2)], axis=1) for blk in range(nb)], axis=0).astype(o_ref.dtype)


def _nsa_attention(p, kcmp, vcmp, ks, vs, kw, vw, slopes):
    bsz, t, _ = p.shape
    g, rep, dh = NSA_GROUPS, NSA_REP, NSA_DH
    qs = NSA_BLOCKS_PER_STEP * NSA_Q_BLOCK
    whole = lambda a: pl.BlockSpec((1, a.shape[1], a.shape[2] // g), lambda b, gi, i: (b, 0, gi))
    gate_col = p.shape[-1] // LANES - 1
    return pl.pallas_call(
        _nsa_attn_kernel,
        grid=(bsz, g, t // qs),
        in_specs=[
            pl.BlockSpec((1, qs, rep * dh), lambda b, gi, i: (b, i, gi)),
            whole(kcmp), whole(vcmp), whole(ks), whole(vs), whole(kw), whole(vw),
            pl.BlockSpec((1, qs, LANES), lambda b, gi, i: (b, i, gate_col)),
            pl.BlockSpec((1, 8, LANES), lambda b, gi, i: (gi, 0, 0)),
        ],
        out_specs=pl.BlockSpec((1, qs, rep * dh), lambda b, gi, i: (b, i, gi)),
        out_shape=jax.ShapeDtypeStruct((bsz, t, g * rep * dh), BF16),
        scratch_shapes=[pltpu.VMEM((rep * qs, LANES), F32)],
        compiler_params=_cparams(("parallel", "parallel", "arbitrary")),
        name="nsa_attention",
    )(p, kcmp, vcmp, ks, vs, kw, vw, p, slopes)


def _even_mixer(x, norm_g, sc, sh, w_in, shift_mu, a_up, a_b, gla_g, w0, w2, a0, a2, g2, k_k, k_a, r_k, gn_w, gn_b):
    d = x.shape[-1]
    qk, vw, w = GLA_HEADS * GLA_DK, GLA_HEADS * GLA_DV, RWKV_HEADS * RWKV_N
    gla_cols = 2 * qk + 2 * vw + GLA_LOWRANK
    wg, wr = w_in[:, :gla_cols], w_in[:, gla_cols:]
    o_r, o_wl, o_k, o_v, o_al, o_gl = np.cumsum([0, w, RWKV_W_LORA, w, w, RWKV_A_LORA]).tolist()
    pad = lambda a, n: jnp.pad(a, ((0, 0), (0, n - a.shape[1])))
    w_perm = jnp.concatenate([
        wg[:, :2 * qk + 2 * vw],
        wr[:, o_r:o_r + w], wr[:, o_k:o_k + w], wr[:, o_v:o_v + w],
        pad(wg[:, 2 * qk + 2 * vw:], LANES),
        wr[:, o_wl:o_wl + RWKV_W_LORA], wr[:, o_al:o_al + RWKV_A_LORA],
        wr[:, o_gl:o_gl + RWKV_G_LORA]], axis=1).astype(BF16)
    p = _norm_proj(x, norm_g, sc, sh, w_perm)
    a_up_pad = jnp.zeros((LANES, qk), F32).at[:GLA_LOWRANK].set(a_up)
    o_gla = _gla(p, a_up_pad, a_b, gla_g)
    mu = shift_mu
    mus = [mu[o_r:o_r + w], mu[o_k:o_k + w], mu[o_v:o_v + w],
           jnp.concatenate([mu[o_wl:o_wl + RWKV_W_LORA], mu[o_al:o_al + RWKV_A_LORA]]), mu[o_gl:o_gl + RWKV_G_LORA]]
    w2p = jnp.zeros((LANES, w), F32).at[:RWKV_W_LORA].set(w2)
    a2p = jnp.zeros((LANES, w), F32).at[RWKV_W_LORA:RWKV_W_LORA + RWKV_A_LORA].set(a2)
    o_rw = _rwkv(p, mus, w0, w2p, a0, a2p, g2, k_k, k_a, r_k.reshape(-1), gn_w, gn_b)
    return [o_gla, o_rw]


def _nsa_mixer(x, norm_g, sc, sh, w_in, pe_k, w1_k, w2_k, pe_v, w1_v, w2_v):
    bsz, t, d = x.shape
    g, dh, heads = NSA_GROUPS, NSA_DH, NSA_HEADS
    n_cols = w_in.shape[1]
    n_pad = -(-n_cols // (3 * LANES)) * (3 * LANES)
    w_pad = jnp.pad(w_in, ((0, 0), (0, n_pad - n_cols))).astype(BF16)
    kv = g * dh
    off = heads * dh
    p, ks, vs, kw, vw = _norm_proj(x, norm_g, sc, sh, w_pad, nsa_kv_start=off + 2 * kv)
    seg = lambda i: p[..., off + i * kv: off + (i + 1) * kv]
    kcmp, vcmp = _nsa_compress(seg(0), seg(1), pe_k, w1_k, w2_k, pe_v, w1_v, w2_v)
    slopes = 2.0 ** (-8.0 * jnp.arange(1, heads + 1, dtype=F32) / heads)
    slopes = jnp.broadcast_to(jnp.pad(slopes.reshape(g, NSA_REP), ((0, 0), (0, 8 - NSA_REP)))[:, :, None], (g, 8, LANES))

    def per_group(a, *cols):
        rows = a.shape[1]
        const = [jnp.broadcast_to(c.astype(BF16)[None], (bsz, rows, c.shape[-1])) for c in cols]
        const.append(jnp.zeros((bsz, rows, LANES - dh - sum(c.shape[-1] for c in cols)), BF16))
        parts = []
        for gi in range(g):
            parts += [a[..., gi * dh:(gi + 1) * dh].astype(BF16)] + const
        return jnp.concatenate(parts, axis=-1)

    cmp_end = jnp.arange(kcmp.shape[1]) * NSA_CMP_STRIDE + NSA_CMP_LEN - 1
    cmp_end_cols = jnp.tile(jnp.stack([cmp_end // LANES * LANES, cmp_end % LANES], axis=-1), (1, 3))
    return [_nsa_attention(p, per_group(kcmp, cmp_end_cols), per_group(vcmp), ks, vs, kw, vw, slopes)]


def kernel(x, c, ada_w, ada_b, norm1_g, norm2_g, ffn_w_up, ffn_conv_w, ffn_conv_b, ffn_w_down, ev_w_in, ev_shift_mu, gla_a_up, gla_a_b, gla_norm_g, rw_w0, rw_w2, rw_a0, rw_a2, rw_g2, rw_k_k, rw_k_a, rw_r_k, rw_gn_w, rw_gn_b, ev_w_out, od_w_in, cmp_pe_k, cmp_w1_k, cmp_w2_k, cmp_pe_v, cmp_w1_v, cmp_w2_v, od_w_out, final_norm_g):
    bsz, t, d = x.shape
    depth = ada_w.shape[0]
    mod = _ada_mod(c, ada_w, ada_b)
    for layer in range(depth):
        sh1, sc1, g1, sh2, sc2, g2 = (mod[layer, :, i * d:(i + 1) * d].reshape(bsz, 1, d) for i in range(6))
        i = layer // 2
        if layer % 2 == 0:
            mix = _even_mixer(x, norm1_g[layer], sc1, sh1, ev_w_in[i], ev_shift_mu[i], gla_a_up[i], gla_a_b[i],
                              gla_norm_g[i], rw_w0[i], rw_w2[i], rw_a0[i], rw_a2[i], rw_g2[i], rw_k_k[i], rw_k_a[i],
                              rw_r_k[i], rw_gn_w[i], rw_gn_b[i])
            w_out = ev_w_out[i]
        else:
            mix = _nsa_mixer(x, norm1_g[layer], sc1, sh1, od_w_in[i], cmp_pe_k[i], cmp_w1_k[i], cmp_w2_k[i],
                             cmp_pe_v[i], cmp_w1_v[i], cmp_w2_v[i])
            w_out = od_w_out[i]
        x = _conv_ffn(mix, w_out.astype(BF16), x, g1, norm2_g[layer], sc2, sh2, g2, ffn_w_up[layer].astype(BF16),
                      ffn_conv_w[layer], ffn_conv_b[layer], ffn_w_down[layer].astype(BF16), final_norm_g,
                      final_norm=(layer == depth - 1))
    return x
```

```python
import functools

import numpy as np
import jax
import jax.numpy as jnp
from jax import lax
from jax.experimental import pallas as pl
from jax.experimental.pallas import tpu as pltpu

F32 = jnp.float32
BF16 = jnp.bfloat16

D_MODEL = 1024
NORM_EPS = 1e-6
GLA_HEADS, GLA_DK, GLA_DV, GLA_LOWRANK, GLA_GATE_NORM, GLA_CHUNK = 4, 64, 128, 16, 16.0, 16
RWKV_HEADS, RWKV_N, RWKV_GN_EPS = 8, 64, 64e-5
RWKV_W_LORA, RWKV_A_LORA, RWKV_G_LORA = 64, 64, 128
RWKV_CHUNK = 64
NSA_HEADS, NSA_GROUPS, NSA_DH = 16, 4, 64
NSA_REP = NSA_HEADS // NSA_GROUPS
NSA_CMP_LEN, NSA_CMP_STRIDE, NSA_CMP_HIDDEN = 32, 16, 64
NSA_SEL_LEN, NSA_N_SEL, NSA_N_LOCAL, NSA_WINDOW, NSA_Q_BLOCK, NSA_FORCE = 64, 8, 2, 512, 128, 100.0
NSA_SWEEP_TILES = 4
NSA_BLOCKS_PER_STEP = 2
FFN_HIDDEN = 2816

LANES = 128
VMEM_LIMIT = 56 * 1024 * 1024
NEG_BIG = -1e30


def _cparams(sem):
    return pltpu.CompilerParams(dimension_semantics=sem, vmem_limit_bytes=VMEM_LIMIT)


def _dot(a, b):
    return jnp.dot(a, b, preferred_element_type=F32)


def _dot_nt(a, b):
    return lax.dot_general(a, b, (((1,), (1,)), ((), ())), preferred_element_type=F32)


def _mm(a, b):
    return _dot(a.astype(BF16), b.astype(BF16))


def _mm_nt(a, b):
    return _dot_nt(a.astype(BF16), b.astype(BF16))


def _split3(a):
    a1 = a.astype(BF16)
    r1 = a - a1.astype(F32)
    a2 = r1.astype(BF16)
    a3 = (r1 - a2.astype(F32)).astype(BF16)
    return a1, a2, a3


def _mm_sel(a, b01, terms=3):
    b = b01.astype(BF16)
    out = None
    for part in _split3(a)[:terms]:
        out = _dot(part, b) if out is None else out + _dot(part, b)
    return out


def _sel_mm(a01, b):
    b1, b2, b3 = _split3(b)
    a = a01.astype(BF16)
    return _dot(a, b1) + _dot(a, b2) + _dot(a, b3)


def _mm3(a, b):
    a1 = a.astype(BF16)
    a2 = (a - a1.astype(F32)).astype(BF16)
    b1 = b.astype(BF16)
    b2 = (b - b1.astype(F32)).astype(BF16)
    return _dot(a1, b1) + _dot(a1, b2) + _dot(a2, b1)


def _mm3_nt(a, b):
    a1 = a.astype(BF16)
    a2 = (a - a1.astype(F32)).astype(BF16)
    b1 = b.astype(BF16)
    b2 = (b - b1.astype(F32)).astype(BF16)
    return _dot_nt(a1, b1) + _dot_nt(a1, b2) + _dot_nt(a2, b1)


def _iota(shape, dim):
    return lax.broadcasted_iota(jnp.int32, shape, dim)


def _sigmoid(x):
    return 1.0 / (1.0 + jnp.exp(-x))


def _softplus(x):
    return jnp.maximum(x, 0.0) + jnp.log(1.0 + jnp.exp(-jnp.abs(x)))


def _gelu_tanh(x):
    return x * (0.5 * (1.0 + jnp.tanh(0.7978845608028654 * (x + 0.044715 * (x * x * x)))))


def _rms(x, eps):
    return x * lax.rsqrt(jnp.mean(x * x, axis=-1, keepdims=True) + eps)


def _mod_kernel(c_ref, w_ref, b_ref, o_ref):
    c = c_ref[...]
    cond = c * _sigmoid(c)
    o_ref[0] = _mm3(cond, w_ref[0]) + b_ref[0]


def _ada_mod(c, ada_w, ada_b):
    depth, d, n = ada_w.shape
    bsz = c.shape[0]
    rows = 8
    c8 = jnp.zeros((rows, d), F32).at[:bsz].set(c)
    tn = 1536
    out = pl.pallas_call(
        _mod_kernel,
        grid=(depth, n // tn),
        in_specs=[
            pl.BlockSpec((rows, d), lambda l, j: (0, 0)),
            pl.BlockSpec((1, d, tn), lambda l, j: (l, 0, j)),
            pl.BlockSpec((1, 1, tn), lambda l, j: (l, 0, j)),
        ],
        out_specs=pl.BlockSpec((1, rows, tn), lambda l, j: (l, 0, j)),
        out_shape=jax.ShapeDtypeStruct((depth, rows, n), F32),
        compiler_params=_cparams(("parallel", "parallel")),
        name="ada_mod",
    )(c8, ada_w, ada_b.reshape(depth, 1, n))
    return out[:, :bsz]


def _norm_proj_kernel(x_ref, g_ref, sc_ref, sh_ref, w_ref, o_ref, *kv_refs, nsa_kv_start):
    hn = _rms(x_ref[0], NORM_EPS) * g_ref[...]
    hn = (hn * (1.0 + sc_ref[0]) + sh_ref[0]).astype(BF16)
    out = _dot(hn, w_ref[...])
    o_ref[0] = out
    if nsa_kv_start is None:
        return
    tm, dh, kv = out.shape[0], NSA_DH, NSA_GROUPS * NSA_DH
    lane = _iota((tm, LANES), 1)
    pos = pl.program_id(1) * tm + _iota((tm, LANES), 0)
    pos_hi, pos_lo = (pos // LANES * LANES).astype(F32), (pos % LANES).astype(F32)
    in_pos = (lane >= dh) & (lane < dh + 6)
    pos_cols = jnp.where(in_pos & ((lane - dh) % 2 == 0), pos_hi, jnp.where(in_pos, pos_lo, 0.0))
    ones_col = jnp.where(lane == dh, 1.0, 0.0)
    block_onehot = jnp.where(lane == pos // NSA_SEL_LEN, 1.0, 0.0)

    def group_tile(seg, gi):
        c0 = nsa_kv_start + seg * kv + (gi // 2) * LANES
        tile = out[:, c0:c0 + LANES]
        return pltpu.roll(tile, dh, 1) if gi % 2 else tile

    def layout(seg, const, extra=None):
        tiles = []
        for gi in range(NSA_GROUPS):
            tiles.append(jnp.where(lane < dh, group_tile(seg, gi), const))
            if extra is not None:
                tiles.append(extra)
        return jnp.concatenate(tiles, axis=1).astype(BF16)

    ks_ref, vs_ref, kw_ref, vw_ref = kv_refs
    ks_ref[0] = layout(0, pos_cols, block_onehot)
    vs_ref[0] = layout(1, ones_col)
    kw_ref[0] = layout(2, pos_cols)
    vw_ref[0] = layout(3, ones_col)


def _norm_proj(x, g, sc, sh, w, *, tm=512, nsa_kv_start=None):
    bsz, t, d = x.shape
    n = w.shape[1]
    out_specs = [pl.BlockSpec((1, tm, n), lambda b, i: (b, i, 0))]
    out_shape = [jax.ShapeDtypeStruct((bsz, t, n), F32)]
    if nsa_kv_start is not None:
        for width in (2 * LANES, LANES, LANES, LANES):
            out_specs.append(pl.BlockSpec((1, tm, NSA_GROUPS * width), lambda b, i: (b, i, 0)))
            out_shape.append(jax.ShapeDtypeStruct((bsz, t, NSA_GROUPS * width), BF16))
    outs = pl.pallas_call(
        functools.partial(_norm_proj_kernel, nsa_kv_start=nsa_kv_start),
        grid=(bsz, t // tm),
        in_specs=[
            pl.BlockSpec((1, tm, d), lambda b, i: (b, i, 0)),
            pl.BlockSpec((1, d), lambda b, i: (0, 0)),
            pl.BlockSpec((1, 1, d), lambda b, i: (b, 0, 0)),
            pl.BlockSpec((1, 1, d), lambda b, i: (b, 0, 0)),
            pl.BlockSpec((d, n), lambda b, i: (0, 0)),
        ],
        out_specs=out_specs,
        out_shape=out_shape,
        compiler_params=_cparams(("parallel", "parallel")),
        name="norm_proj",
    )(x, g.reshape(1, d), sc, sh, w)
    return outs[0] if nsa_kv_start is None else outs


def _ffn_kernel(*refs, final_norm, n_mix):
    mix_refs, wo_refs = refs[:n_mix], refs[n_mix:2 * n_mix]
    (x_ref, g1_ref, g_ref, sc_ref, sh_ref, gate_ref, wu_ref, wv_ref, cw_ref, cb_ref, wd_ref, fg_ref,
     o_ref, hn_ref, acc_ref, halo_ref, x1_ref) = refs[2 * n_mix:]
    ti = pl.program_id(1)
    fj = pl.program_id(2)
    tm, fk = acc_ref.shape[0], wu_ref.shape[1]

    @pl.when(fj == 0)
    def _():
        proj = _mm(mix_refs[0][0], wo_refs[0][...])
        for m_ref, w_ref in zip(mix_refs[1:], wo_refs[1:]):
            proj = proj + _mm(m_ref[0], w_ref[...])
        x1 = x_ref[0] + g1_ref[0] * proj
        x1_ref[...] = x1
        hn = _rms(x1, NORM_EPS) * g_ref[...]
        hn_ref[...] = (hn * (1.0 + sc_ref[0]) + sh_ref[0]).astype(BF16)
        acc_ref[...] = jnp.zeros_like(acc_ref)

    @pl.when(ti == 0)
    def _():
        halo_ref[fj] = jnp.zeros((8, fk), F32)

    hn = hn_ref[...]
    u = _dot(hn, wu_ref[...])
    v = _dot(hn, wv_ref[...])
    prev = halo_ref[fj]
    row = _iota((tm, fk), 0)
    u1 = jnp.where(row == 0, prev[7:8], pltpu.roll(u, 1, 0))
    u2 = jnp.where(row == 0, prev[6:7], jnp.where(row == 1, prev[7:8], pltpu.roll(u, 2, 0)))
    halo_ref[fj] = u[tm - 8:tm]
    cw = cw_ref[...]
    uc = cw[0:1] * u2 + cw[1:2] * u1 + cw[2:3] * u + cb_ref[...]
    h = _gelu_tanh(uc) * v
    acc_ref[...] += _dot(h.astype(BF16), wd_ref[...])

    @pl.when(fj == pl.num_programs(2) - 1)
    def _():
        y = x1_ref[...] + gate_ref[0] * acc_ref[...]
        if final_norm:
            y = _rms(y, NORM_EPS) * fg_ref[...]
        o_ref[0] = y


def _conv_ffn(mixes, w_out, x, g1, g, sc, sh, gate, w_up, conv_w, conv_b, w_down, final_g, *, final_norm, tm=512, fk=2816):
    bsz, t, d = x.shape
    f = w_down.shape[0]
    nf = f // fk
    widths = [m.shape[-1] for m in mixes]
    offs = np.cumsum([0] + widths).tolist()
    w_outs = [w_out[o:o + k] for o, k in zip(offs, widths)]
    kern = functools.partial(_ffn_kernel, final_norm=final_norm, n_mix=len(mixes))
    return pl.pallas_call(
        kern,
        grid=(bsz, t // tm, nf),
        in_specs=[pl.BlockSpec((1, tm, k), lambda b, i, j: (b, i, 0)) for k in widths]
        + [pl.BlockSpec((k, d), lambda b, i, j: (0, 0)) for k in widths]
        + [
            pl.BlockSpec((1, tm, d), lambda b, i, j: (b, i, 0)),
            pl.BlockSpec((1, 1, d), lambda b, i, j: (b, 0, 0)),
            pl.BlockSpec((1, d), lambda b, i, j: (0, 0)),
            pl.BlockSpec((1, 1, d), lambda b, i, j: (b, 0, 0)),
            pl.BlockSpec((1, 1, d), lambda b, i, j: (b, 0, 0)),
            pl.BlockSpec((1, 1, d), lambda b, i, j: (b, 0, 0)),
            pl.BlockSpec((d, fk), lambda b, i, j: (0, j)),
            pl.BlockSpec((d, fk), lambda b, i, j: (0, j + nf)),
            pl.BlockSpec((3, fk), lambda b, i, j: (0, j)),
            pl.BlockSpec((1, fk), lambda b, i, j: (0, j)),
            pl.BlockSpec((fk, d), lambda b, i, j: (j, 0)),
            pl.BlockSpec((1, d), lambda b, i, j: (0, 0)),
        ],
        out_specs=pl.BlockSpec((1, tm, d), lambda b, i, j: (b, i, 0)),
        out_shape=jax.ShapeDtypeStruct((bsz, t, d), F32),
        scratch_shapes=[pltpu.VMEM((tm, d), BF16), pltpu.VMEM((tm, d), F32), pltpu.VMEM((nf, 8, fk), F32),
                        pltpu.VMEM((tm, d), F32)],
        compiler_params=_cparams(("parallel", "arbitrary", "arbitrary")),
        name="conv_ffn",
    )(*mixes, *w_outs, x, g1, g.reshape(1, d), sc, sh, gate, w_up, w_up, conv_w, conv_b.reshape(1, f), w_down,
      final_g.reshape(1, d))


def _gla_kernel(q_ref, k_ref, v_ref, og_ref, lr_ref, aup_ref, ab_ref, gg_ref, o_ref,
                st_ref, b_scr, bend_scr, q_scr, k_scr, kd_scr, qg_scr, vt_scr, o_scr):
    tb = q_ref.shape[1]
    cs = GLA_CHUNK
    n_chunks = tb // cs

    @pl.when(pl.program_id(1) == 0)
    def _():
        st_ref[...] = jnp.zeros_like(st_ref)

    z = _mm3(lr_ref[0], aup_ref[...]) + ab_ref[...]
    la = -_softplus(-z) * (1.0 / GLA_GATE_NORM)
    rr, cc = _iota((tb, tb), 0), _iota((tb, tb), 1)
    same = (rr // cs) == (cc // cs)
    b = _sel_mm(jnp.where(same & (cc <= rr), 1.0, 0.0), la)
    bend = _sel_mm(jnp.where(same, 1.0, 0.0), la)
    q = q_ref[0] * (GLA_DK ** -0.5)
    k = k_ref[0]
    b_scr[...] = b
    bend_scr[...] = bend
    q_scr[...] = q
    k_scr[...] = k
    kd_scr[...] = (k * jnp.exp(bend - b)).astype(BF16)
    qg_scr[...] = q * jnp.exp(b)
    v_all = v_ref[0]
    for h in range(GLA_HEADS):
        vt_scr[h] = v_all[:, h * GLA_DV:(h + 1) * GLA_DV].T.astype(BF16)

    lane_c = _iota((cs, LANES), 1)
    row_c = _iota((cs, LANES), 0)
    lane_t = _iota((tb, LANES), 1)
    head_rows = _iota((LANES, LANES), 0) // GLA_DK

    def chunk(c, carry):
        r0 = c * cs
        tmask = (lane_t // cs) == c
        for p in range(GLA_HEADS // 2):
            lanes = slice(p * LANES, (p + 1) * LANES)
            bc = b_scr[pl.ds(r0, cs), lanes]
            qc = q_scr[pl.ds(r0, cs), lanes]
            kc = k_scr[pl.ds(r0, cs), lanes]
            blocks = []
            for s in range(cs):
                lo = s // 8 * 8
                m = row_c[lo:] >= s
                rel = jnp.where(m, bc[lo:] - bc[s:s + 1], 0.0)
                live = jnp.where(m, qc[lo:] * kc[s:s + 1] * jnp.exp(rel), 0.0)
                blocks.append(live if lo == 0 else jnp.concatenate([jnp.zeros((lo, LANES), F32), live], axis=0))
            a_st = jnp.concatenate(blocks, axis=0)
            qgc = qg_scr[pl.ds(r0, cs), lanes]
            lhs = jnp.concatenate([jnp.where(lane_c < GLA_DK, qgc, 0.0),
                                   jnp.where(lane_c >= GLA_DK, qgc, 0.0)], axis=0)
            st = st_ref[p]
            o_inter = _mm_nt(lhs, st)
            for hh in range(2):
                h = 2 * p + hh
                zsum = _mm_sel(a_st, jnp.where(head_rows == hh, 1.0, 0.0), terms=1)
                vc = v_ref[0, pl.ds(r0, cs), h * GLA_DV:(h + 1) * GLA_DV]
                o_h = o_inter[hh * cs:(hh + 1) * cs]
                for s in range(cs):
                    o_h = o_h + zsum[s * cs:(s + 1) * cs] * vc[s:s + 1]
                o_scr[pl.ds(r0, cs), h * GLA_DV:(h + 1) * GLA_DV] = o_h
            lhs_u = jnp.concatenate([jnp.where(tmask, vt_scr[2 * p], 0.0).astype(BF16),
                                     jnp.where(tmask, vt_scr[2 * p + 1], 0.0).astype(BF16)], axis=0)
            upd = _dot(lhs_u, kd_scr[:, lanes])
            upd = jnp.where(lane_t < GLA_DK, upd[:GLA_DV], upd[GLA_DV:])
            decay = jnp.exp(bend_scr[pl.ds(r0, 1), lanes])
            st_ref[p] = st * decay + upd
        return carry

    for c in range(n_chunks):
        chunk(c, 0)

    og = og_ref[0]
    for h in range(GLA_HEADS):
        sl = slice(h * GLA_DV, (h + 1) * GLA_DV)
        gate = og[:, sl]
        o_ref[0, :, sl] = (_rms(o_scr[:, sl], NORM_EPS) * gg_ref[...] * (gate * _sigmoid(gate))).astype(o_ref.dtype)


def _gla(p, a_up_pad, a_b, gla_g, *, tb=128):
    bsz, t, _ = p.shape
    qk = GLA_HEADS * GLA_DK
    vw = GLA_HEADS * GLA_DV
    assert tb == LANES and GLA_DV == LANES
    return pl.pallas_call(
        _gla_kernel,
        grid=(bsz, t // tb),
        in_specs=[
            pl.BlockSpec((1, tb, qk), lambda b, i: (b, i, 0)),
            pl.BlockSpec((1, tb, qk), lambda b, i: (b, i, 1)),
            pl.BlockSpec((1, tb, vw), lambda b, i: (b, i, 1)),
            pl.BlockSpec((1, tb, vw), lambda b, i: (b, i, 2)),
            pl.BlockSpec((1, tb, LANES), lambda b, i: (b, i, 24)),
            pl.BlockSpec((LANES, qk), lambda b, i: (0, 0)),
            pl.BlockSpec((1, qk), lambda b, i: (0, 0)),
            pl.BlockSpec((1, GLA_DV), lambda b, i: (0, 0)),
        ],
        out_specs=pl.BlockSpec((1, tb, vw), lambda b, i: (b, i, 0)),
        out_shape=jax.ShapeDtypeStruct((bsz, t, vw), BF16),
        scratch_shapes=[
            pltpu.VMEM((GLA_HEADS // 2, GLA_DV, LANES), F32),
            pltpu.VMEM((tb, qk), F32), pltpu.VMEM((tb, qk), F32),
            pltpu.VMEM((tb, qk), F32), pltpu.VMEM((tb, qk), F32),
            pltpu.VMEM((tb, qk), BF16), pltpu.VMEM((tb, qk), F32),
            pltpu.VMEM((GLA_HEADS, GLA_DV, tb), BF16),
            pltpu.VMEM((tb, vw), F32),
        ],
        compiler_params=_cparams(("parallel", "arbitrary")),
        name="gla",
    )(p, p, p, p, p, a_up_pad, a_b.reshape(1, qk), gla_g.reshape(1, GLA_DV))


def _rwkv_kernel(r_ref, k_ref, v_ref, wa_ref, gl_ref, mur_ref, muk_ref, muv_ref, muwa_ref, mugl_ref,
                 w0_ref, w2_ref, a0_ref, a2_ref, g2_ref, kkw_ref, ka_ref, rk_ref, gnw_ref, gnb_ref,
                 o_ref,
                 lr_scr, lk_scr, lv_scr, lwa_scr, lgl_scr, s_ref,
                 lw_p, r_p, kk_p, be_p, k2_p, v_p, y_p, pp_all, y0_all, ge_all, g_all, h_all):
    tb = r_ref.shape[1]
    cs = RWKV_CHUNK
    n_chunks = tb // cs
    n_pairs = RWKV_HEADS // 2
    ti = pl.program_id(1)
    lasts = (lr_scr, lk_scr, lv_scr, lwa_scr, lgl_scr)

    @pl.when(ti == 0)
    def _():
        s_ref[...] = jnp.zeros_like(s_ref)
        for ref in lasts:
            ref[...] = jnp.zeros_like(ref)

    def shifted(x_ref, last_ref, mu_ref):
        x = x_ref[0]
        row = _iota(x.shape, 0)
        prev = jnp.where(row == 0, last_ref[7:8], pltpu.roll(x, 1, 0))
        last_ref[...] = x[tb - 8:tb]
        return x + (prev - x) * mu_ref[...]

    r = shifted(r_ref, lr_scr, mur_ref)
    k = shifted(k_ref, lk_scr, muk_ref)
    v = shifted(v_ref, lv_scr, muv_ref)
    wa = shifted(wa_ref, lwa_scr, muwa_ref)
    gl = shifted(gl_ref, lgl_scr, mugl_ref)

    logw = -jnp.exp(-_softplus(-(w0_ref[...] + _mm3(jnp.tanh(wa), w2_ref[...]))) - 0.5)
    a = _sigmoid(a0_ref[...] + _mm3(wa, a2_ref[...]))
    g = _mm(_sigmoid(gl), g2_ref[...])
    seg = jnp.where((_iota((LANES, LANES), 0) // RWKV_N) == (_iota((LANES, LANES), 1) // RWKV_N), 1.0, 0.0)

    def segsum(x):
        return jnp.concatenate([_mm_sel(x[:, i * LANES:(i + 1) * LANES], seg, terms=2) for i in range(n_pairs)], axis=1)

    kk = k * kkw_ref[...]
    kk = kk * lax.rsqrt(jnp.maximum(segsum(kk * kk), 1e-24))
    k2 = k * (1.0 + (a - 1.0) * ka_ref[...])
    beta = kk * a
    for p in range(n_pairs):
        sl = slice(p * LANES, (p + 1) * LANES)
        lw_p[p] = logw[:, sl]
        r_p[p] = r[:, sl]
        kk_p[p] = kk[:, sl]
        be_p[p] = beta[:, sl]
        k2_p[p] = k2[:, sl]
        v_p[p] = v[:, sl]

    rr, cc = _iota((tb, tb), 0), _iota((tb, tb), 1)
    same = (rr // cs) == (cc // cs)
    tri_incl = same & (cc <= rr)
    tri_strict = same & (cc < rr)
    l_incl = jnp.where(tri_incl, 1.0, 0.0)
    l_all = jnp.where(same, 1.0, 0.0)
    eye = jnp.where(rr == cc, 1.0, 0.0)
    lane = _iota((tb, LANES), 1)
    half = lane < RWKV_N
    same_half = (rr // RWKV_N) == (cc // RWKV_N)
    lane_c = _iota((cs, LANES), 1)
    lane_s = _iota((RWKV_N, LANES), 1)

    def below_left(s):
        return ((rr // (2 * s)) == (cc // (2 * s))) & ((rr % (2 * s)) >= s) & ((cc % (2 * s)) < s)

    pairs = range(n_pairs)
    heads = [(p, hh) for p in pairs for hh in range(2)]
    l_both = jnp.concatenate([l_incl, l_all], axis=0)
    rt, bt, at, vp, ak = {}, {}, {}, {}, {}
    for p in pairs:
        lw = lw_p[p]
        sums = _sel_mm(l_both, lw)
        cum = sums[:tb]
        ge_all[p] = jnp.exp(sums[tb:])
        ig = jnp.exp(-cum)
        rt[p] = r_p[p] * jnp.exp(cum)
        bt[p] = kk_p[p] * jnp.exp(cum - lw)
        at[p] = -(be_p[p] * ig)
        vp[p] = v_p[p]
        ak[p] = jnp.concatenate([at[p], k2_p[p] * ig], axis=0)
    a_ab, a_ak, a_ra, a_rk = {}, {}, {}, {}
    for h in heads:
        p, hh = h
        hm = half if hh == 0 else jnp.logical_not(half)
        s_b = _mm3_nt(jnp.where(hm, bt[p], 0.0), ak[p])
        s_r = _mm_nt(jnp.where(hm, rt[p], 0.0), ak[p])
        a_ab[h] = jnp.where(tri_strict, s_b[:, :tb], 0.0)
        a_ak[h] = jnp.where(tri_strict, s_b[:, tb:], 0.0)
        a_ra[h] = jnp.where(tri_incl, s_r[:, :tb], 0.0)
        a_rk[h] = jnp.where(tri_incl, s_r[:, tb:], 0.0)
    m_inv = {h: eye + jnp.where(below_left(1), a_ab[h], 0.0) for h in heads}
    w1 = {h: _mm3(a_ak[h], vp[h[0]]) for h in heads}
    s = 2
    while s < cs:
        low = {h: _mm3(m_inv[h], jnp.where(below_left(s), a_ab[h], 0.0)) for h in heads}
        m_inv = {h: m_inv[h] + _mm3(low[h], m_inv[h]) for h in heads}
        s *= 2
    mw = {h: _mm3(m_inv[h], jnp.concatenate([w1[h], bt[h[0]]], axis=1)) for h in heads}
    p_h = {h: _mm(a_ra[h], mw[h][:, LANES:]) for h in heads}
    y0_h = {h: _mm(a_ra[h], mw[h][:, :LANES]) + _mm(a_rk[h], vp[h[0]]) for h in heads}
    zeros = jnp.zeros((tb, LANES), F32)
    for p in pairs:
        u0 = jnp.where(half, mw[(p, 0)][:, :LANES], mw[(p, 1)][:, :LANES])
        mb = jnp.where(half, mw[(p, 0)][:, LANES:], mw[(p, 1)][:, LANES:])
        pp_all[p] = rt[p] + jnp.where(half, p_h[(p, 0)], p_h[(p, 1)])
        y0_all[p] = jnp.where(half, y0_h[(p, 0)], y0_h[(p, 1)])
        u0t, vt, mbt = u0.T, vp[p].T, mb.T
        for c in range(n_chunks):
            tmask = (lane // cs) == c
            lhs = jnp.concatenate([
                jnp.concatenate([jnp.where(tmask, u0t, 0.0), jnp.where(tmask, vt, 0.0)], axis=1),
                jnp.concatenate([jnp.where(tmask, mbt, 0.0), zeros], axis=1)], axis=0)
            hg = _mm3(lhs, ak[p])
            h_all[p * n_chunks + c] = jnp.where(lane_s < RWKV_N, hg[:RWKV_N], hg[RWKV_N:tb])
            g_all[p * n_chunks + c] = jnp.where(same_half, hg[tb:], 0.0) + eye

    state = [s_ref[p] for p in range(n_pairs)]
    for c in range(n_chunks):
        rows = slice(c * cs, (c + 1) * cs)
        for p in range(n_pairs):
            sp = state[p]
            pc = pp_all[p, rows, :]
            lhs_y = jnp.concatenate([jnp.where(lane_c < RWKV_N, pc, 0.0), jnp.where(lane_c >= RWKV_N, pc, 0.0)], axis=0)
            yy = _mm_nt(lhs_y, jnp.concatenate([sp, sp], axis=0))
            y_p[p, rows, :] = jnp.where(lane_c < RWKV_N, yy[:cs], yy[cs:]) + y0_all[p, rows, :]
            state[p] = (_mm3(sp, g_all[p * n_chunks + c]) + h_all[p * n_chunks + c]) * ge_all[p, c * cs:c * cs + 1, :]
    for p in range(n_pairs):
        s_ref[p] = state[p]

    y = jnp.concatenate([y_p[p] for p in range(n_pairs)], axis=1)
    mu = segsum(y) * (1.0 / RWKV_N)
    yc = y - mu
    var = segsum(yc * yc) * (1.0 / RWKV_N)
    yn = yc * lax.rsqrt(var + RWKV_GN_EPS) * gnw_ref[...] + gnb_ref[...]
    bonus = segsum(r * k2 * rk_ref[...]) * v
    o_ref[0] = ((yn + bonus) * g).astype(o_ref.dtype)


def _rwkv(p, mus, w0, w2p, a0, a2p, g2, k_k, k_a, r_k, gn_w, gn_b, *, tb=128):
    bsz, t, _ = p.shape
    w = RWKV_HEADS * RWKV_N
    n_pairs = RWKV_HEADS // 2
    assert tb == LANES
    row = lambda a: a.reshape(1, -1)
    full = lambda shape: pl.BlockSpec(shape, lambda b, i: (0,) * len(shape))
    ptile = lambda: pltpu.VMEM((n_pairs, tb, LANES), F32)
    return pl.pallas_call(
        _rwkv_kernel,
        grid=(bsz, t // tb),
        in_specs=[
            pl.BlockSpec((1, tb, w), lambda b, i: (b, i, 3)),
            pl.BlockSpec((1, tb, w), lambda b, i: (b, i, 4)),
            pl.BlockSpec((1, tb, w), lambda b, i: (b, i, 5)),
            pl.BlockSpec((1, tb, LANES), lambda b, i: (b, i, 25)),
            pl.BlockSpec((1, tb, LANES), lambda b, i: (b, i, 26)),
            full((1, w)), full((1, w)), full((1, w)), full((1, LANES)), full((1, LANES)),
            full((1, w)), full((LANES, w)), full((1, w)), full((LANES, w)), full((LANES, w)),
            full((1, w)), full((1, w)), full((1, w)), full((1, w)), full((1, w)),
        ],
        out_specs=pl.BlockSpec((1, tb, w), lambda b, i: (b, i, 0)),
        out_shape=jax.ShapeDtypeStruct((bsz, t, w), BF16),
        scratch_shapes=[
            pltpu.VMEM((8, w), F32), pltpu.VMEM((8, w), F32), pltpu.VMEM((8, w), F32),
            pltpu.VMEM((8, LANES), F32), pltpu.VMEM((8, LANES), F32),
            pltpu.VMEM((n_pairs, RWKV_N, LANES), F32),
            ptile(), ptile(), ptile(), ptile(), ptile(), ptile(), ptile(), ptile(), ptile(), ptile(),
            pltpu.VMEM((n_pairs * (tb // RWKV_CHUNK), LANES, LANES), F32),
            pltpu.VMEM((n_pairs * (tb // RWKV_CHUNK), RWKV_N, LANES), F32),
        ],
        compiler_params=_cparams(("parallel", "arbitrary")),
        name="rwkv7",
    )(p, p, p, p, p, *[row(m) for m in mus], row(w0), w2p, row(a0), a2p, g2,
      row(k_k), row(k_a), row(r_k), row(gn_w), row(gn_b))


def _cmp_kernel(sk_ref, sv_ref, wak_ref, wbk_ref, wav_ref, wbv_ref, pek_ref, pev_ref, w1k_ref, w1v_ref,
                w2k_ref, w2v_ref, ok_ref, ov_ref):
    def one(seg_ref, wa_ref, wb_ref, pe_ref, w1_ref, w2_ref, o_ref):
        seg = seg_ref[0].astype(BF16)
        first = _dot(seg, wa_ref[...])
        second = _dot(seg, wb_ref[...])
        n = first.shape[0]
        pe_term = _mm3(pe_ref[...], w1_ref[...])[0:1]
        hidden = _gelu_tanh(first + pltpu.roll(second, n - 1, 0) + pe_term)
        o_ref[0] = _mm(hidden, w2_ref[...])

    one(sk_ref, wak_ref, wbk_ref, pek_ref, w1k_ref, w2k_ref, ok_ref)
    one(sv_ref, wav_ref, wbv_ref, pev_ref, w1v_ref, w2v_ref, ov_ref)


def _nsa_compress(kc_tok, vc_tok, pe_k, w1_k, w2_k, pe_v, w1_v, w2_v):
    bsz, t, gw = kc_tok.shape
    st, dh, hid, g = NSA_CMP_STRIDE, NSA_DH, NSA_CMP_HIDDEN, NSA_GROUPS
    nseg = t // st
    eye = jnp.eye(g, dtype=F32)

    def expand_w1(w1):
        w = w1.reshape(NSA_CMP_LEN, dh, hid)
        big = jnp.einsum('ldc,gh->lgdhc', w, eye).reshape(NSA_CMP_LEN * g * dh, g * hid)
        half = st * g * dh
        return big[:half].astype(BF16), big[half:].astype(BF16)

    def expand_w2(w2):
        return jnp.einsum('cd,gh->gchd', w2, eye).reshape(g * hid, g * dh).astype(BF16)

    def pe_rows(pe):
        return jnp.zeros((8, NSA_CMP_LEN * dh), F32).at[0].set(pe.reshape(-1))

    wak, wbk = expand_w1(w1_k)
    wav, wbv = expand_w1(w1_v)
    full = lambda shape: pl.BlockSpec(shape, lambda b: (0,) * len(shape))
    seg_spec = pl.BlockSpec((1, nseg, st * gw), lambda b: (b, 0, 0))
    out_spec = pl.BlockSpec((1, nseg, gw), lambda b: (b, 0, 0))
    return pl.pallas_call(
        _cmp_kernel,
        grid=(bsz,),
        in_specs=[seg_spec, seg_spec,
                  full(wak.shape), full(wbk.shape), full(wav.shape), full(wbv.shape),
                  full((8, NSA_CMP_LEN * dh)), full((8, NSA_CMP_LEN * dh)),
                  full((NSA_CMP_LEN * dh, g * hid)), full((NSA_CMP_LEN * dh, g * hid)),
                  full((g * hid, gw)), full((g * hid, gw))],
        out_specs=[out_spec, out_spec],
        out_shape=[jax.ShapeDtypeStruct((bsz, nseg, gw), F32)] * 2,
        compiler_params=_cparams(("parallel",)),
        name="nsa_compress",
    )(kc_tok.reshape(bsz, nseg, st * gw), vc_tok.reshape(bsz, nseg, st * gw),
      wak, wbk, wav, wbv, pe_rows(pe_k), pe_rows(pe_v),
      jnp.tile(w1_k, (1, g)), jnp.tile(w1_v, (1, g)), expand_w2(w2_k), expand_w2(w2_v))


def _nsa_attn_kernel(q_ref, kc_ref, vc_ref, ks_ref, vs_ref, kw_ref, vw_ref, gt_ref, sl_ref, o_ref, acc_scr):
    g = pl.program_id(1)
    qi = pl.program_id(2)
    qb, dh, rep = NSA_Q_BLOCK, NSA_DH, NSA_REP
    n_cmp_pad = kc_ref.shape[1]
    kt = LANES
    nb = q_ref.shape[1] // qb
    rows_all = nb * rep * qb

    lane_q = _iota((1, LANES), 1)
    lane_b = _iota((qb, LANES), 1)
    slope_cols = []
    for r in range(rep):
        s1, s2, s3 = (t.astype(F32) for t in _split3(sl_ref[0, r:r + 1, :]))
        slope_cols.append(jnp.where((lane_q >= dh) & (lane_q < dh + 2), s1,
                                    jnp.where((lane_q >= dh + 2) & (lane_q < dh + 4), s2,
                                              jnp.where((lane_q >= dh + 4) & (lane_q < dh + 6), s3, 0.0))))
    q_rows = []
    for blk in range(nb):
        for r in range(rep):
            two_heads = q_ref[0, blk * qb:(blk + 1) * qb, (r // 2) * LANES:(r // 2 + 1) * LANES]
            if r % 2:
                two_heads = pltpu.roll(two_heads, dh, 1)
            q_rows.append(jnp.where(lane_b < dh, two_heads * (dh ** -0.5), slope_cols[r]))
    q = jnp.concatenate(q_rows, axis=0).astype(BF16)
    t0 = qi * (nb * qb)

    def row_pos(shape):
        rows = _iota(shape, 0)
        return rows // (rep * qb) * qb + rows % qb

    ahead_of_lane = row_pos((rows_all, kt)) - _iota((rows_all, kt), 1)

    def lane_tiles(s):
        return [s[:, j * kt:(j + 1) * kt] for j in range(s.shape[1] // kt)]

    def tile_max(tiles, start):
        m = start
        for tile in tiles:
            m = jnp.maximum(m, tile)
        return m

    def normalised(acc):
        return acc / acc[:, dh:dh + 1]

    gsel = jnp.where(_iota((LANES, LANES), 0) == g * (rep * 3) + _iota((LANES, LANES), 1), 1.0, 0.0)
    gates = _sigmoid(_mm_sel(gt_ref[0], gsel))

    n_idx = _iota((rows_all, n_cmp_pad), 1)
    valid_c = ((t0 + row_pos((rows_all, n_cmp_pad)) >= n_idx * NSA_CMP_STRIDE + NSA_CMP_LEN - 1)
               & (n_idx < n_cmp_pad - 1))
    s_cmp = jnp.where(valid_c, _dot_nt(q, kc_ref[0]), NEG_BIG)
    e_cmp = jnp.where(valid_c, jnp.exp(s_cmp - jnp.max(s_cmp, axis=-1, keepdims=True)), 0.0)
    den = jnp.sum(e_cmp, axis=-1, keepdims=True)
    p_cmp = e_cmp / jnp.where(den > 0.0, den, 1.0)
    o_cmp = _mm(p_cmp, vc_ref[0])
    n_sel_blocks = ks_ref.shape[1] // NSA_SEL_LEN
    on, oj = _iota((n_cmp_pad, LANES), 0), _iota((n_cmp_pad, LANES), 1)
    overlap = jnp.where((on * NSA_CMP_STRIDE <= oj * NSA_SEL_LEN + NSA_SEL_LEN - 1)
                        & (on * NSA_CMP_STRIDE + NSA_CMP_LEN - 1 >= oj * NSA_SEL_LEN)
                        & (oj < n_sel_blocks) & (on < n_cmp_pad - 1), 1.0, 0.0)
    p_groups = []
    for blk in range(nb):
        rows = [slice((blk * rep + r) * qb, (blk * rep + r + 1) * qb) for r in range(rep)]
        p_sum = p_cmp[rows[0]]
        for rs in rows[1:]:
            p_sum = p_sum + p_cmp[rs]
        p_groups.append(p_sum)
    imp = _mm_sel(jnp.concatenate(p_groups, axis=0), overlap)

    nq = nb * qb
    imp_t = imp.T[:n_sel_blocks]
    jj = _iota((n_sel_blocks, nq), 0)
    jf = jj.astype(F32)
    ahead = (t0 + _iota((n_sel_blocks, nq), 1)) // NSA_SEL_LEN - jj
    valid_b = ahead >= 0
    forced = (jj == 0) | (valid_b & (ahead < NSA_N_LOCAL))
    score = jnp.where(valid_b, imp_t + jnp.where(forced, NSA_FORCE, 0.0), -NSA_FORCE)
    sel = jnp.zeros((n_sel_blocks, nq), F32)
    for _ in range(NSA_N_SEL):
        best = jnp.max(score, axis=0, keepdims=True)
        first = jnp.min(jnp.where(score == best, jf, float(n_sel_blocks)), axis=0, keepdims=True)
        pick = jf == first
        sel = jnp.where(pick, 1.0, sel)
        score = jnp.where(pick, NEG_BIG, score)
    sel_q = jnp.concatenate([sel, jnp.zeros((LANES - n_sel_blocks, nq), F32)], axis=0).T.astype(BF16)

    w_tiles = NSA_WINDOW // kt
    win_s, win_v = [], []
    for rel in range(-w_tiles, nb):
        kb = qi * nb + rel
        k0 = pl.multiple_of(jnp.maximum(kb, 0) * kt, kt)
        s = _dot_nt(q, kw_ref[0, pl.ds(k0, kt), :])
        if not all(-w_tiles < rel - blk < 0 for blk in range(nb)):
            dist = ahead_of_lane - rel * kt
            s = jnp.where((dist >= 0) & (dist < NSA_WINDOW), s, NEG_BIG)
        if rel < 0:
            s = s + jnp.where(kb >= 0, 0.0, NEG_BIG)
        win_s.append(s)
        win_v.append(vw_ref[0, pl.ds(k0, kt), :])
    m_win = jnp.broadcast_to(jnp.max(tile_max(win_s[1:], win_s[0]), axis=-1, keepdims=True), (rows_all, kt))
    e_win = jnp.concatenate([jnp.exp(s - m_win).astype(BF16) for s in win_s], axis=1)
    o_win = normalised(_dot(e_win, jnp.concatenate(win_v, axis=0)))

    def gate_of(rb, branch):
        blk, r = divmod(rb, rep)
        col = 3 * r + branch
        return jnp.broadcast_to(gates[blk * qb:(blk + 1) * qb, col:col + 1], (qb, LANES))

    row_blocks = [slice(rb * qb, (rb + 1) * qb) for rb in range(nb * rep)]
    partial = [gate_of(rb, 0) * o_cmp[rs] + gate_of(rb, 2) * o_win[rs] for rb, rs in enumerate(row_blocks)]
    gate_sel = jnp.concatenate([gate_of(rb, 1) for rb in range(nb * rep)], axis=0)

    tpg = NSA_SWEEP_TILES
    kg = tpg * kt
    assert tpg % nb == 0

    sel_bias = ((sel_q.astype(F32) - 1.0) * (2.0 ** 100)).astype(BF16)
    q_sel = jnp.concatenate([q, jnp.concatenate([sel_bias[blk * qb:(blk + 1) * qb] for blk in range(nb)
                                                 for _ in range(rep)], axis=0)], axis=1)
    n_past = (qi * nb) // tpg
    first_off = (n_past * tpg - qi * nb) * kt

    def sweep(n_before):
        def run():
            m_run, acc = None, None
            for gi in range(n_before + 1):
                tiles = lane_tiles(_dot_nt(q_sel, ks_ref[0, gi * kg:(gi + 1) * kg, :]))
                if gi == n_before:
                    tiles = [jnp.where(ahead_of_lane >= first_off + j * kt, tile, NEG_BIG) for j, tile in enumerate(tiles)]
                m_grp = jnp.broadcast_to(jnp.max(tile_max(tiles[1:], tiles[0]), axis=-1, keepdims=True), (rows_all, kt))
                m_new = m_grp if m_run is None else jnp.maximum(m_run, m_grp)
                e = jnp.concatenate([jnp.exp(tile - m_new).astype(BF16) for tile in tiles], axis=1)
                pv = _dot(e, vs_ref[0, gi * kg:(gi + 1) * kg, :])
                acc = pv if acc is None else jnp.exp(m_run - m_new) * acc + pv
                m_run = m_new
            acc_scr[...] = gate_sel * normalised(acc)
        return run

    lax.switch(n_past, [sweep(n) for n in range(ks_ref.shape[1] // kg)])
    mixed = [partial[rb] + acc_scr[rs, :] for rb, rs in enumerate(row_blocks)]
    o_ref[0] = jnp.concatenate(
        [jnp.concatenate([jnp.where(lane_b < dh, mixed[blk * rep + r], pltpu.roll(mixed[blk * rep + r + 1], dh, 1))
                          for r in range(0, rep, 2)], axis=1) for blk in range(nb)], axis=0).astype(o_ref.dtype)


def _nsa_attention(p, kcmp, vcmp, ks, vs, kw, vw, slopes):
    bsz, t, _ = p.shape
    g, rep, dh = NSA_GROUPS, NSA_REP, NSA_DH
    qs = NSA_BLOCKS_PER_STEP * NSA_Q_BLOCK
    whole = lambda a: pl.BlockSpec((1, a.shape[1], a.shape[2] // g), lambda b, gi, i: (b, 0, gi))
    gate_col = p.shape[-1] // LANES - 1
    return pl.pallas_call(
        _nsa_attn_kernel,
        grid=(bsz, g, t // qs),
        in_specs=[
            pl.BlockSpec((1, qs, rep * dh), lambda b, gi, i: (b, i, gi)),
            whole(kcmp), whole(vcmp), whole(ks), whole(vs), whole(kw), whole(vw),
            pl.BlockSpec((1, qs, LANES), lambda b, gi, i: (b, i, gate_col)),
            pl.BlockSpec((1, 8, LANES), lambda b, gi, i: (gi, 0, 0)),
        ],
        out_specs=pl.BlockSpec((1, qs, rep * dh), lambda b, gi, i: (b, i, gi)),
        out_shape=jax.ShapeDtypeStruct((bsz, t, g * rep * dh), BF16),
        scratch_shapes=[pltpu.VMEM((rep * qs, LANES), F32)],
        compiler_params=_cparams(("parallel", "parallel", "arbitrary")),
        name="nsa_attention",
    )(p, kcmp, vcmp, ks, vs, kw, vw, p, slopes)


def _even_mixer(x, norm_g, sc, sh, w_in, shift_mu, a_up, a_b, gla_g, w0, w2, a0, a2, g2, k_k, k_a, r_k, gn_w, gn_b):
    d = x.shape[-1]
    qk, vw, w = GLA_HEADS * GLA_DK, GLA_HEADS * GLA_DV, RWKV_HEADS * RWKV_N
    gla_cols = 2 * qk + 2 * vw + GLA_LOWRANK
    wg, wr = w_in[:, :gla_cols], w_in[:, gla_cols:]
    o_r, o_wl, o_k, o_v, o_al, o_gl = np.cumsum([0, w, RWKV_W_LORA, w, w, RWKV_A_LORA]).tolist()
    pad = lambda a, n: jnp.pad(a, ((0, 0), (0, n - a.shape[1])))
    w_perm = jnp.concatenate([
        wg[:, :2 * qk + 2 * vw],
        wr[:, o_r:o_r + w], wr[:, o_k:o_k + w], wr[:, o_v:o_v + w],
        pad(wg[:, 2 * qk + 2 * vw:], LANES),
        wr[:, o_wl:o_wl + RWKV_W_LORA], wr[:, o_al:o_al + RWKV_A_LORA],
        wr[:, o_gl:o_gl + RWKV_G_LORA]], axis=1).astype(BF16)
    p = _norm_proj(x, norm_g, sc, sh, w_perm)
    a_up_pad = jnp.zeros((LANES, qk), F32).at[:GLA_LOWRANK].set(a_up)
    o_gla = _gla(p, a_up_pad, a_b, gla_g)
    mu = shift_mu
    mus = [mu[o_r:o_r + w], mu[o_k:o_k + w], mu[o_v:o_v + w],
           jnp.concatenate([mu[o_wl:o_wl + RWKV_W_LORA], mu[o_al:o_al + RWKV_A_LORA]]), mu[o_gl:o_gl + RWKV_G_LORA]]
    w2p = jnp.zeros((LANES, w), F32).at[:RWKV_W_LORA].set(w2)
    a2p = jnp.zeros((LANES, w), F32).at[RWKV_W_LORA:RWKV_W_LORA + RWKV_A_LORA].set(a2)
    o_rw = _rwkv(p, mus, w0, w2p, a0, a2p, g2, k_k, k_a, r_k.reshape(-1), gn_w, gn_b)
    return [o_gla, o_rw]


def _nsa_mixer(x, norm_g, sc, sh, w_in, pe_k, w1_k, w2_k, pe_v, w1_v, w2_v):
    bsz, t, d = x.shape
    g, dh, heads = NSA_GROUPS, NSA_DH, NSA_HEADS
    n_cols = w_in.shape[1]
    n_pad = -(-n_cols // (3 * LANES)) * (3 * LANES)
    w_pad = jnp.pad(w_in, ((0, 0), (0, n_pad - n_cols))).astype(BF16)
    kv = g * dh
    off = heads * dh
    p, ks, vs, kw, vw = _norm_proj(x, norm_g, sc, sh, w_pad, nsa_kv_start=off + 2 * kv)
    seg = lambda i: p[..., off + i * kv: off + (i + 1) * kv]
    kcmp, vcmp = _nsa_compress(seg(0), seg(1), pe_k, w1_k, w2_k, pe_v, w1_v, w2_v)
    slopes = 2.0 ** (-8.0 * jnp.arange(1, heads + 1, dtype=F32) / heads)
    slopes = jnp.broadcast_to(jnp.pad(slopes.reshape(g, NSA_REP), ((0, 0), (0, 8 - NSA_REP)))[:, :, None], (g, 8, LANES))

    def per_group(a, *cols):
        rows = a.shape[1]
        const = [jnp.broadcast_to(c.astype(BF16)[None], (bsz, rows, c.shape[-1])) for c in cols]
        const.append(jnp.zeros((bsz, rows, LANES - dh - sum(c.shape[-1] for c in cols)), BF16))
        parts = []
        for gi in range(g):
            parts += [a[..., gi * dh:(gi + 1) * dh].astype(BF16)] + const
        return jnp.concatenate(parts, axis=-1)

    cmp_end = jnp.arange(kcmp.shape[1]) * NSA_CMP_STRIDE + NSA_CMP_LEN - 1
    cmp_end_cols = jnp.tile(jnp.stack([cmp_end // LANES * LANES, cmp_end % LANES], axis=-1), (1, 3))
    return [_nsa_attention(p, per_group(kcmp, cmp_end_cols), per_group(vcmp), ks, vs, kw, vw, slopes)]


def kernel(x, c, ada_w, ada_b, norm1_g, norm2_g, ffn_w_up, ffn_conv_w, ffn_conv_b, ffn_w_down, ev_w_in, ev_shift_mu, gla_a_up, gla_a_b, gla_norm_g, rw_w0, rw_w2, rw_a0, rw_a2, rw_g2, rw_k_k, rw_k_a, rw_r_k, rw_gn_w, rw_gn_b, ev_w_out, od_w_in, cmp_pe_k, cmp_w1_k, cmp_w2_k, cmp_pe_v, cmp_w1_v, cmp_w2_v, od_w_out, final_norm_g):
    bsz, t, d = x.shape
    depth = ada_w.shape[0]
    mod = _ada_mod(c, ada_w, ada_b)
    for layer in range(depth):
        sh1, sc1, g1, sh2, sc2, g2 = (mod[layer, :, i * d:(i + 1) * d].reshape(bsz, 1, d) for i in range(6))
        i = layer // 2
        if layer % 2 == 0:
            mix = _even_mixer(x, norm1_g[layer], sc1, sh1, ev_w_in[i], ev_shift_mu[i], gla_a_up[i], gla_a_b[i],
                              gla_norm_g[i], rw_w0[i], rw_w2[i], rw_a0[i], rw_a2[i], rw_g2[i], rw_k_k[i], rw_k_a[i],
                              rw_r_k[i], rw_gn_w[i], rw_gn_b[i])
            w_out = ev_w_out[i]
        else:
            mix = _nsa_mixer(x, norm1_g[layer], sc1, sh1, od_w_in[i], cmp_pe_k[i], cmp_w1_k[i], cmp_w2_k[i],
                             cmp_pe_v[i], cmp_w1_v[i], cmp_w2_v[i])
            w_out = od_w_out[i]
        x = _conv_ffn(mix, w_out.astype(BF16), x, g1, norm2_g[layer], sc2, sh2, g2, ffn_w_up[layer].astype(BF16),
                      ffn_conv_w[layer], ffn_conv_b[layer], ffn_w_down[layer].astype(BF16), final_norm_g,
                      final_norm=(layer == depth - 1))
    return x
```
